```python
import jax, jax.numpy as jnp
from jax import lax
import numpy as np

D_MODEL = 1024
BATCH = 4
SEQ = 4096
DEPTH = 2

GRID_W = 64
CTX_LEN = 256
N_MIXERS = 2
N_HEADS = 16
HEAD_DIM = D_MODEL // N_HEADS
WIN_ROWS = 8
WIN_COLS = 16
FOURIER_GROUPS = 4
GROUP_DIM = D_MODEL // FOURIER_GROUPS
D_FF = 4 * D_MODEL
N_ATTN_LAYERS = (DEPTH + 1) // 2
N_FOURIER_LAYERS = DEPTH // 2
EPS = 1e-6

kernel_name = "hybrid_natten_fnet_dit_block"


def _rmsnorm(x, g):
    x32 = x.astype(jnp.float32)
    y = x32 * lax.rsqrt(jnp.mean(x32 * x32, axis=-1, keepdims=True) + EPS)
    return (y * g.astype(jnp.float32)).astype(x.dtype)


def _modulate(h, shift, scale):
    return h * (1 + scale) + shift


def _squared_relu_mlp(h, w1, w2):
    return jnp.square(jax.nn.relu(h @ w1)) @ w2


def _fourier_mix(h, w):
    b, n, d = h.shape
    hg = h.astype(jnp.float32).reshape(b, n, FOURIER_GROUPS, GROUP_DIM)
    f = jnp.fft.fft2(hg, axes=(1, 3), norm="ortho").real
    return f.reshape(b, n, d).astype(h.dtype) @ w


def _neighbourhood_attention(h, hc, wqkv, wo, gq, gk, rpb, ctx_out):
    b, n, d = h.shape
    rows = n // GRID_W
    kh = min(WIN_ROWS, rows)
    scale = HEAD_DIM ** -0.5

    def proj(t):
        qkv = (t @ wqkv).reshape(t.shape[0], t.shape[1], 3, N_HEADS, HEAD_DIM)
        return _rmsnorm(qkv[:, :, 0], gq), _rmsnorm(qkv[:, :, 1], gk), qkv[:, :, 2]

    q, k, v = proj(h)
    qc, kc, vc = proj(hc)
    q_grid = q.reshape(b, rows, GRID_W, N_HEADS, HEAD_DIM)

    cols = jnp.arange(GRID_W)
    col_start = jnp.clip(cols - WIN_COLS // 2, 0, GRID_W - WIN_COLS)
    col_ok = (cols[None, :] >= col_start[:, None]) & (cols[None, :] < col_start[:, None] + WIN_COLS)
    dc_idx = jnp.clip(cols[None, :] - cols[:, None] + WIN_COLS - 1, 0, 2 * WIN_COLS - 2)

    def row_block(r):
        rs = jnp.clip(r - kh // 2, 0, rows - kh)
        k_win = lax.dynamic_slice_in_dim(k, rs * GRID_W, kh * GRID_W, axis=1).reshape(b, kh, GRID_W, N_HEADS, HEAD_DIM)
        v_win = lax.dynamic_slice_in_dim(v, rs * GRID_W, kh * GRID_W, axis=1).reshape(b, kh, GRID_W, N_HEADS, HEAD_DIM)
        q_r = lax.dynamic_index_in_dim(q_grid, r, axis=1, keepdims=False)
        dr = rs + jnp.arange(kh) - r
        bias = jnp.take(jnp.take(rpb, dr + WIN_ROWS - 1, axis=1), dc_idx, axis=2)
        bias = jnp.transpose(bias, (0, 2, 1, 3)).astype(jnp.float32)
        s_lat = jnp.einsum('bqhd,bikhd->bhqik', q_r, k_win).astype(jnp.float32) * scale + bias
        s_lat = jnp.where(col_ok[:, None, :], s_lat, -jnp.inf)
        s_ctx = jnp.einsum('bqhd,bkhd->bhqk', q_r, kc).astype(jnp.float32) * scale
        s = jnp.concatenate([s_lat.reshape(b, N_HEADS, GRID_W, kh * GRID_W), s_ctx], axis=-1)
        p = jax.nn.softmax(s, axis=-1).astype(v.dtype)
        p_lat = p[..., :kh * GRID_W].reshape(b, N_HEADS, GRID_W, kh, GRID_W)
        p_ctx = p[..., kh * GRID_W:]
        return (jnp.einsum('bhqik,bikhd->bqhd', p_lat, v_win)
                + jnp.einsum('bhqk,bkhd->bqhd', p_ctx, vc))

    o = lax.map(row_block, jnp.arange(rows))
    y = jnp.moveaxis(o, 0, 1).reshape(b, n, d) @ wo
    if not ctx_out:
        return y, None
    s_cc = jnp.einsum('bqhd,bkhd->bhqk', qc, kc).astype(jnp.float32) * scale
    p_cc = jax.nn.softmax(s_cc, axis=-1).astype(vc.dtype)
    oc = jnp.einsum('bhqk,bkhd->bqhd', p_cc, vc).reshape(hc.shape[0], hc.shape[1], d)
    return y, oc @ wo


def setup_inputs(seed: int = 0) -> dict:
    key = jax.random.key(seed)
    ks = jax.random.split(key, 15)
    nrm = jax.random.normal
    return {
        "x": nrm(ks[0], (BATCH, SEQ, D_MODEL), jnp.float32),
        "c": nrm(ks[1], (BATCH, D_MODEL), jnp.float32),
        "ctx": nrm(ks[2], (BATCH, CTX_LEN, D_MODEL), jnp.float32),
        "c_ctx": nrm(ks[3], (D_MODEL,), jnp.float32),
        "ada_w": nrm(ks[4], (DEPTH, D_MODEL, 6 * D_MODEL), jnp.float32) * D_MODEL ** -0.5,
        "ada_b": 0.02 * nrm(ks[5], (DEPTH, 6 * D_MODEL), jnp.float32),
        "norm_g": 1.0 + 0.05 * nrm(ks[6], (DEPTH, 2, D_MODEL), jnp.float32),
        "attn_wqkv": nrm(ks[7], (N_ATTN_LAYERS, D_MODEL, 3 * D_MODEL), jnp.float32) * D_MODEL ** -0.5,
        "attn_wo": nrm(ks[8], (N_ATTN_LAYERS, D_MODEL, D_MODEL), jnp.float32) * D_MODEL ** -0.5,
        "q_norm_g": 1.0 + 0.05 * nrm(ks[9], (N_ATTN_LAYERS, HEAD_DIM), jnp.float32),
        "k_norm_g": 1.0 + 0.05 * nrm(ks[10], (N_ATTN_LAYERS, HEAD_DIM), jnp.float32),
        "rpb": 0.1 * nrm(ks[11], (N_ATTN_LAYERS, N_HEADS, 2 * WIN_ROWS - 1, 2 * WIN_COLS - 1), jnp.float32),
        "fourier_w": nrm(ks[12], (N_FOURIER_LAYERS, D_MODEL, D_MODEL), jnp.float32) * D_MODEL ** -0.5,
        "mlp_w1": nrm(ks[13], (DEPTH, D_MODEL, D_FF), jnp.float32) * D_MODEL ** -0.5,
        "mlp_w2": nrm(ks[14], (DEPTH, D_FF, D_MODEL), jnp.float32) * D_FF ** -0.5,
    }


def reference(x, c, ctx, c_ctx, ada_w, ada_b, norm_g, attn_wqkv, attn_wo, q_norm_g, k_norm_g, rpb,
              fourier_w, mlp_w1, mlp_w2):
    for layer in range(DEPTH):
        is_attn = layer % N_MIXERS == 0
        idx = layer // N_MIXERS
        need_ctx = layer < DEPTH - 1
        uses_ctx = need_ctx or is_attn

        mods = jnp.split(jax.nn.silu(c) @ ada_w[layer] + ada_b[layer], 6, axis=-1)
        sh1, sc1, g1, sh2, sc2, g2 = [m[:, None, :] for m in mods]
        h = _modulate(_rmsnorm(x, norm_g[layer, 0]), sh1, sc1)
        if uses_ctx:
            csh1, csc1, cg1, csh2, csc2, cg2 = jnp.split(
                jax.nn.silu(c_ctx) @ ada_w[layer] + ada_b[layer], 6, axis=-1)
            hc = _modulate(_rmsnorm(ctx, norm_g[layer, 0]), csh1, csc1)

        if is_attn:
            y, yc = _neighbourhood_attention(h, hc, attn_wqkv[idx], attn_wo[idx], q_norm_g[idx],
                                             k_norm_g[idx], rpb[idx], need_ctx)
        else:
            y = _fourier_mix(h, fourier_w[idx])
            yc = _fourier_mix(hc, fourier_w[idx]) if need_ctx else None

        x = x + g1 * y
        x = x + g2 * _squared_relu_mlp(_modulate(_rmsnorm(x, norm_g[layer, 1]), sh2, sc2),
                                       mlp_w1[layer], mlp_w2[layer])
        if need_ctx:
            ctx = ctx + cg1 * yc
            ctx = ctx + cg2 * _squared_relu_mlp(_modulate(_rmsnorm(ctx, norm_g[layer, 1]), csh2, csc2),
                                               mlp_w1[layer], mlp_w2[layer])
    return x
```

```python
import functools
import math

import numpy as np
import jax
import jax.numpy as jnp
from jax import lax
from jax.experimental import pallas as pl
from jax.experimental.pallas import tpu as pltpu

F32 = jnp.float32
BF16 = jnp.bfloat16

GRID_W = 64
WIN_ROWS = 8
WIN_COLS = 16
FOURIER_GROUPS = 4
N_MIXERS = 2
EPS = 1e-6
NEG_INF = float("-inf")

ROWS_PER_STEP = 2
WIN_UNION = WIN_ROWS + ROWS_PER_STEP
DR_PAD = WIN_ROWS + 1
N_DR_TILES = 2 * DR_PAD
FFT_L2 = 64
VMEM_LIMIT = 56 * 1024 * 1024


def _cparams(sem):
    return pltpu.CompilerParams(dimension_semantics=sem, vmem_limit_bytes=VMEM_LIMIT)


def _resident(shape, index_map):
    return pl.BlockSpec(shape, index_map, pipeline_mode=pl.Buffered(1))


def _mods_body(c_ref, w_ref, b_ref, o_ref):
    c = c_ref[...]
    s = (c * jax.nn.sigmoid(c)).astype(BF16)
    o_ref[...] = jnp.dot(s, w_ref[...].astype(BF16), preferred_element_type=F32) + b_ref[...]


def _mods(cc, ada_w, ada_b):
    n_layers, d, d6 = ada_w.shape
    tn = 1536
    return pl.pallas_call(
        _mods_body,
        grid=(n_layers, d6 // tn),
        in_specs=[
            pl.BlockSpec((8, d), lambda l, n: (0, 0)),
            pl.BlockSpec((None, d, tn), lambda l, n: (l, 0, n)),
            pl.BlockSpec((None, 1, tn), lambda l, n: (l, 0, n)),
        ],
        out_specs=pl.BlockSpec((None, 8, tn), lambda l, n: (l, 0, n)),
        out_shape=jax.ShapeDtypeStruct((n_layers, 8, d6), F32),
        compiler_params=_cparams(("parallel", "parallel")),
    )(cc, ada_w, ada_b.reshape(n_layers, 1, d6))


def _norm_mod(x, g, shift, scale):
    ms = jnp.mean(x * x, axis=-1, keepdims=True)
    return (x * lax.rsqrt(ms + EPS) * g) * (1.0 + scale) + shift


def _qkv_body(x_ref, mod_ref, g_ref, w_ref, q_ref, k_ref, v_ref):
    d = x_ref.shape[-1]
    h = _norm_mod(x_ref[...], g_ref[...], mod_ref[0:1, :], mod_ref[1:2, :]).astype(BF16)
    q_ref[...] = jnp.dot(h, w_ref[:, 0:d], preferred_element_type=F32).astype(BF16)
    k_ref[...] = jnp.dot(h, w_ref[:, d:2 * d], preferred_element_type=F32).astype(BF16)
    v_ref[...] = jnp.dot(h, w_ref[:, 2 * d:3 * d], preferred_element_type=F32).astype(BF16)


def _qkv(x, mods4, layer, ctx_row, g, w_bf16, tm):
    b, t, d = x.shape
    mod_idx = (lambda bi, ti: (layer, bi, 0, 0)) if ctx_row is None else (lambda bi, ti: (layer, ctx_row, 0, 0))
    tok = pl.BlockSpec((None, tm, d), lambda bi, ti: (bi, ti, 0))
    out = jax.ShapeDtypeStruct((b, t, d), BF16)
    return pl.pallas_call(
        _qkv_body,
        grid=(b, t // tm),
        in_specs=[
            tok,
            pl.BlockSpec((None, None, 6, d), mod_idx),
            pl.BlockSpec((1, d), lambda bi, ti: (0, 0)),
            _resident((d, 3 * d), lambda bi, ti: (0, 0)),
        ],
        out_specs=[tok, tok, tok],
        out_shape=[out, out, out],
        compiler_params=_cparams(("parallel", "parallel")),
    )(x, mods4, g, w_bf16)


def _bias_body(r_ref, o_ref, *, tn):
    col0 = pl.program_id(0) * tn
    e = col0 + lax.broadcasted_iota(jnp.int32, (64, tn), 1)
    r = lax.broadcasted_iota(jnp.int32, (64, tn), 0)
    cq = e // (2 * GRID_W)
    half = (e // GRID_W) % 2
    ck = e % GRID_W
    sel = ((r // 32 == half) & (r % 32 == ck - cq + WIN_COLS - 1)).astype(BF16)
    rv = r_ref[...]
    hi = rv.astype(BF16)
    rem = rv - hi.astype(F32)
    mid = rem.astype(BF16)
    lo = (rem - mid.astype(F32)).astype(BF16)
    acc = jnp.dot(hi, sel, preferred_element_type=F32)
    acc += jnp.dot(mid, sel, preferred_element_type=F32)
    acc += jnp.dot(lo, sel, preferred_element_type=F32)
    e1 = col0 + lax.broadcasted_iota(jnp.int32, (1, tn), 1)
    cq1 = e1 // (2 * GRID_W)
    ck1 = e1 % GRID_W
    start = jnp.clip(cq1 - WIN_COLS // 2, 0, GRID_W - WIN_COLS)
    ok = (ck1 >= start) & (ck1 < start + WIN_COLS)
    o_ref[...] = acc + jnp.where(ok, 0.0, NEG_INF)


def _bias_table(rpb):
    h, nr, nc = rpb.shape
    pad_lo = DR_PAD - (WIN_ROWS - 1)
    rp = jnp.pad(rpb, ((0, 0), (pad_lo, pad_lo + 1), (0, 32 - nc)))
    r2 = jnp.concatenate([rp[:, 0:N_DR_TILES], rp[:, 1:N_DR_TILES + 1]], axis=-1)
    r2 = r2.reshape(h * N_DR_TILES, 64)
    n_cols = GRID_W * 2 * GRID_W
    tn = 2048
    out = pl.pallas_call(
        functools.partial(_bias_body, tn=tn),
        grid=(n_cols // tn,),
        in_specs=[pl.BlockSpec((h * N_DR_TILES, 64), lambda n: (0, 0))],
        out_specs=pl.BlockSpec((h * N_DR_TILES, tn), lambda n: (0, n)),
        out_shape=jax.ShapeDtypeStruct((h * N_DR_TILES, n_cols), F32),
        compiler_params=_cparams(("parallel",)),
    )(r2)
    return out.reshape(h, N_DR_TILES, GRID_W, 2 * GRID_W)


def _head_norm(t, gain, ones_blk):
    t2 = t * t
    hi = t2.astype(BF16)
    lo = (t2 - hi.astype(F32)).astype(BF16)
    ssq = jnp.dot(hi, ones_blk, preferred_element_type=F32) + jnp.dot(lo, ones_blk, preferred_element_type=F32)
    hd = t.shape[-1] // 2
    return t * lax.rsqrt(ssq * (1.0 / hd) + EPS) * gain


def _softmax_pv(s_parts, v_parts):
    m = s_parts[0].max(axis=-1, keepdims=True)
    for s in s_parts[1:]:
        m = jnp.maximum(m, s.max(axis=-1, keepdims=True))
    l = None
    acc = None
    for s, v in zip(s_parts, v_parts):
        p = jnp.exp(s - m)
        ps = p.sum(axis=-1, keepdims=True)
        pv = jnp.dot(p.astype(BF16), v, preferred_element_type=F32)
        l = ps if l is None else l + ps
        acc = pv if acc is None else acc + pv
    return acc / l


def _attn_body(q_ref, k_ref, v_ref, qc_ref, kc_ref, vc_ref, gq_ref, gk_ref, bias_ref, o_ref, oc_ref,
               qa_s, qb_s, kn_s, *, n_rows):
    s_len = q_ref.shape[0]
    c_len = qc_ref.shape[0]
    hd = q_ref.shape[1] // 2
    qrows = ROWS_PER_STEP * GRID_W
    kwin = WIN_UNION * GRID_W
    scale = hd ** -0.5

    lane = lax.broadcasted_iota(jnp.int32, (1, 2 * hd), 1)
    first = lane < hd
    ri = lax.broadcasted_iota(jnp.int32, (2 * hd, 2 * hd), 0)
    ci = lax.broadcasted_iota(jnp.int32, (2 * hd, 2 * hd), 1)
    ones_blk = (ri // hd == ci // hd).astype(BF16)
    gq = gq_ref[...] * scale
    gk = gk_ref[...]

    chunk = 512

    def norm_chunk(src_q, src_k, src_off, dst_off, n):
        qn = _head_norm(src_q[pl.ds(src_off, n), :].astype(F32), gq, ones_blk)
        qa_s[pl.ds(dst_off, n), :] = jnp.where(first, qn, 0.0).astype(BF16)
        qb_s[pl.ds(dst_off, n), :] = jnp.where(first, 0.0, qn).astype(BF16)
        kn = _head_norm(src_k[pl.ds(src_off, n), :].astype(F32), gk, ones_blk)
        kn_s[pl.ds(dst_off, n), :] = kn.astype(BF16)

    def norm_loop(i, carry):
        off = pl.multiple_of(i * chunk, chunk)
        norm_chunk(q_ref, k_ref, off, off, chunk)
        return carry

    lax.fori_loop(0, s_len // chunk, norm_loop, 0)
    norm_chunk(qc_ref, kc_ref, 0, s_len, c_len)

    kc = kn_s[pl.ds(s_len, c_len), :]
    vc = vc_ref[...]
    key_row_of_lane = lax.broadcasted_iota(jnp.int32, (1, kwin), 1) // GRID_W

    def pick_heads(acc):
        n = acc.shape[0] // 2
        return jnp.where(first, acc[:n], acc[n:])

    def step(j, carry):
        r0 = j * ROWS_PER_STEP
        u = jnp.clip(r0 - WIN_ROWS // 2, 0, n_rows - WIN_UNION)
        qoff = pl.multiple_of(j * qrows, qrows)
        koff = pl.multiple_of(u * GRID_W, 2 * GRID_W)
        qq = jnp.concatenate([qa_s[pl.ds(qoff, qrows), :], qb_s[pl.ds(qoff, qrows), :]], axis=0)
        kw = kn_s[pl.ds(koff, kwin), :]
        vw = v_ref[pl.ds(koff, kwin), :]
        nt = (((1,), (1,)), ((), ()))
        s_lat = lax.dot_general(qq, kw, nt, preferred_element_type=F32)
        s_ctx = lax.dot_general(qq, kc, nt, preferred_element_type=F32)

        blocks = []
        for hh in range(2):
            for a in range(ROWS_PER_STEP):
                r = r0 + a
                rs = jnp.clip(r - WIN_ROWS // 2, 0, n_rows - WIN_ROWS)
                kr = u + key_row_of_lane
                row_ok = (kr >= rs) & (kr < rs + WIN_ROWS)
                t0 = u - r + DR_PAD
                tiles = [bias_ref[hh, t0 + 2 * ii] for ii in range(WIN_UNION // 2)]
                bias = jnp.concatenate(tiles, axis=1)
                blk = s_lat[(hh * ROWS_PER_STEP + a) * GRID_W:(hh * ROWS_PER_STEP + a + 1) * GRID_W]
                blocks.append(jnp.where(row_ok, blk + bias, NEG_INF))
        s_lat = jnp.concatenate(blocks, axis=0)

        acc = _softmax_pv([s_lat, s_ctx], [vw, vc])
        o_ref[pl.ds(qoff, qrows), :] = pick_heads(acc).astype(o_ref.dtype)
        return carry

    lax.fori_loop(0, n_rows // ROWS_PER_STEP, step, 0)

    qqc = jnp.concatenate([qa_s[pl.ds(s_len, c_len), :], qb_s[pl.ds(s_len, c_len), :]], axis=0)
    s_cc = lax.dot_general(qqc, kc, (((1,), (1,)), ((), ())), preferred_element_type=F32)
    oc_ref[...] = pick_heads(_softmax_pv([s_cc], [vc])).astype(oc_ref.dtype)


def _attention(q, k, v, qc, kc, vc, gq2, gk2, bias_tab):
    b, s_len, d = q.shape
    c_len = qc.shape[1]
    pw = gq2.shape[-1]
    n_rows = s_len // GRID_W
    assert n_rows >= WIN_UNION and n_rows % ROWS_PER_STEP == 0 and s_len % 512 == 0
    lat = pl.BlockSpec((None, s_len, pw), lambda bi, pi: (bi, 0, pi))
    cx = pl.BlockSpec((None, c_len, pw), lambda bi, pi: (bi, 0, pi))
    gain = pl.BlockSpec((1, pw), lambda bi, pi: (0, 0))
    return pl.pallas_call(
        functools.partial(_attn_body, n_rows=n_rows),
        grid=(b, d // pw),
        in_specs=[lat, lat, lat, cx, cx, cx, gain, gain,
                  pl.BlockSpec((2, N_DR_TILES, GRID_W, 2 * GRID_W), lambda bi, pi: (pi, 0, 0, 0))],
        out_specs=[lat, cx],
        out_shape=[jax.ShapeDtypeStruct((b, s_len, d), BF16), jax.ShapeDtypeStruct((b, c_len, d), BF16)],
        scratch_shapes=[pltpu.VMEM((s_len + c_len, pw), BF16)] * 3,
        compiler_params=_cparams(("parallel", "parallel")),
    )(q, k, v, qc, kc, vc, gq2, gk2, bias_tab)


def _mlp_body(*refs, has_proj, ff_chunk):
    if has_proj:
        x_ref, o_ref, mod_ref, g_ref, wo_ref, w1_ref, w2_ref, out_ref = refs
        y = jnp.dot(o_ref[...], wo_ref[...], preferred_element_type=F32)
        x = x_ref[...] + mod_ref[2:3, :] * y
    else:
        x_ref, mod_ref, g_ref, w1_ref, w2_ref, out_ref = refs
        x = x_ref[...]
    h = _norm_mod(x, g_ref[...], mod_ref[3:4, :], mod_ref[4:5, :]).astype(BF16)
    d_ff = w1_ref.shape[1]
    acc = None
    for c in range(0, d_ff, ff_chunk):
        a = jnp.maximum(jnp.dot(h, w1_ref[:, c:c + ff_chunk], preferred_element_type=F32), 0.0)
        part = jnp.dot((a * a).astype(BF16), w2_ref[c:c + ff_chunk, :], preferred_element_type=F32)
        acc = part if acc is None else acc + part
    out_ref[...] = x + mod_ref[5:6, :] * acc


def _mlp(x, o, mods4, layer, ctx_row, g, wo, w1, w2, tm):
    b, t, d = x.shape
    d_ff = w1.shape[1]
    mod_idx = (lambda bi, ti: (layer, bi, 0, 0)) if ctx_row is None else (lambda bi, ti: (layer, ctx_row, 0, 0))
    tok = pl.BlockSpec((None, tm, d), lambda bi, ti: (bi, ti, 0))
    const = lambda bi, ti: (0, 0)
    has_proj = o is not None
    in_specs = [tok] + ([tok] if has_proj else []) + [
        pl.BlockSpec((None, None, 6, d), mod_idx), pl.BlockSpec((1, d), const)]
    in_specs += ([_resident((d, d), const)] if has_proj else []) + [
        _resident((d, d_ff), const), _resident((d_ff, d), const)]
    args = [x] + ([o] if has_proj else []) + [mods4, g] + ([wo] if has_proj else []) + [w1, w2]
    return pl.pallas_call(
        functools.partial(_mlp_body, has_proj=has_proj, ff_chunk=1024),
        grid=(b, t // tm),
        in_specs=in_specs,
        out_specs=tok,
        out_shape=jax.ShapeDtypeStruct((b, t, d), F32),
        compiler_params=_cparams(("parallel", "parallel")),
    )(*args)


def _dft_tables(n, dg):
    l2 = FFT_L2
    l1 = n // l2
    ar = np.arange
    a = 2.0 * np.pi * np.outer(ar(dg), ar(dg)) / dg
    norm = 1.0 / math.sqrt(n * dg)
    cd, sd = np.cos(a) * norm, np.sin(a) * norm
    al = 2.0 * np.pi * np.outer(ar(l1), ar(l1)) / l1
    c1, s1 = np.cos(al), np.sin(al)
    m1 = np.block([[c1, -s1], [-s1, -c1]])
    k = ar(l1)[:, None, None] + l1 * ar(l2)[None, :, None]
    be = 2.0 * np.pi * (k * ar(l2)[None, None, :] % n) / n
    m2 = np.concatenate([np.cos(be), np.sin(be)], axis=-1)
    return (jnp.asarray(cd, F32), jnp.asarray(sd, F32), jnp.asarray(m1, F32), jnp.asarray(m2, F32))


def _fold_body(cd_ref, sd_ref, w_ref, o_ref):
    w = w_ref[...]
    hp = lax.Precision.HIGHEST
    o_ref[0] = jnp.dot(cd_ref[...], w, precision=hp, preferred_element_type=F32).astype(BF16)
    o_ref[1] = jnp.dot(sd_ref[...], w, precision=hp, preferred_element_type=F32).astype(BF16)


def _fold_fourier_weight(cd, sd, w):
    d = w.shape[0]
    dg = cd.shape[0]
    return pl.pallas_call(
        _fold_body,
        grid=(d // dg,),
        in_specs=[pl.BlockSpec((dg, dg), lambda g: (0, 0)), pl.BlockSpec((dg, dg), lambda g: (0, 0)),
                  pl.BlockSpec((dg, d), lambda g: (g, 0))],
        out_specs=pl.BlockSpec((2, dg, d), lambda g: (0, g, 0)),
        out_shape=jax.ShapeDtypeStruct((2, d, d), BF16),
        compiler_params=_cparams(("parallel",)),
    )(cd, sd, w)


def _fproj_body(x_ref, mod_ref, g_ref, w_ref, p_ref):
    h = _norm_mod(x_ref[...], g_ref[...], mod_ref[0:1, :], mod_ref[1:2, :]).astype(BF16)
    p_ref[0] = jnp.dot(h, w_ref[0], preferred_element_type=F32).astype(BF16)
    p_ref[1] = jnp.dot(h, w_ref[1], preferred_element_type=F32).astype(BF16)


def _fourier_proj(x, mods4, layer, g, wf, tm):
    b, t, d = x.shape
    return pl.pallas_call(
        _fproj_body,
        grid=(b, t // tm),
        in_specs=[
            pl.BlockSpec((None, tm, d), lambda bi, ti: (bi, ti, 0)),
            pl.BlockSpec((None, None, 6, d), lambda bi, ti: (layer, bi, 0, 0)),
            pl.BlockSpec((1, d), lambda bi, ti: (0, 0)),
            _resident((2, d, d), lambda bi, ti: (0, 0, 0)),
        ],
        out_specs=pl.BlockSpec((None, 2, tm, d), lambda bi, ti: (bi, 0, ti, 0)),
        out_shape=jax.ShapeDtypeStruct((b, 2, t, d), BF16),
        compiler_params=_cparams(("parallel", "parallel")),
    )(x, mods4, g, wf)


def _dft1_body(m_ref, p_ref, z_ref):
    z_ref[...] = jnp.dot(m_ref[...].astype(BF16), p_ref[...], preferred_element_type=F32).astype(BF16)


def _dft_stage1(p2d, m1):
    b, r, n = p2d.shape
    tn = 8192
    blk = pl.BlockSpec((None, r, tn), lambda bi, ni: (bi, 0, ni))
    return pl.pallas_call(
        _dft1_body,
        grid=(b, n // tn),
        in_specs=[pl.BlockSpec((r, r), lambda bi, ni: (0, 0)), blk],
        out_specs=blk,
        out_shape=jax.ShapeDtypeStruct((b, r, n), BF16),
        compiler_params=_cparams(("parallel", "parallel")),
    )(m1, p2d)


def _dft2_body(m_ref, z_ref, x_ref, mod_ref, o_ref, *, kb, d):
    gate = mod_ref[2:3, :]
    for kk in range(kb):
        zz = jnp.concatenate([z_ref[0, kk], z_ref[1, kk]], axis=0)
        y = jnp.dot(m_ref[kk].astype(BF16), zz, preferred_element_type=F32)
        o_ref[:, kk * d:(kk + 1) * d] = x_ref[:, kk * d:(kk + 1) * d] + gate * y


def _dft_stage2_residual(z5, m2, x3, mods4, layer, kb):
    b, _, l1, l2, d = z5.shape
    xblk = pl.BlockSpec((None, l2, kb * d), lambda bi, ki: (bi, 0, ki))
    return pl.pallas_call(
        functools.partial(_dft2_body, kb=kb, d=d),
        grid=(b, l1 // kb),
        in_specs=[
            pl.BlockSpec((kb, l2, 2 * l2), lambda bi, ki: (ki, 0, 0)),
            pl.BlockSpec((None, 2, kb, l2, d), lambda bi, ki: (bi, 0, ki, 0, 0)),
            xblk,
            pl.BlockSpec((None, None, 6, d), lambda bi, ki: (layer, bi, 0, 0)),
        ],
        out_specs=xblk,
        out_shape=jax.ShapeDtypeStruct(x3.shape, F32),
        compiler_params=_cparams(("parallel", "parallel")),
    )(m2, z5, x3, mods4)


def kernel(x, c, ctx, c_ctx, ada_w, ada_b, norm_g, attn_wqkv, attn_wo, q_norm_g, k_norm_g, rpb,
           fourier_w, mlp_w1, mlp_w2):
    b, s_len, d = x.shape
    depth = ada_w.shape[0]
    hd = q_norm_g.shape[-1]
    assert depth == 2 and b + 1 <= 8 and s_len % (FFT_L2 * 16) == 0 and d % (2 * hd) == 0
    ctx_row = b

    cc = jnp.zeros((8, d), F32).at[:b].set(c).at[b].set(c_ctx)
    mods4 = _mods(cc, ada_w, ada_b).reshape(depth, 8, 6, d)

    tm = 512
    for layer in range(depth):
        is_attn = layer % N_MIXERS == 0
        idx = layer // N_MIXERS
        need_ctx = layer < depth - 1
        g_pre = norm_g[layer, 0][None]
        g_post = norm_g[layer, 1][None]
        w1 = mlp_w1[layer].astype(BF16)
        w2 = mlp_w2[layer].astype(BF16)
        if is_attn:
            wqkv = attn_wqkv[idx].astype(BF16)
            wo = attn_wo[idx].astype(BF16)
            gq2 = jnp.tile(q_norm_g[idx], 2)[None]
            gk2 = jnp.tile(k_norm_g[idx], 2)[None]
            q, k, v = _qkv(x, mods4, layer, None, g_pre, wqkv, tm)
            qc, kc, vc = _qkv(ctx, mods4, layer, ctx_row, g_pre, wqkv, ctx.shape[1])
            o, oc = _attention(q, k, v, qc, kc, vc, gq2, gk2, _bias_table(rpb[idx]))
            x = _mlp(x, o, mods4, layer, None, g_post, wo, w1, w2, tm)
            if need_ctx:
                ctx = _mlp(ctx, oc, mods4, layer, ctx_row, g_post, wo, w1, w2, ctx.shape[1])
        else:
            assert not need_ctx, "a Fourier layer that still feeds a context stream is not supported"
            l1 = s_len // FFT_L2
            cd, sd, m1, m2 = _dft_tables(s_len, d // FOURIER_GROUPS)
            wf = _fold_fourier_weight(cd, sd, fourier_w[idx])
            p = _fourier_proj(x, mods4, layer, g_pre, wf, tm)
            z = _dft_stage1(p.reshape(b, 2 * l1, FFT_L2 * d), m1)
            x3 = _dft_stage2_residual(z.reshape(b, 2, l1, FFT_L2, d), m2,
                                      x.reshape(b, FFT_L2, l1 * d), mods4, layer, 8)
            x = _mlp(x3.reshape(b, s_len, d), None, mods4, layer, None, g_post, None, w1, w2, tm)
    return x
```

```python
import functools
import math

import numpy as np
import jax
import jax.numpy as jnp
from jax import lax
from jax.experimental import pallas as pl
from jax.experimental.pallas import tpu as pltpu

F32 = jnp.float32
BF16 = jnp.bfloat16

GRID_W = 64
WIN_ROWS = 8
WIN_COLS = 16
FOURIER_GROUPS = 4
N_MIXERS = 2
EPS = 1e-6
NEG_INF = float("-inf")
LOG2E = math.log2(math.e)

N_DR_TILES = 2 * WIN_ROWS - 2
FFT_L2 = 64
SUBLANES = 8
LANES = 128
TOKEN_TILE = 512
FF_CHUNK = 1024
VMEM_LIMIT = 56 * 1024 * 1024


def _cparams(sem):
    return pltpu.CompilerParams(dimension_semantics=sem, vmem_limit_bytes=VMEM_LIMIT)


def _resident(shape, index_map):
    return pl.BlockSpec(shape, index_map, pipeline_mode=pl.Buffered(1))


def _mods_body(c_ref, w_ref, b_ref, o_ref):
    c = c_ref[...]
    s = (c * jax.nn.sigmoid(c)).astype(BF16)
    o_ref[...] = jnp.dot(s, w_ref[...].astype(BF16), preferred_element_type=F32) + b_ref[...]


def _mods(cc, ada_w, ada_b):
    n_layers, d, d6 = ada_w.shape
    tn = 1536
    return pl.pallas_call(
        _mods_body,
        grid=(n_layers, d6 // tn),
        in_specs=[
            pl.BlockSpec((8, d), lambda l, n: (0, 0)),
            pl.BlockSpec((None, d, tn), lambda l, n: (l, 0, n)),
            pl.BlockSpec((None, 1, tn), lambda l, n: (l, 0, n)),
        ],
        out_specs=pl.BlockSpec((None, 8, tn), lambda l, n: (l, 0, n)),
        out_shape=jax.ShapeDtypeStruct((n_layers, 8, d6), F32),
        compiler_params=_cparams(("parallel", "parallel")),
    )(cc, ada_w, ada_b.reshape(n_layers, 1, d6))


def _norm_mod(x, g, shift, scale):
    ms = jnp.mean(x * x, axis=-1, keepdims=True)
    return (x * lax.rsqrt(ms + EPS) * g) * (1.0 + scale) + shift


def _sq_relu_mlp(h, w1_ref, w2_ref):
    acc = None
    for c in range(0, w1_ref.shape[1], FF_CHUNK):
        a = jnp.maximum(jnp.dot(h, w1_ref[:, c:c + FF_CHUNK], preferred_element_type=F32), 0.0)
        part = jnp.dot((a * a).astype(BF16), w2_ref[c:c + FF_CHUNK, :], preferred_element_type=F32)
        acc = part if acc is None else acc + part
    return acc


def _qkv_body(x_ref, mod_ref, g_ref, w_ref, q_ref, k_ref, v_ref):
    d = x_ref.shape[-1]
    h = _norm_mod(x_ref[...], g_ref[...], mod_ref[0:1, :], mod_ref[1:2, :]).astype(BF16)
    q_ref[...] = jnp.dot(h, w_ref[:, 0:d], preferred_element_type=F32).astype(BF16)
    k_ref[...] = jnp.dot(h, w_ref[:, d:2 * d], preferred_element_type=F32).astype(BF16)
    v_ref[...] = jnp.dot(h, w_ref[:, 2 * d:3 * d], preferred_element_type=F32).astype(BF16)


def _qkv(x, mods4, layer, ctx_row, g, w_bf16, tm):
    b, t, d = x.shape
    mod_idx = (lambda bi, ti: (layer, bi, 0, 0)) if ctx_row is None else (lambda bi, ti: (layer, ctx_row, 0, 0))
    tok = pl.BlockSpec((None, tm, d), lambda bi, ti: (bi, ti, 0))
    out = jax.ShapeDtypeStruct((b, t, d), BF16)
    return pl.pallas_call(
        _qkv_body,
        grid=(b, t // tm),
        in_specs=[
            tok,
            pl.BlockSpec((None, None, 6, d), mod_idx),
            pl.BlockSpec((1, d), lambda bi, ti: (0, 0)),
            _resident((d, 3 * d), lambda bi, ti: (0, 0)),
        ],
        out_specs=[tok, tok, tok],
        out_shape=[out, out, out],
        compiler_params=_cparams(("parallel", "parallel")),
    )(x, mods4, g, w_bf16)


def _bias_body(r_ref, o_ref, *, tn):
    col0 = pl.program_id(0) * tn
    e = col0 + lax.broadcasted_iota(jnp.int32, (64, tn), 1)
    r = lax.broadcasted_iota(jnp.int32, (64, tn), 0)
    cq = e // (2 * GRID_W)
    half = (e // GRID_W) % 2
    ck = e % GRID_W
    sel = ((r // 32 == half) & (r % 32 == ck - cq + WIN_COLS - 1)).astype(BF16)
    rv = r_ref[...]
    hi = rv.astype(BF16)
    rem = rv - hi.astype(F32)
    mid = rem.astype(BF16)
    lo = (rem - mid.astype(F32)).astype(BF16)
    acc = jnp.dot(hi, sel, preferred_element_type=F32)
    acc += jnp.dot(mid, sel, preferred_element_type=F32)
    acc += jnp.dot(lo, sel, preferred_element_type=F32)
    e1 = col0 + lax.broadcasted_iota(jnp.int32, (1, tn), 1)
    cq1 = e1 // (2 * GRID_W)
    ck1 = e1 % GRID_W
    start = jnp.clip(cq1 - WIN_COLS // 2, 0, GRID_W - WIN_COLS)
    ok = (ck1 >= start) & (ck1 < start + WIN_COLS)
    o_ref[...] = acc * LOG2E + jnp.where(ok, 0.0, NEG_INF)


def _bias_table(rpb):
    h, nr, nc = rpb.shape
    rp = jnp.pad(rpb, ((0, 0), (0, 0), (0, 32 - nc)))
    r2 = jnp.concatenate([rp[:, 0:N_DR_TILES], rp[:, 1:N_DR_TILES + 1]], axis=-1)
    r2 = r2.reshape(h * N_DR_TILES, 64)
    n_cols = GRID_W * 2 * GRID_W
    tn = 2048
    out = pl.pallas_call(
        functools.partial(_bias_body, tn=tn),
        grid=(n_cols // tn,),
        in_specs=[pl.BlockSpec((h * N_DR_TILES, 64), lambda n: (0, 0))],
        out_specs=pl.BlockSpec((h * N_DR_TILES, tn), lambda n: (0, n)),
        out_shape=jax.ShapeDtypeStruct((h * N_DR_TILES, n_cols), F32),
        compiler_params=_cparams(("parallel",)),
    )(r2)
    return out.reshape(h, N_DR_TILES, GRID_W, 2 * GRID_W)


def _head_norm(t, gain, ones_blk):
    ssq = jnp.dot((t * t).astype(BF16), ones_blk, preferred_element_type=F32)
    hd = t.shape[-1] // 2
    return t * lax.rsqrt(ssq * (1.0 / hd) + EPS) * gain


def _softmax2_pv(s_parts, v_parts):
    m = s_parts[0].max(axis=-1, keepdims=True)
    for s in s_parts[1:]:
        m = jnp.maximum(m, s.max(axis=-1, keepdims=True))
    l = None
    acc = None
    for s, v in zip(s_parts, v_parts):
        p = jnp.exp2(s - m)
        ps = p.sum(axis=-1, keepdims=True)
        pv = jnp.dot(p.astype(BF16), v, preferred_element_type=F32)
        l = ps if l is None else l + ps
        acc = pv if acc is None else acc + pv
    return acc / l


def _attn_body(q_ref, k_ref, v_ref, qc_ref, kc_ref, vc_ref, gq_ref, gk_ref, bias_ref, o_ref, oc_ref,
               qn_s, kn_s, *, n_rows, unroll):
    s_len = q_ref.shape[0]
    c_len = qc_ref.shape[0]
    hd = q_ref.shape[1] // 2
    kwin = WIN_ROWS * GRID_W
    nt = (((1,), (1,)), ((), ()))

    lane = lax.broadcasted_iota(jnp.int32, (1, 2 * hd), 1)
    first = lane < hd
    ri = lax.broadcasted_iota(jnp.int32, (2 * hd, 2 * hd), 0)
    ci = lax.broadcasted_iota(jnp.int32, (2 * hd, 2 * hd), 1)
    ones_blk = (ri // hd == ci // hd).astype(BF16)
    gq = gq_ref[...] * (hd ** -0.5 * LOG2E)
    gk = gk_ref[...]

    chunk = 512

    def norm_chunk(src_q, src_k, src_off, dst_off, n):
        qn_s[pl.ds(dst_off, n), :] = _head_norm(src_q[pl.ds(src_off, n), :].astype(F32), gq, ones_blk).astype(BF16)
        kn_s[pl.ds(dst_off, n), :] = _head_norm(src_k[pl.ds(src_off, n), :].astype(F32), gk, ones_blk).astype(BF16)

    def norm_loop(i, carry):
        off = pl.multiple_of(i * chunk, chunk)
        norm_chunk(q_ref, k_ref, off, off, chunk)
        return carry

    lax.fori_loop(0, s_len // chunk, norm_loop, 0)
    norm_chunk(qc_ref, kc_ref, 0, s_len, c_len)

    kc = kn_s[pl.ds(s_len, c_len), :]
    vc = vc_ref[...]

    def stack_heads(q):
        zero = jnp.zeros_like(q)
        return jnp.concatenate([jnp.where(first, q, zero), jnp.where(first, zero, q)], axis=0)

    def pick_heads(acc):
        n = acc.shape[0] // 2
        return jnp.where(first, acc[:n], acc[n:])

    def one_row(r):
        rs = jnp.clip(r - WIN_ROWS // 2, 0, n_rows - WIN_ROWS)
        qoff = pl.multiple_of(r * GRID_W, GRID_W)
        koff = pl.multiple_of(rs * GRID_W, GRID_W)
        qq = stack_heads(qn_s[pl.ds(qoff, GRID_W), :])
        kw = kn_s[pl.ds(koff, kwin), :]
        vw = v_ref[pl.ds(koff, kwin), :]
        s_lat = lax.dot_general(qq, kw, nt, preferred_element_type=F32)
        s_ctx = lax.dot_general(qq, kc, nt, preferred_element_type=F32)
        t0 = rs - r + WIN_ROWS - 1
        bias = jnp.concatenate(
            [jnp.concatenate([bias_ref[hh, t0 + 2 * ii] for ii in range(WIN_ROWS // 2)], axis=1)
             for hh in range(2)], axis=0)
        acc = _softmax2_pv([s_lat + bias, s_ctx], [vw, vc])
        o_ref[pl.ds(qoff, GRID_W), :] = pick_heads(acc).astype(o_ref.dtype)

    def step(j, carry):
        for a in range(unroll):
            one_row(j * unroll + a)
        return carry

    lax.fori_loop(0, n_rows // unroll, step, 0)

    qqc = stack_heads(qn_s[pl.ds(s_len, c_len), :])
    s_cc = lax.dot_general(qqc, kc, nt, preferred_element_type=F32)
    oc_ref[...] = pick_heads(_softmax2_pv([s_cc], [vc])).astype(oc_ref.dtype)


def _attention(q, k, v, qc, kc, vc, gq2, gk2, bias_tab):
    b, s_len, d = q.shape
    c_len = qc.shape[1]
    pw = gq2.shape[-1]
    n_rows = s_len // GRID_W
    unroll = 2
    assert n_rows >= WIN_ROWS and n_rows % unroll == 0 and s_len % 512 == 0
    lat = pl.BlockSpec((None, s_len, pw), lambda bi, pi: (bi, 0, pi))
    cx = pl.BlockSpec((None, c_len, pw), lambda bi, pi: (bi, 0, pi))
    gain = pl.BlockSpec((1, pw), lambda bi, pi: (0, 0))
    return pl.pallas_call(
        functools.partial(_attn_body, n_rows=n_rows, unroll=unroll),
        grid=(b, d // pw),
        in_specs=[lat, lat, lat, cx, cx, cx, gain, gain,
                  pl.BlockSpec((2, N_DR_TILES, GRID_W, 2 * GRID_W), lambda bi, pi: (pi, 0, 0, 0))],
        out_specs=[lat, cx],
        out_shape=[jax.ShapeDtypeStruct((b, s_len, d), BF16), jax.ShapeDtypeStruct((b, c_len, d), BF16)],
        scratch_shapes=[pltpu.VMEM((s_len + c_len, pw), BF16)] * 2,
        compiler_params=_cparams(("parallel", "parallel")),
    )(q, k, v, qc, kc, vc, gq2, gk2, bias_tab)


def _proj_mlp_body(x_ref, o_ref, mod_ref, g_ref, wo_ref, w1_ref, w2_ref, out_ref):
    y = jnp.dot(o_ref[...], wo_ref[...], preferred_element_type=F32)
    x = x_ref[...] + mod_ref[2:3, :] * y
    h = _norm_mod(x, g_ref[...], mod_ref[3:4, :], mod_ref[4:5, :]).astype(BF16)
    out_ref[...] = x + mod_ref[5:6, :] * _sq_relu_mlp(h, w1_ref, w2_ref)


def _proj_mlp(x, o, mods4, layer, ctx_row, g, wo, w1, w2, tm):
    b, t, d = x.shape
    d_ff = w1.shape[1]
    mod_idx = (lambda bi, ti: (layer, bi, 0, 0)) if ctx_row is None else (lambda bi, ti: (layer, ctx_row, 0, 0))
    tok = pl.BlockSpec((None, tm, d), lambda bi, ti: (bi, ti, 0))
    const = lambda bi, ti: (0, 0)
    return pl.pallas_call(
        _proj_mlp_body,
        grid=(b, t // tm),
        in_specs=[tok, tok, pl.BlockSpec((None, None, 6, d), mod_idx), pl.BlockSpec((1, d), const),
                  _resident((d, d), const), _resident((d, d_ff), const), _resident((d_ff, d), const)],
        out_specs=tok,
        out_shape=jax.ShapeDtypeStruct((b, t, d), F32),
        compiler_params=_cparams(("parallel", "parallel")),
    )(x, o, mods4, g, wo, w1, w2)


def _dft_tables(n, dg):
    l2 = FFT_L2
    l1 = n // l2
    ar = np.arange
    a = 2.0 * np.pi * np.outer(ar(dg), ar(dg)) / dg
    norm = 1.0 / math.sqrt(n * dg)
    cd, sd = np.cos(a) * norm, np.sin(a) * norm
    al = 2.0 * np.pi * np.outer(ar(l1), ar(l1)) / l1
    c1, s1 = np.cos(al), np.sin(al)
    m1 = np.block([[c1, -s1], [-s1, -c1]])
    k = ar(l1)[:, None, None] + l1 * ar(l2)[None, :, None]
    be = 2.0 * np.pi * (k * ar(l2)[None, None, :] % n) / n
    m2 = np.concatenate([np.cos(be), np.sin(be)], axis=-1)
    return (jnp.asarray(cd, F32), jnp.asarray(sd, F32), jnp.asarray(m1, F32), jnp.asarray(m2, F32))


def _put_cols(s_ref, first, val, rows=slice(None)):
    for c in range(val.shape[1] // LANES):
        s_ref[first + c, rows, :] = val[:, c * LANES:(c + 1) * LANES]


def _get_cols(s_ref, first, n_chunks, rows=slice(None)):
    return jnp.concatenate([s_ref[first + c, rows, :] for c in range(n_chunks)], axis=1)


def _fold_body(cd_ref, sd_ref, w_ref, o_ref):
    w = w_ref[...]
    hp = lax.Precision.HIGHEST
    o_ref[0] = jnp.dot(cd_ref[...], w, precision=hp, preferred_element_type=F32).astype(BF16)
    o_ref[1] = jnp.dot(sd_ref[...], w, precision=hp, preferred_element_type=F32).astype(BF16)


def _fold_fourier_weight(cd, sd, w):
    d = w.shape[0]
    dg = cd.shape[0]
    return pl.pallas_call(
        _fold_body,
        grid=(d // dg,),
        in_specs=[pl.BlockSpec((dg, dg), lambda g: (0, 0)), pl.BlockSpec((dg, dg), lambda g: (0, 0)),
                  pl.BlockSpec((dg, d), lambda g: (g, 0))],
        out_specs=pl.BlockSpec((2, dg, d), lambda g: (0, g, 0)),
        out_shape=jax.ShapeDtypeStruct((2, d, d), BF16),
        compiler_params=_cparams(("parallel",)),
    )(cd, sd, w)


def _dft1_body(x_ref, mod_ref, g_ref, w_ref, m_ref, z_ref, p_s):
    l1, nb, d = x_ref.shape
    x = x_ref[...].reshape(l1 * nb, d)
    h = _norm_mod(x, g_ref[...], mod_ref[0:1, :], mod_ref[1:2, :]).astype(BF16)
    nc = d // LANES
    _put_cols(p_s, 0, jnp.dot(h, w_ref[0], preferred_element_type=F32))
    _put_cols(p_s, nc, jnp.dot(h, w_ref[1], preferred_element_type=F32))
    m1 = m_ref[...].astype(BF16)
    for m in range(nb):
        rows = pl.ds(m, l1, stride=nb)
        pm = jnp.concatenate([_get_cols(p_s, 0, nc, rows), _get_cols(p_s, nc, nc, rows)], axis=0).astype(BF16)
        z_ref[m] = jnp.dot(m1, pm, preferred_element_type=F32)


def _fourier_stage1(x, mods4, layer, g, wf, m1):
    b, s_len, d = x.shape
    l1 = s_len // FFT_L2
    nb = SUBLANES
    return pl.pallas_call(
        _dft1_body,
        grid=(b, FFT_L2 // nb),
        in_specs=[
            pl.BlockSpec((None, l1, nb, d), lambda bi, ti: (bi, 0, ti, 0)),
            pl.BlockSpec((None, None, 6, d), lambda bi, ti: (layer, bi, 0, 0)),
            pl.BlockSpec((1, d), lambda bi, ti: (0, 0)),
            _resident((2, d, d), lambda bi, ti: (0, 0, 0)),
            pl.BlockSpec((2 * l1, 2 * l1), lambda bi, ti: (0, 0)),
        ],
        out_specs=pl.BlockSpec((None, nb, 2 * l1, d), lambda bi, ti: (bi, ti, 0, 0)),
        out_shape=jax.ShapeDtypeStruct((b, FFT_L2, 2 * l1, d), F32),
        scratch_shapes=[pltpu.VMEM((2 * d // LANES, l1 * nb, LANES), F32)],
        compiler_params=_cparams(("parallel", "parallel")),
    )(x.reshape(b, l1, FFT_L2, d), mods4, g, wf, m1)


def _dft2_mlp_body(zr_ref, zi_ref, x_ref, m_ref, mod_ref, g_ref, w1_ref, w2_ref, out_ref, z_s, y_s):
    l2, kb, d = x_ref.shape
    n = l2 * kb
    nc = d // LANES
    _put_cols(z_s, 0, zr_ref[...].reshape(n, d))
    _put_cols(z_s, nc, zi_ref[...].reshape(n, d))
    for kk in range(kb):
        rows = pl.ds(kk, l2, stride=kb)
        zz = jnp.concatenate([_get_cols(z_s, 0, nc, rows), _get_cols(z_s, nc, nc, rows)],
                             axis=0).astype(BF16)
        _put_cols(y_s, 0, jnp.dot(m_ref[kk].astype(BF16), zz, preferred_element_type=F32), rows)
    x = x_ref[...].reshape(n, d) + mod_ref[2:3, :] * _get_cols(y_s, 0, nc)
    h = _norm_mod(x, g_ref[...], mod_ref[3:4, :], mod_ref[4:5, :]).astype(BF16)
    out = x + mod_ref[5:6, :] * _sq_relu_mlp(h, w1_ref, w2_ref)
    out_ref[...] = out.reshape(l2, kb, d)


def _fourier_stage2_mlp(z, x, mods4, layer, g, m2, w1, w2):
    b, s_len, d = x.shape
    l1 = s_len // FFT_L2
    kb = SUBLANES
    d_ff = w1.shape[1]
    zblk = lambda part: pl.BlockSpec((None, FFT_L2, None, kb, d), lambda bi, ki: (bi, 0, part, ki, 0))
    xblk = pl.BlockSpec((None, FFT_L2, kb, d), lambda bi, ki: (bi, 0, ki, 0))
    const = lambda bi, ki: (0, 0)
    z5 = z.reshape(b, FFT_L2, 2, l1, d)
    out = pl.pallas_call(
        _dft2_mlp_body,
        grid=(b, l1 // kb),
        in_specs=[
            zblk(0), zblk(1), xblk,
            pl.BlockSpec((kb, FFT_L2, 2 * FFT_L2), lambda bi, ki: (ki, 0, 0)),
            pl.BlockSpec((None, None, 6, d), lambda bi, ki: (layer, bi, 0, 0)),
            pl.BlockSpec((1, d), const),
            _resident((d, d_ff), const), _resident((d_ff, d), const),
        ],
        out_specs=xblk,
        out_shape=jax.ShapeDtypeStruct((b, FFT_L2, l1, d), F32),
        scratch_shapes=[pltpu.VMEM((2 * d // LANES, FFT_L2 * kb, LANES), F32),
                        pltpu.VMEM((d // LANES, FFT_L2 * kb, LANES), F32)],
        compiler_params=_cparams(("parallel", "parallel")),
    )(z5, z5, x.reshape(b, FFT_L2, l1, d), m2, mods4, g, w1, w2)
    return out.reshape(b, s_len, d)


def kernel(x, c, ctx, c_ctx, ada_w, ada_b, norm_g, attn_wqkv, attn_wo, q_norm_g, k_norm_g, rpb,
           fourier_w, mlp_w1, mlp_w2):
    b, s_len, d = x.shape
    depth = ada_w.shape[0]
    hd = q_norm_g.shape[-1]
    assert depth == 2 and b + 1 <= 8 and d % (2 * hd) == 0
    assert s_len % (FFT_L2 * SUBLANES) == 0 and s_len % TOKEN_TILE == 0
    ctx_row = b

    cc = jnp.zeros((8, d), F32).at[:b].set(c).at[b].set(c_ctx)
    mods4 = _mods(cc, ada_w, ada_b).reshape(depth, 8, 6, d)

    for layer in range(depth):
        is_attn = layer % N_MIXERS == 0
        idx = layer // N_MIXERS
        need_ctx = layer < depth - 1
        g_pre = norm_g[layer, 0][None]
        g_post = norm_g[layer, 1][None]
        w1 = mlp_w1[layer].astype(BF16)
        w2 = mlp_w2[layer].astype(BF16)
        if is_attn:
            wqkv = attn_wqkv[idx].astype(BF16)
            wo = attn_wo[idx].astype(BF16)
            gq2 = jnp.tile(q_norm_g[idx], 2)[None]
            gk2 = jnp.tile(k_norm_g[idx], 2)[None]
            q, k, v = _qkv(x, mods4, layer, None, g_pre, wqkv, TOKEN_TILE)
            qc, kc, vc = _qkv(ctx, mods4, layer, ctx_row, g_pre, wqkv, ctx.shape[1])
            o, oc = _attention(q, k, v, qc, kc, vc, gq2, gk2, _bias_table(rpb[idx]))
            x = _proj_mlp(x, o, mods4, layer, None, g_post, wo, w1, w2, TOKEN_TILE)
            if need_ctx:
                ctx = _proj_mlp(ctx, oc, mods4, layer, ctx_row, g_post, wo, w1, w2, ctx.shape[1])
        else:
            assert not need_ctx, "a Fourier layer that still feeds a context stream is not supported"
            cd, sd, m1, m2 = _dft_tables(s_len, d // FOURIER_GROUPS)
            wf = _fold_fourier_weight(cd, sd, fourier_w[idx])
            z = _fourier_stage1(x, mods4, layer, g_pre, wf, m1)
            x = _fourier_stage2_mlp(z, x, mods4, layer, g_post, m2, w1, w2)
    return x
```

```python
import functools
import math

import numpy as np
import jax
import jax.numpy as jnp
from jax import lax
from jax.experimental import pallas as pl
from jax.experimental.pallas import tpu as pltpu

F32 = jnp.float32
BF16 = jnp.bfloat16

GRID_W = 64
WIN_ROWS = 8
WIN_COLS = 16
FOURIER_GROUPS = 4
N_MIXERS = 2
EPS = 1e-6
NEG_INF = float("-inf")
LOG2E = math.log2(math.e)

N_DR_TILES = 2 * WIN_ROWS - 2
FFT_L2 = 64
SUBLANES = 8
LANES = 128
TOKEN_TILE = 512
FF_CHUNK = 1024
VMEM_LIMIT = 56 * 1024 * 1024


def _cparams(sem):
    return pltpu.CompilerParams(dimension_semantics=sem, vmem_limit_bytes=VMEM_LIMIT)


def _resident(shape, index_map):
    return pl.BlockSpec(shape, index_map, pipeline_mode=pl.Buffered(1))


def _mods_body(c_ref, w_ref, b_ref, o_ref):
    c = c_ref[...]
    s = (c * jax.nn.sigmoid(c)).astype(BF16)
    o_ref[...] = jnp.dot(s, w_ref[...].astype(BF16), preferred_element_type=F32) + b_ref[...]


def _mods(cc, ada_w, ada_b):
    n_layers, d, d6 = ada_w.shape
    tn = 1536
    return pl.pallas_call(
        _mods_body,
        grid=(n_layers, d6 // tn),
        in_specs=[
            pl.BlockSpec((8, d), lambda l, n: (0, 0)),
            pl.BlockSpec((None, d, tn), lambda l, n: (l, 0, n)),
            pl.BlockSpec((None, 1, tn), lambda l, n: (l, 0, n)),
        ],
        out_specs=pl.BlockSpec((None, 8, tn), lambda l, n: (l, 0, n)),
        out_shape=jax.ShapeDtypeStruct((n_layers, 8, d6), F32),
        compiler_params=_cparams(("parallel", "parallel")),
    )(cc, ada_w, ada_b.reshape(n_layers, 1, d6))


def _norm_mod(x, g, shift, scale):
    ms = jnp.mean(x * x, axis=-1, keepdims=True)
    return (x * lax.rsqrt(ms + EPS) * g) * (1.0 + scale) + shift


def _sq_relu_mlp(h, w1_ref, w2_ref):
    acc = None
    for c in range(0, w1_ref.shape[1], FF_CHUNK):
        a = jnp.maximum(jnp.dot(h, w1_ref[:, c:c + FF_CHUNK], preferred_element_type=F32), 0.0)
        part = jnp.dot((a * a).astype(BF16), w2_ref[c:c + FF_CHUNK, :], preferred_element_type=F32)
        acc = part if acc is None else acc + part
    return acc


def _qkv_body(x_ref, mod_ref, g_ref, w_ref, q_ref, k_ref, v_ref):
    d = x_ref.shape[-1]
    h = _norm_mod(x_ref[...], g_ref[...], mod_ref[0:1, :], mod_ref[1:2, :]).astype(BF16)
    q_ref[...] = jnp.dot(h, w_ref[:, 0:d], preferred_element_type=F32).astype(BF16)
    k_ref[...] = jnp.dot(h, w_ref[:, d:2 * d], preferred_element_type=F32).astype(BF16)
    v_ref[...] = jnp.dot(h, w_ref[:, 2 * d:3 * d], preferred_element_type=F32).astype(BF16)


def _qkv(x, mods4, layer, ctx_row, g, w_bf16, tm):
    b, t, d = x.shape
    mod_idx = (lambda bi, ti: (layer, bi, 0, 0)) if ctx_row is None else (lambda bi, ti: (layer, ctx_row, 0, 0))
    tok = pl.BlockSpec((None, tm, d), lambda bi, ti: (bi, ti, 0))
    out = jax.ShapeDtypeStruct((b, t, d), BF16)
    return pl.pallas_call(
        _qkv_body,
        grid=(b, t // tm),
        in_specs=[
            tok,
            pl.BlockSpec((None, None, 6, d), mod_idx),
            pl.BlockSpec((1, d), lambda bi, ti: (0, 0)),
            _resident((d, 3 * d), lambda bi, ti: (0, 0)),
        ],
        out_specs=[tok, tok, tok],
        out_shape=[out, out, out],
        compiler_params=_cparams(("parallel", "parallel")),
    )(x, mods4, g, w_bf16)


def _bias_body(r_ref, o_ref, *, tn):
    col0 = pl.program_id(0) * tn
    e = col0 + lax.broadcasted_iota(jnp.int32, (64, tn), 1)
    r = lax.broadcasted_iota(jnp.int32, (64, tn), 0)
    cq = e // (2 * GRID_W)
    half = (e // GRID_W) % 2
    ck = e % GRID_W
    sel = ((r // 32 == half) & (r % 32 == ck - cq + WIN_COLS - 1)).astype(BF16)
    rv = r_ref[...]
    hi = rv.astype(BF16)
    rem = rv - hi.astype(F32)
    mid = rem.astype(BF16)
    lo = (rem - mid.astype(F32)).astype(BF16)
    acc = jnp.dot(hi, sel, preferred_element_type=F32)
    acc += jnp.dot(mid, sel, preferred_element_type=F32)
    acc += jnp.dot(lo, sel, preferred_element_type=F32)
    e1 = col0 + lax.broadcasted_iota(jnp.int32, (1, tn), 1)
    cq1 = e1 // (2 * GRID_W)
    ck1 = e1 % GRID_W
    start = jnp.clip(cq1 - WIN_COLS // 2, 0, GRID_W - WIN_COLS)
    ok = (ck1 >= start) & (ck1 < start + WIN_COLS)
    o_ref[...] = acc * LOG2E + jnp.where(ok, 0.0, NEG_INF)


def _bias_table(rpb):
    h, nr, nc = rpb.shape
    rp = jnp.pad(rpb, ((0, 0), (0, 0), (0, 32 - nc)))
    r2 = jnp.concatenate([rp[:, 0:N_DR_TILES], rp[:, 1:N_DR_TILES + 1]], axis=-1)
    r2 = r2.reshape(h * N_DR_TILES, 64)
    n_cols = GRID_W * 2 * GRID_W
    tn = 2048
    out = pl.pallas_call(
        functools.partial(_bias_body, tn=tn),
        grid=(n_cols // tn,),
        in_specs=[pl.BlockSpec((h * N_DR_TILES, 64), lambda n: (0, 0))],
        out_specs=pl.BlockSpec((h * N_DR_TILES, tn), lambda n: (0, n)),
        out_shape=jax.ShapeDtypeStruct((h * N_DR_TILES, n_cols), F32),
        compiler_params=_cparams(("parallel",)),
    )(r2)
    return out.reshape(h, N_DR_TILES, GRID_W, 2 * GRID_W)


def _head_norm(t, gain, ones_blk):
    ssq = jnp.dot((t * t).astype(BF16), ones_blk, preferred_element_type=F32)
    hd = t.shape[-1] // 2
    return t * lax.rsqrt(ssq * (1.0 / hd) + EPS) * gain


def _softmax2_pv(s_parts, v_parts):
    m = s_parts[0].max(axis=-1, keepdims=True)
    for s in s_parts[1:]:
        m = jnp.maximum(m, s.max(axis=-1, keepdims=True))
    l = None
    acc = None
    for s, v in zip(s_parts, v_parts):
        p = jnp.exp2(s - m)
        ps = p.sum(axis=-1, keepdims=True)
        pv = jnp.dot(p.astype(BF16), v, preferred_element_type=F32)
        l = ps if l is None else l + ps
        acc = pv if acc is None else acc + pv
    return acc / l


def _attn_body(q_ref, k_ref, v_ref, qc_ref, kc_ref, vc_ref, gq_ref, gk_ref, bias_ref, o_ref, oc_ref,
               qn_s, kn_s, s_s, p_s, l_s, *, n_rows):
    s_len = q_ref.shape[0]
    c_len = qc_ref.shape[0]
    hd = q_ref.shape[1] // 2
    kwin = WIN_ROWS * GRID_W
    nt = (((1,), (1,)), ((), ()))

    lane = lax.broadcasted_iota(jnp.int32, (1, 2 * hd), 1)
    first = lane < hd
    ri = lax.broadcasted_iota(jnp.int32, (2 * hd, 2 * hd), 0)
    ci = lax.broadcasted_iota(jnp.int32, (2 * hd, 2 * hd), 1)
    ones_blk = (ri // hd == ci // hd).astype(BF16)
    gq = gq_ref[...] * (hd ** -0.5 * LOG2E)
    gk = gk_ref[...]

    chunk = 512

    def norm_chunk(src_q, src_k, src_off, dst_off, n):
        qn_s[pl.ds(dst_off, n), :] = _head_norm(src_q[pl.ds(src_off, n), :].astype(F32), gq, ones_blk).astype(BF16)
        kn_s[pl.ds(dst_off, n), :] = _head_norm(src_k[pl.ds(src_off, n), :].astype(F32), gk, ones_blk).astype(BF16)

    def norm_loop(i, carry):
        off = pl.multiple_of(i * chunk, chunk)
        norm_chunk(q_ref, k_ref, off, off, chunk)
        return carry

    lax.fori_loop(0, s_len // chunk, norm_loop, 0)
    norm_chunk(qc_ref, kc_ref, 0, s_len, c_len)

    kc = kn_s[pl.ds(s_len, c_len), :]
    vc = vc_ref[...]

    def stack_heads(q):
        zero = jnp.zeros_like(q)
        return jnp.concatenate([jnp.where(first, q, zero), jnp.where(first, zero, q)], axis=0)

    def pick_heads(acc):
        n = acc.shape[0] // 2
        return jnp.where(first, acc[:n], acc[n:])

    def win_start(r):
        return jnp.clip(r - WIN_ROWS // 2, 0, n_rows - WIN_ROWS)

    def scores(r, slot):
        rs = win_start(r)
        qq = stack_heads(qn_s[pl.ds(pl.multiple_of(r * GRID_W, GRID_W), GRID_W), :])
        kw = kn_s[pl.ds(pl.multiple_of(rs * GRID_W, GRID_W), kwin), :]
        t0 = rs - r + WIN_ROWS - 1
        bias = jnp.concatenate(
            [jnp.concatenate([bias_ref[hh, t0 + 2 * ii] for ii in range(WIN_ROWS // 2)], axis=1)
             for hh in range(2)], axis=0)
        s_s[slot, :, 0:kwin] = lax.dot_general(qq, kw, nt, preferred_element_type=F32) + bias
        s_s[slot, :, kwin:kwin + c_len] = lax.dot_general(qq, kc, nt, preferred_element_type=F32)

    def probs(slot):
        s = s_s[slot]
        p = jnp.exp2(s - s.max(axis=-1, keepdims=True))
        p_s[slot] = p.astype(BF16)
        l_s[slot] = 1.0 / p.sum(axis=-1, keepdims=True)

    def output(r, slot):
        vw = v_ref[pl.ds(pl.multiple_of(win_start(r) * GRID_W, GRID_W), kwin), :]
        acc = jnp.dot(p_s[slot, :, 0:kwin], vw, preferred_element_type=F32)
        acc += jnp.dot(p_s[slot, :, kwin:kwin + c_len], vc, preferred_element_type=F32)
        o_ref[pl.ds(pl.multiple_of(r * GRID_W, GRID_W), GRID_W), :] = pick_heads(acc * l_s[slot]).astype(o_ref.dtype)

    def step(t, do_out=True, do_probs=True, do_scores=True):
        for a in range(2):
            if do_out:
                output(t - 4 + a, a)
        for a in range(2):
            if do_probs:
                probs(a)
        for a in range(2):
            if do_scores:
                scores(t + a, a)

    step(0, do_out=False, do_probs=False)
    step(2, do_out=False)

    def loop_step(j, carry):
        step(2 * j)
        return carry

    lax.fori_loop(2, n_rows // 2, loop_step, 0)
    step(n_rows, do_scores=False)
    step(n_rows + 2, do_probs=False, do_scores=False)

    qqc = stack_heads(qn_s[pl.ds(s_len, c_len), :])
    s_cc = lax.dot_general(qqc, kc, nt, preferred_element_type=F32)
    oc_ref[...] = pick_heads(_softmax2_pv([s_cc], [vc])).astype(oc_ref.dtype)


def _attention(q, k, v, qc, kc, vc, gq2, gk2, bias_tab):
    b, s_len, d = q.shape
    c_len = qc.shape[1]
    pw = gq2.shape[-1]
    n_rows = s_len // GRID_W
    n_keys = WIN_ROWS * GRID_W + c_len
    assert n_rows >= WIN_ROWS and n_rows % 2 == 0 and n_rows >= 4 and s_len % 512 == 0
    lat = pl.BlockSpec((None, s_len, pw), lambda bi, pi: (bi, 0, pi))
    cx = pl.BlockSpec((None, c_len, pw), lambda bi, pi: (bi, 0, pi))
    gain = pl.BlockSpec((1, pw), lambda bi, pi: (0, 0))
    return pl.pallas_call(
        functools.partial(_attn_body, n_rows=n_rows),
        grid=(b, d // pw),
        in_specs=[lat, lat, lat, cx, cx, cx, gain, gain,
                  pl.BlockSpec((2, N_DR_TILES, GRID_W, 2 * GRID_W), lambda bi, pi: (pi, 0, 0, 0))],
        out_specs=[lat, cx],
        out_shape=[jax.ShapeDtypeStruct((b, s_len, d), BF16), jax.ShapeDtypeStruct((b, c_len, d), BF16)],
        scratch_shapes=[pltpu.VMEM((s_len + c_len, pw), BF16)] * 2 + [
            pltpu.VMEM((2, 2 * GRID_W, n_keys), F32), pltpu.VMEM((2, 2 * GRID_W, n_keys), BF16),
            pltpu.VMEM((2, 2 * GRID_W, 1), F32)],
        compiler_params=_cparams(("parallel", "parallel")),
    )(q, k, v, qc, kc, vc, gq2, gk2, bias_tab)


def _proj_mlp_body(x_ref, o_ref, mod_ref, g_ref, wo_ref, w1_ref, w2_ref, out_ref):
    y = jnp.dot(o_ref[...], wo_ref[...], preferred_element_type=F32)
    x = x_ref[...] + mod_ref[2:3, :] * y
    h = _norm_mod(x, g_ref[...], mod_ref[3:4, :], mod_ref[4:5, :]).astype(BF16)
    out_ref[...] = x + mod_ref[5:6, :] * _sq_relu_mlp(h, w1_ref, w2_ref)


def _proj_mlp(x, o, mods4, layer, ctx_row, g, wo, w1, w2, tm):
    b, t, d = x.shape
    d_ff = w1.shape[1]
    mod_idx = (lambda bi, ti: (layer, bi, 0, 0)) if ctx_row is None else (lambda bi, ti: (layer, ctx_row, 0, 0))
    tok = pl.BlockSpec((None, tm, d), lambda bi, ti: (bi, ti, 0))
    const = lambda bi, ti: (0, 0)
    return pl.pallas_call(
        _proj_mlp_body,
        grid=(b, t // tm),
        in_specs=[tok, tok, pl.BlockSpec((None, None, 6, d), mod_idx), pl.BlockSpec((1, d), const),
                  _resident((d, d), const), _resident((d, d_ff), const), _resident((d_ff, d), const)],
        out_specs=tok,
        out_shape=jax.ShapeDtypeStruct((b, t, d), F32),
        compiler_params=_cparams(("parallel", "parallel")),
    )(x, o, mods4, g, wo, w1, w2)


def _dft_tables(n, dg):
    l2 = FFT_L2
    l1 = n // l2
    ar = np.arange
    a = 2.0 * np.pi * np.outer(ar(dg), ar(dg)) / dg
    norm = 1.0 / math.sqrt(n * dg)
    cd, sd = np.cos(a) * norm, np.sin(a) * norm
    al = 2.0 * np.pi * np.outer(ar(l1), ar(l1)) / l1
    c1, s1 = np.cos(al), np.sin(al)
    m1 = np.block([[c1, -s1], [-s1, -c1]])
    k = ar(l1)[:, None, None] + l1 * ar(l2)[None, :, None]
    be = 2.0 * np.pi * (k * ar(l2)[None, None, :] % n) / n
    m2 = np.concatenate([np.cos(be), np.sin(be)], axis=-1)
    return (jnp.asarray(cd, F32), jnp.asarray(sd, F32), jnp.asarray(m1, F32), jnp.asarray(m2, F32))


def _put_cols(s_ref, first, val, rows=slice(None)):
    for c in range(val.shape[1] // LANES):
        s_ref[first + c, rows, :] = val[:, c * LANES:(c + 1) * LANES]


def _get_cols(s_ref, first, n_chunks, rows=slice(None)):
    return jnp.concatenate([s_ref[first + c, rows, :] for c in range(n_chunks)], axis=1)


def _fold_body(cd_ref, sd_ref, w_ref, o_ref):
    w = w_ref[...]
    hp = lax.Precision.HIGHEST
    o_ref[0] = jnp.dot(cd_ref[...], w, precision=hp, preferred_element_type=F32).astype(BF16)
    o_ref[1] = jnp.dot(sd_ref[...], w, precision=hp, preferred_element_type=F32).astype(BF16)


def _fold_fourier_weight(cd, sd, w):
    d = w.shape[0]
    dg = cd.shape[0]
    return pl.pallas_call(
        _fold_body,
        grid=(d // dg,),
        in_specs=[pl.BlockSpec((dg, dg), lambda g: (0, 0)), pl.BlockSpec((dg, dg), lambda g: (0, 0)),
                  pl.BlockSpec((dg, d), lambda g: (g, 0))],
        out_specs=pl.BlockSpec((2, dg, d), lambda g: (0, g, 0)),
        out_shape=jax.ShapeDtypeStruct((2, d, d), BF16),
        compiler_params=_cparams(("parallel",)),
    )(cd, sd, w)


def _dft1_body(x_ref, mod_ref, g_ref, w_ref, m_ref, z_ref, p_s):
    l1, nb, d = x_ref.shape
    x = x_ref[...].reshape(l1 * nb, d)
    h = _norm_mod(x, g_ref[...], mod_ref[0:1, :], mod_ref[1:2, :]).astype(BF16)
    nc = d // LANES
    _put_cols(p_s, 0, jnp.dot(h, w_ref[0], preferred_element_type=F32))
    _put_cols(p_s, nc, jnp.dot(h, w_ref[1], preferred_element_type=F32))
    m1 = m_ref[...].astype(BF16)
    for m in range(nb):
        rows = pl.ds(m, l1, stride=nb)
        pm = jnp.concatenate([_get_cols(p_s, 0, nc, rows), _get_cols(p_s, nc, nc, rows)], axis=0).astype(BF16)
        z_ref[m] = jnp.dot(m1, pm, preferred_element_type=F32)


def _fourier_stage1(x, mods4, layer, g, wf, m1):
    b, s_len, d = x.shape
    l1 = s_len // FFT_L2
    nb = SUBLANES
    return pl.pallas_call(
        _dft1_body,
        grid=(b, FFT_L2 // nb),
        in_specs=[
            pl.BlockSpec((None, l1, nb, d), lambda bi, ti: (bi, 0, ti, 0)),
            pl.BlockSpec((None, None, 6, d), lambda bi, ti: (layer, bi, 0, 0)),
            pl.BlockSpec((1, d), lambda bi, ti: (0, 0)),
            _resident((2, d, d), lambda bi, ti: (0, 0, 0)),
            pl.BlockSpec((2 * l1, 2 * l1), lambda bi, ti: (0, 0)),
        ],
        out_specs=pl.BlockSpec((None, nb, 2 * l1, d), lambda bi, ti: (bi, ti, 0, 0)),
        out_shape=jax.ShapeDtypeStruct((b, FFT_L2, 2 * l1, d), F32),
        scratch_shapes=[pltpu.VMEM((2 * d // LANES, l1 * nb, LANES), F32)],
        compiler_params=_cparams(("parallel", "parallel")),
    )(x.reshape(b, l1, FFT_L2, d), mods4, g, wf, m1)


def _dft2_mlp_body(zr_ref, zi_ref, x_ref, m_ref, mod_ref, g_ref, w1_ref, w2_ref, out_ref, z_s, y_s):
    l2, kb, d = x_ref.shape
    n = l2 * kb
    nc = d // LANES
    _put_cols(z_s, 0, zr_ref[...].reshape(n, d))
    _put_cols(z_s, nc, zi_ref[...].reshape(n, d))
    for kk in range(kb):
        rows = pl.ds(kk, l2, stride=kb)
        zz = jnp.concatenate([_get_cols(z_s, 0, nc, rows), _get_cols(z_s, nc, nc, rows)],
                             axis=0).astype(BF16)
        _put_cols(y_s, 0, jnp.dot(m_ref[kk].astype(BF16), zz, preferred_element_type=F32), rows)
    x = x_ref[...].reshape(n, d) + mod_ref[2:3, :] * _get_cols(y_s, 0, nc)
    h = _norm_mod(x, g_ref[...], mod_ref[3:4, :], mod_ref[4:5, :]).astype(BF16)
    out = x + mod_ref[5:6, :] * _sq_relu_mlp(h, w1_ref, w2_ref)
    out_ref[...] = out.reshape(l2, kb, d)


def _fourier_stage2_mlp(z, x, mods4, layer, g, m2, w1, w2):
    b, s_len, d = x.shape
    l1 = s_len // FFT_L2
    kb = SUBLANES
    d_ff = w1.shape[1]
    zblk = lambda part: pl.BlockSpec((None, FFT_L2, None, kb, d), lambda bi, ki: (bi, 0, part, ki, 0))
    xblk = pl.BlockSpec((None, FFT_L2, kb, d), lambda bi, ki: (bi, 0, ki, 0))
    const = lambda bi, ki: (0, 0)
    z5 = z.reshape(b, FFT_L2, 2, l1, d)
    out = pl.pallas_call(
        _dft2_mlp_body,
        grid=(b, l1 // kb),
        in_specs=[
            zblk(0), zblk(1), xblk,
            pl.BlockSpec((kb, FFT_L2, 2 * FFT_L2), lambda bi, ki: (ki, 0, 0)),
            pl.BlockSpec((None, None, 6, d), lambda bi, ki: (layer, bi, 0, 0)),
            pl.BlockSpec((1, d), const),
            _resident((d, d_ff), const), _resident((d_ff, d), const),
        ],
        out_specs=xblk,
        out_shape=jax.ShapeDtypeStruct((b, FFT_L2, l1, d), F32),
        scratch_shapes=[pltpu.VMEM((2 * d // LANES, FFT_L2 * kb, LANES), F32),
                        pltpu.VMEM((d // LANES, FFT_L2 * kb, LANES), F32)],
        compiler_params=_cparams(("parallel", "parallel")),
    )(z5, z5, x.reshape(b, FFT_L2, l1, d), m2, mods4, g, w1, w2)
    return out.reshape(b, s_len, d)


def kernel(x, c, ctx, c_ctx, ada_w, ada_b, norm_g, attn_wqkv, attn_wo, q_norm_g, k_norm_g, rpb,
           fourier_w, mlp_w1, mlp_w2):
    b, s_len, d = x.shape
    depth = ada_w.shape[0]
    hd = q_norm_g.shape[-1]
    assert depth == 2 and b + 1 <= 8 and d % (2 * hd) == 0
    assert s_len % (FFT_L2 * SUBLANES) == 0 and s_len % TOKEN_TILE == 0
    ctx_row = b

    cc = jnp.zeros((8, d), F32).at[:b].set(c).at[b].set(c_ctx)
    mods4 = _mods(cc, ada_w, ada_b).reshape(depth, 8, 6, d)

    for layer in range(depth):
        is_attn = layer % N_MIXERS == 0
        idx = layer // N_MIXERS
        need_ctx = layer < depth - 1
        g_pre = norm_g[layer, 0][None]
        g_post = norm_g[layer, 1][None]
        w1 = mlp_w1[layer].astype(BF16)
        w2 = mlp_w2[layer].astype(BF16)
        if is_attn:
            wqkv = attn_wqkv[idx].astype(BF16)
            wo = attn_wo[idx].astype(BF16)
            gq2 = jnp.tile(q_norm_g[idx], 2)[None]
            gk2 = jnp.tile(k_norm_g[idx], 2)[None]
            q, k, v = _qkv(x, mods4, layer, None, g_pre, wqkv, TOKEN_TILE)
            qc, kc, vc = _qkv(ctx, mods4, layer, ctx_row, g_pre, wqkv, ctx.shape[1])
            o, oc = _attention(q, k, v, qc, kc, vc, gq2, gk2, _bias_table(rpb[idx]))
            x = _proj_mlp(x, o, mods4, layer, None, g_post, wo, w1, w2, TOKEN_TILE)
            if need_ctx:
                ctx = _proj_mlp(ctx, oc, mods4, layer, ctx_row, g_post, wo, w1, w2, ctx.shape[1])
        else:
            assert not need_ctx, "a Fourier layer that still feeds a context stream is not supported"
            cd, sd, m1, m2 = _dft_tables(s_len, d // FOURIER_GROUPS)
            wf = _fold_fourier_weight(cd, sd, fourier_w[idx])
            z = _fourier_stage1(x, mods4, layer, g_pre, wf, m1)
            x = _fourier_stage2_mlp(z, x, mods4, layer, g_post, m2, w1, w2)
    return x
```

```python
import functools
import math

import numpy as np
import jax
import jax.numpy as jnp
from jax import lax
from jax.experimental import pallas as pl
from jax.experimental.pallas import tpu as pltpu

F32 = jnp.float32
BF16 = jnp.bfloat16

GRID_W = 64
WIN_ROWS = 8
WIN_COLS = 16
FOURIER_GROUPS = 4
N_MIXERS = 2
EPS = 1e-6
NEG_INF = float("-inf")
LOG2E = math.log2(math.e)

N_DR_TILES = 2 * WIN_ROWS - 2
ATTN_ROWS_PER_STEP = 4
FFT_L2 = 64
SUBLANES = 8
LANES = 128
TOKEN_TILE = 512
FF_CHUNK = 1024
VMEM_LIMIT = 56 * 1024 * 1024


def _cparams(sem):
    return pltpu.CompilerParams(dimension_semantics=sem, vmem_limit_bytes=VMEM_LIMIT)


def _resident(shape, index_map):
    return pl.BlockSpec(shape, index_map, pipeline_mode=pl.Buffered(1))


def _mods_body(c_ref, w_ref, b_ref, o_ref):
    c = c_ref[...]
    s = (c * jax.nn.sigmoid(c)).astype(BF16)
    o_ref[...] = jnp.dot(s, w_ref[...].astype(BF16), preferred_element_type=F32) + b_ref[...]


def _mods(cc, ada_w, ada_b):
    n_layers, d, d6 = ada_w.shape
    tn = 1536
    return pl.pallas_call(
        _mods_body,
        grid=(n_layers, d6 // tn),
        in_specs=[
            pl.BlockSpec((8, d), lambda l, n: (0, 0)),
            pl.BlockSpec((None, d, tn), lambda l, n: (l, 0, n)),
            pl.BlockSpec((None, 1, tn), lambda l, n: (l, 0, n)),
        ],
        out_specs=pl.BlockSpec((None, 8, tn), lambda l, n: (l, 0, n)),
        out_shape=jax.ShapeDtypeStruct((n_layers, 8, d6), F32),
        compiler_params=_cparams(("parallel", "parallel")),
    )(cc, ada_w, ada_b.reshape(n_layers, 1, d6))


def _norm_mod(x, g, shift, scale):
    ms = jnp.mean(x * x, axis=-1, keepdims=True)
    return (x * lax.rsqrt(ms + EPS) * g) * (1.0 + scale) + shift


def _sq_relu_mlp(h, w1_ref, w2_ref):
    acc = None
    for c in range(0, w1_ref.shape[1], FF_CHUNK):
        a = jnp.maximum(jnp.dot(h, w1_ref[:, c:c + FF_CHUNK], preferred_element_type=F32), 0.0)
        part = jnp.dot((a * a).astype(BF16), w2_ref[c:c + FF_CHUNK, :], preferred_element_type=F32)
        acc = part if acc is None else acc + part
    return acc


def _qkv_body(x_ref, mod_ref, g_ref, w_ref, q_ref, k_ref, v_ref):
    d = x_ref.shape[-1]
    h = _norm_mod(x_ref[...], g_ref[...], mod_ref[0:1, :], mod_ref[1:2, :]).astype(BF16)
    q_ref[...] = jnp.dot(h, w_ref[:, 0:d], preferred_element_type=F32).astype(BF16)
    k_ref[...] = jnp.dot(h, w_ref[:, d:2 * d], preferred_element_type=F32).astype(BF16)
    v_ref[...] = jnp.dot(h, w_ref[:, 2 * d:3 * d], preferred_element_type=F32).astype(BF16)


def _qkv(x, mods4, layer, ctx_row, g, w_bf16, tm):
    b, t, d = x.shape
    mod_idx = (lambda bi, ti: (layer, bi, 0, 0)) if ctx_row is None else (lambda bi, ti: (layer, ctx_row, 0, 0))
    tok = pl.BlockSpec((None, tm, d), lambda bi, ti: (bi, ti, 0))
    out = jax.ShapeDtypeStruct((b, t, d), BF16)
    return pl.pallas_call(
        _qkv_body,
        grid=(b, t // tm),
        in_specs=[
            tok,
            pl.BlockSpec((None, None, 6, d), mod_idx),
            pl.BlockSpec((1, d), lambda bi, ti: (0, 0)),
            _resident((d, 3 * d), lambda bi, ti: (0, 0)),
        ],
        out_specs=[tok, tok, tok],
        out_shape=[out, out, out],
        compiler_params=_cparams(("parallel", "parallel")),
    )(x, mods4, g, w_bf16)


def _bias_body(r_ref, o_ref, *, tn):
    col0 = pl.program_id(0) * tn
    e = col0 + lax.broadcasted_iota(jnp.int32, (64, tn), 1)
    r = lax.broadcasted_iota(jnp.int32, (64, tn), 0)
    cq = e // (2 * GRID_W)
    half = (e // GRID_W) % 2
    ck = e % GRID_W
    sel = ((r // 32 == half) & (r % 32 == ck - cq + WIN_COLS - 1)).astype(BF16)
    rv = r_ref[...]
    hi = rv.astype(BF16)
    rem = rv - hi.astype(F32)
    mid = rem.astype(BF16)
    lo = (rem - mid.astype(F32)).astype(BF16)
    acc = jnp.dot(hi, sel, preferred_element_type=F32)
    acc += jnp.dot(mid, sel, preferred_element_type=F32)
    acc += jnp.dot(lo, sel, preferred_element_type=F32)
    e1 = col0 + lax.broadcasted_iota(jnp.int32, (1, tn), 1)
    cq1 = e1 // (2 * GRID_W)
    ck1 = e1 % GRID_W
    start = jnp.clip(cq1 - WIN_COLS // 2, 0, GRID_W - WIN_COLS)
    ok = (ck1 >= start) & (ck1 < start + WIN_COLS)
    o_ref[...] = acc * LOG2E + jnp.where(ok, 0.0, NEG_INF)


def _bias_table(rpb):
    h, nr, nc = rpb.shape
    rp = jnp.pad(rpb, ((0, 0), (0, 0), (0, 32 - nc)))
    r2 = jnp.concatenate([rp[:, 0:N_DR_TILES], rp[:, 1:N_DR_TILES + 1]], axis=-1)
    r2 = r2.reshape(h * N_DR_TILES, 64)
    n_cols = GRID_W * 2 * GRID_W
    tn = 2048
    out = pl.pallas_call(
        functools.partial(_bias_body, tn=tn),
        grid=(n_cols // tn,),
        in_specs=[pl.BlockSpec((h * N_DR_TILES, 64), lambda n: (0, 0))],
        out_specs=pl.BlockSpec((h * N_DR_TILES, tn), lambda n: (0, n)),
        out_shape=jax.ShapeDtypeStruct((h * N_DR_TILES, n_cols), F32),
        compiler_params=_cparams(("parallel",)),
    )(r2)
    return out.reshape(h, N_DR_TILES, GRID_W, 2 * GRID_W)


def _head_norm(t, gain, ones_blk):
    ssq = jnp.dot((t * t).astype(BF16), ones_blk, preferred_element_type=F32)
    hd = t.shape[-1] // 2
    return t * lax.rsqrt(ssq * (1.0 / hd) + EPS) * gain


def _softmax2_pv(s_parts, v_parts):
    m = s_parts[0].max(axis=-1, keepdims=True)
    for s in s_parts[1:]:
        m = jnp.maximum(m, s.max(axis=-1, keepdims=True))
    l = None
    acc = None
    for s, v in zip(s_parts, v_parts):
        p = jnp.exp2(s - m)
        ps = p.sum(axis=-1, keepdims=True)
        pv = jnp.dot(p.astype(BF16), v, preferred_element_type=F32)
        l = ps if l is None else l + ps
        acc = pv if acc is None else acc + pv
    return acc / l


def _attn_body(q_ref, k_ref, v_ref, qc_ref, kc_ref, vc_ref, gq_ref, gk_ref, bias_ref, o_ref, oc_ref,
               qn_s, kn_s, s_s, p_s, l_s, *, n_rows):
    s_len = q_ref.shape[0]
    c_len = qc_ref.shape[0]
    hd = q_ref.shape[1] // 2
    kwin = WIN_ROWS * GRID_W
    nt = (((1,), (1,)), ((), ()))

    lane = lax.broadcasted_iota(jnp.int32, (1, 2 * hd), 1)
    first = lane < hd
    ri = lax.broadcasted_iota(jnp.int32, (2 * hd, 2 * hd), 0)
    ci = lax.broadcasted_iota(jnp.int32, (2 * hd, 2 * hd), 1)
    ones_blk = (ri // hd == ci // hd).astype(BF16)
    gq = gq_ref[...] * (hd ** -0.5 * LOG2E)
    gk = gk_ref[...]

    chunk = 512

    def norm_chunk(src_q, src_k, src_off, dst_off, n):
        qn_s[pl.ds(dst_off, n), :] = _head_norm(src_q[pl.ds(src_off, n), :].astype(F32), gq, ones_blk).astype(BF16)
        kn_s[pl.ds(dst_off, n), :] = _head_norm(src_k[pl.ds(src_off, n), :].astype(F32), gk, ones_blk).astype(BF16)

    def norm_loop(i, carry):
        off = pl.multiple_of(i * chunk, chunk)
        norm_chunk(q_ref, k_ref, off, off, chunk)
        return carry

    lax.fori_loop(0, s_len // chunk, norm_loop, 0)
    norm_chunk(qc_ref, kc_ref, 0, s_len, c_len)

    kc = kn_s[pl.ds(s_len, c_len), :]
    vc = vc_ref[...]

    def stack_heads(q):
        zero = jnp.zeros_like(q)
        return jnp.concatenate([jnp.where(first, q, zero), jnp.where(first, zero, q)], axis=0)

    def pick_heads(acc):
        n = acc.shape[0] // 2
        return jnp.where(first, acc[:n], acc[n:])

    def win_start(r):
        return jnp.clip(r - WIN_ROWS // 2, 0, n_rows - WIN_ROWS)

    def scores(r, slot):
        rs = win_start(r)
        qq = stack_heads(qn_s[pl.ds(pl.multiple_of(r * GRID_W, GRID_W), GRID_W), :])
        kw = kn_s[pl.ds(pl.multiple_of(rs * GRID_W, GRID_W), kwin), :]
        t0 = rs - r + WIN_ROWS - 1
        bias = jnp.concatenate(
            [jnp.concatenate([bias_ref[hh, t0 + 2 * ii] for ii in range(WIN_ROWS // 2)], axis=1)
             for hh in range(2)], axis=0)
        s_s[slot, :, 0:kwin] = lax.dot_general(qq, kw, nt, preferred_element_type=F32) + bias
        s_s[slot, :, kwin:kwin + c_len] = lax.dot_general(qq, kc, nt, preferred_element_type=F32)

    def probs(slot):
        s = s_s[slot]
        p = jnp.exp2(s - s.max(axis=-1, keepdims=True))
        p_s[slot] = p.astype(BF16)
        l_s[slot] = 1.0 / p.sum(axis=-1, keepdims=True)

    def output(r, slot):
        vw = v_ref[pl.ds(pl.multiple_of(win_start(r) * GRID_W, GRID_W), kwin), :]
        acc = jnp.dot(p_s[slot, :, 0:kwin], vw, preferred_element_type=F32)
        acc += jnp.dot(p_s[slot, :, kwin:kwin + c_len], vc, preferred_element_type=F32)
        o_ref[pl.ds(pl.multiple_of(r * GRID_W, GRID_W), GRID_W), :] = pick_heads(acc * l_s[slot]).astype(o_ref.dtype)

    rps = ATTN_ROWS_PER_STEP
    n_blocks = n_rows // rps

    def step(k, do_out=True, do_probs=True, do_scores=True):
        blk, par = k
        for a in range(rps):
            if do_out:
                output((blk - 2) * rps + a, par * rps + a)
        for a in range(rps):
            if do_probs:
                probs((1 - par) * rps + a)
        for a in range(rps):
            if do_scores:
                scores(blk * rps + a, par * rps + a)

    step((0, 0), do_out=False, do_probs=False)
    step((1, 1), do_out=False)

    def loop_step(j, carry):
        step((2 * j, 0))
        step((2 * j + 1, 1))
        return carry

    lax.fori_loop(1, n_blocks // 2, loop_step, 0)
    step((n_blocks, 0), do_scores=False)
    step((n_blocks + 1, 1), do_probs=False, do_scores=False)

    qqc = stack_heads(qn_s[pl.ds(s_len, c_len), :])
    s_cc = lax.dot_general(qqc, kc, nt, preferred_element_type=F32)
    oc_ref[...] = pick_heads(_softmax2_pv([s_cc], [vc])).astype(oc_ref.dtype)


def _attention(q, k, v, qc, kc, vc, gq2, gk2, bias_tab):
    b, s_len, d = q.shape
    c_len = qc.shape[1]
    pw = gq2.shape[-1]
    n_rows = s_len // GRID_W
    n_keys = WIN_ROWS * GRID_W + c_len
    assert n_rows >= WIN_ROWS and n_rows % (2 * ATTN_ROWS_PER_STEP) == 0
    assert s_len % 512 == 0
    lat = pl.BlockSpec((None, s_len, pw), lambda bi, pi: (bi, 0, pi))
    cx = pl.BlockSpec((None, c_len, pw), lambda bi, pi: (bi, 0, pi))
    gain = pl.BlockSpec((1, pw), lambda bi, pi: (0, 0))
    return pl.pallas_call(
        functools.partial(_attn_body, n_rows=n_rows),
        grid=(b, d // pw),
        in_specs=[lat, lat, lat, cx, cx, cx, gain, gain,
                  pl.BlockSpec((2, N_DR_TILES, GRID_W, 2 * GRID_W), lambda bi, pi: (pi, 0, 0, 0))],
        out_specs=[lat, cx],
        out_shape=[jax.ShapeDtypeStruct((b, s_len, d), BF16), jax.ShapeDtypeStruct((b, c_len, d), BF16)],
        scratch_shapes=[pltpu.VMEM((s_len + c_len, pw), BF16)] * 2 + [
            pltpu.VMEM((2 * ATTN_ROWS_PER_STEP, 2 * GRID_W, n_keys), F32),
            pltpu.VMEM((2 * ATTN_ROWS_PER_STEP, 2 * GRID_W, n_keys), BF16),
            pltpu.VMEM((2 * ATTN_ROWS_PER_STEP, 2 * GRID_W, 1), F32)],
        compiler_params=_cparams(("parallel", "parallel")),
    )(q, k, v, qc, kc, vc, gq2, gk2, bias_tab)


def _proj_mlp_body(x_ref, o_ref, mod_ref, g_ref, wo_ref, w1_ref, w2_ref, out_ref):
    y = jnp.dot(o_ref[...], wo_ref[...], preferred_element_type=F32)
    x = x_ref[...] + mod_ref[2:3, :] * y
    h = _norm_mod(x, g_ref[...], mod_ref[3:4, :], mod_ref[4:5, :]).astype(BF16)
    out_ref[...] = x + mod_ref[5:6, :] * _sq_relu_mlp(h, w1_ref, w2_ref)


def _proj_mlp(x, o, mods4, layer, ctx_row, g, wo, w1, w2, tm):
    b, t, d = x.shape
    d_ff = w1.shape[1]
    mod_idx = (lambda bi, ti: (layer, bi, 0, 0)) if ctx_row is None else (lambda bi, ti: (layer, ctx_row, 0, 0))
    tok = pl.BlockSpec((None, tm, d), lambda bi, ti: (bi, ti, 0))
    const = lambda bi, ti: (0, 0)
    return pl.pallas_call(
        _proj_mlp_body,
        grid=(b, t // tm),
        in_specs=[tok, tok, pl.BlockSpec((None, None, 6, d), mod_idx), pl.BlockSpec((1, d), const),
                  _resident((d, d), const), _resident((d, d_ff), const), _resident((d_ff, d), const)],
        out_specs=tok,
        out_shape=jax.ShapeDtypeStruct((b, t, d), F32),
        compiler_params=_cparams(("parallel", "parallel")),
    )(x, o, mods4, g, wo, w1, w2)


def _dft_tables(n, dg):
    l2 = FFT_L2
    l1 = n // l2
    ar = np.arange
    a = 2.0 * np.pi * np.outer(ar(dg), ar(dg)) / dg
    norm = 1.0 / math.sqrt(n * dg)
    cd, sd = np.cos(a) * norm, np.sin(a) * norm
    al = 2.0 * np.pi * np.outer(ar(l1), ar(l1)) / l1
    c1, s1 = np.cos(al), np.sin(al)
    m1 = np.block([[c1, -s1], [-s1, -c1]])
    k = ar(l1)[:, None, None] + l1 * ar(l2)[None, :, None]
    be = 2.0 * np.pi * (k * ar(l2)[None, None, :] % n) / n
    m2 = np.concatenate([np.cos(be), np.sin(be)], axis=-1)
    return (jnp.asarray(cd, F32), jnp.asarray(sd, F32), jnp.asarray(m1, F32), jnp.asarray(m2, F32))


def _put_cols(s_ref, first, val, rows=slice(None)):
    for c in range(val.shape[1] // LANES):
        s_ref[first + c, rows, :] = val[:, c * LANES:(c + 1) * LANES]


def _get_cols(s_ref, first, n_chunks, rows=slice(None)):
    return jnp.concatenate([s_ref[first + c, rows, :] for c in range(n_chunks)], axis=1)


def _fold_body(cd_ref, sd_ref, w_ref, o_ref):
    w = w_ref[...]
    hp = lax.Precision.HIGHEST
    o_ref[0] = jnp.dot(cd_ref[...], w, precision=hp, preferred_element_type=F32).astype(BF16)
    o_ref[1] = jnp.dot(sd_ref[...], w, precision=hp, preferred_element_type=F32).astype(BF16)


def _fold_fourier_weight(cd, sd, w):
    d = w.shape[0]
    dg = cd.shape[0]
    return pl.pallas_call(
        _fold_body,
        grid=(d // dg,),
        in_specs=[pl.BlockSpec((dg, dg), lambda g: (0, 0)), pl.BlockSpec((dg, dg), lambda g: (0, 0)),
                  pl.BlockSpec((dg, d), lambda g: (g, 0))],
        out_specs=pl.BlockSpec((2, dg, d), lambda g: (0, g, 0)),
        out_shape=jax.ShapeDtypeStruct((2, d, d), BF16),
        compiler_params=_cparams(("parallel",)),
    )(cd, sd, w)


def _dft1_body(x_ref, mod_ref, g_ref, w_ref, m_ref, z_ref, p_s):
    l1, nb, d = x_ref.shape
    x = x_ref[...].reshape(l1 * nb, d)
    h = _norm_mod(x, g_ref[...], mod_ref[0:1, :], mod_ref[1:2, :]).astype(BF16)
    nc = d // LANES
    _put_cols(p_s, 0, jnp.dot(h, w_ref[0], preferred_element_type=F32))
    _put_cols(p_s, nc, jnp.dot(h, w_ref[1], preferred_element_type=F32))
    m1 = m_ref[...].astype(BF16)
    for m in range(nb):
        rows = pl.ds(m, l1, stride=nb)
        pm = jnp.concatenate([_get_cols(p_s, 0, nc, rows), _get_cols(p_s, nc, nc, rows)], axis=0).astype(BF16)
        z_ref[m] = jnp.dot(m1, pm, preferred_element_type=F32)


def _fourier_stage1(x, mods4, layer, g, wf, m1):
    b, s_len, d = x.shape
    l1 = s_len // FFT_L2
    nb = SUBLANES
    return pl.pallas_call(
        _dft1_body,
        grid=(b, FFT_L2 // nb),
        in_specs=[
            pl.BlockSpec((None, l1, nb, d), lambda bi, ti: (bi, 0, ti, 0)),
            pl.BlockSpec((None, None, 6, d), lambda bi, ti: (layer, bi, 0, 0)),
            pl.BlockSpec((1, d), lambda bi, ti: (0, 0)),
            _resident((2, d, d), lambda bi, ti: (0, 0, 0)),
            pl.BlockSpec((2 * l1, 2 * l1), lambda bi, ti: (0, 0)),
        ],
        out_specs=pl.BlockSpec((None, nb, 2 * l1, d), lambda bi, ti: (bi, ti, 0, 0)),
        out_shape=jax.ShapeDtypeStruct((b, FFT_L2, 2 * l1, d), F32),
        scratch_shapes=[pltpu.VMEM((2 * d // LANES, l1 * nb, LANES), F32)],
        compiler_params=_cparams(("parallel", "parallel")),
    )(x.reshape(b, l1, FFT_L2, d), mods4, g, wf, m1)


def _dft2_mlp_body(zr_ref, zi_ref, x_ref, m_ref, mod_ref, g_ref, w1_ref, w2_ref, out_ref, z_s, y_s):
    l2, kb, d = x_ref.shape
    n = l2 * kb
    nc = d // LANES
    _put_cols(z_s, 0, zr_ref[...].reshape(n, d))
    _put_cols(z_s, nc, zi_ref[...].reshape(n, d))
    for kk in range(kb):
        rows = pl.ds(kk, l2, stride=kb)
        zz = jnp.concatenate([_get_cols(z_s, 0, nc, rows), _get_cols(z_s, nc, nc, rows)],
                             axis=0).astype(BF16)
        _put_cols(y_s, 0, jnp.dot(m_ref[kk].astype(BF16), zz, preferred_element_type=F32), rows)
    x = x_ref[...].reshape(n, d) + mod_ref[2:3, :] * _get_cols(y_s, 0, nc)
    h = _norm_mod(x, g_ref[...], mod_ref[3:4, :], mod_ref[4:5, :]).astype(BF16)
    out = x + mod_ref[5:6, :] * _sq_relu_mlp(h, w1_ref, w2_ref)
    out_ref[...] = out.reshape(l2, kb, d)


def _fourier_stage2_mlp(z, x, mods4, layer, g, m2, w1, w2):
    b, s_len, d = x.shape
    l1 = s_len // FFT_L2
    kb = SUBLANES
    d_ff = w1.shape[1]
    zblk = lambda part: pl.BlockSpec((None, FFT_L2, None, kb, d), lambda bi, ki: (bi, 0, part, ki, 0))
    xblk = pl.BlockSpec((None, FFT_L2, kb, d), lambda bi, ki: (bi, 0, ki, 0))
    const = lambda bi, ki: (0, 0)
    z5 = z.reshape(b, FFT_L2, 2, l1, d)
    out = pl.pallas_call(
        _dft2_mlp_body,
        grid=(b, l1 // kb),
        in_specs=[
            zblk(0), zblk(1), xblk,
            pl.BlockSpec((kb, FFT_L2, 2 * FFT_L2), lambda bi, ki: (ki, 0, 0)),
            pl.BlockSpec((None, None, 6, d), lambda bi, ki: (layer, bi, 0, 0)),
            pl.BlockSpec((1, d), const),
            _resident((d, d_ff), const), _resident((d_ff, d), const),
        ],
        out_specs=xblk,
        out_shape=jax.ShapeDtypeStruct((b, FFT_L2, l1, d), F32),
        scratch_shapes=[pltpu.VMEM((2 * d // LANES, FFT_L2 * kb, LANES), F32),
                        pltpu.VMEM((d // LANES, FFT_L2 * kb, LANES), F32)],
        compiler_params=_cparams(("parallel", "parallel")),
    )(z5, z5, x.reshape(b, FFT_L2, l1, d), m2, mods4, g, w1, w2)
    return out.reshape(b, s_len, d)


def kernel(x, c, ctx, c_ctx, ada_w, ada_b, norm_g, attn_wqkv, attn_wo, q_norm_g, k_norm_g, rpb,
           fourier_w, mlp_w1, mlp_w2):
    b, s_len, d = x.shape
    depth = ada_w.shape[0]
    hd = q_norm_g.shape[-1]
    assert depth == 2 and b + 1 <= 8 and d % (2 * hd) == 0
    assert s_len % (FFT_L2 * SUBLANES) == 0 and s_len % TOKEN_TILE == 0
    ctx_row = b

    cc = jnp.zeros((8, d), F32).at[:b].set(c).at[b].set(c_ctx)
    mods4 = _mods(cc, ada_w, ada_b).reshape(depth, 8, 6, d)

    for layer in range(depth):
        is_attn = layer % N_MIXERS == 0
        idx = layer // N_MIXERS
        need_ctx = layer < depth - 1
        g_pre = norm_g[layer, 0][None]
        g_post = norm_g[layer, 1][None]
        w1 = mlp_w1[layer].astype(BF16)
        w2 = mlp_w2[layer].astype(BF16)
        if is_attn:
            wqkv = attn_wqkv[idx].astype(BF16)
            wo = attn_wo[idx].astype(BF16)
            gq2 = jnp.tile(q_norm_g[idx], 2)[None]
            gk2 = jnp.tile(k_norm_g[idx], 2)[None]
            q, k, v = _qkv(x, mods4, layer, None, g_pre, wqkv, TOKEN_TILE)
            qc, kc, vc = _qkv(ctx, mods4, layer, ctx_row, g_pre, wqkv, ctx.shape[1])
            o, oc = _attention(q, k, v, qc, kc, vc, gq2, gk2, _bias_table(rpb[idx]))
            x = _proj_mlp(x, o, mods4, layer, None, g_post, wo, w1, w2, TOKEN_TILE)
            if need_ctx:
                ctx = _proj_mlp(ctx, oc, mods4, layer, ctx_row, g_post, wo, w1, w2, ctx.shape[1])
        else:
            assert not need_ctx, "a Fourier layer that still feeds a context stream is not supported"
            cd, sd, m1, m2 = _dft_tables(s_len, d // FOURIER_GROUPS)
            wf = _fold_fourier_weight(cd, sd, fourier_w[idx])
            z = _fourier_stage1(x, mods4, layer, g_pre, wf, m1)
            x = _fourier_stage2_mlp(z, x, mods4, layer, g_post, m2, w1, w2)
    return x
```

```python
import functools
import math

import numpy as np
import jax
import jax.numpy as jnp
from jax import lax
from jax.experimental import pallas as pl
from jax.experimental.pallas import tpu as pltpu

F32 = jnp.float32
BF16 = jnp.bfloat16

GRID_W = 64
WIN_ROWS = 8
WIN_COLS = 16
FOURIER_GROUPS = 4
N_MIXERS = 2
EPS = 1e-6
NEG_INF = float("-inf")
LOG2E = math.log2(math.e)

N_DR_TILES = 2 * WIN_ROWS - 2
ATTN_ROWS_PER_STEP = 4
FFT_L2 = 64
SUBLANES = 8
LANES = 128
TOKEN_TILE = 512
FF_CHUNK = 1024
VMEM_LIMIT = 56 * 1024 * 1024


def _cparams(sem):
    return pltpu.CompilerParams(dimension_semantics=sem, vmem_limit_bytes=VMEM_LIMIT)


def _resident(shape, index_map):
    return pl.BlockSpec(shape, index_map, pipeline_mode=pl.Buffered(1))


def _mods_body(c_ref, w_ref, b_ref, o_ref):
    c = c_ref[...]
    s = (c * jax.nn.sigmoid(c)).astype(BF16)
    o_ref[...] = jnp.dot(s, w_ref[...].astype(BF16), preferred_element_type=F32) + b_ref[...]


def _mods(cc, ada_w, ada_b):
    n_layers, d, d6 = ada_w.shape
    tn = 1536
    return pl.pallas_call(
        _mods_body,
        grid=(n_layers, d6 // tn),
        in_specs=[
            pl.BlockSpec((8, d), lambda l, n: (0, 0)),
            pl.BlockSpec((None, d, tn), lambda l, n: (l, 0, n)),
            pl.BlockSpec((None, 1, tn), lambda l, n: (l, 0, n)),
        ],
        out_specs=pl.BlockSpec((None, 8, tn), lambda l, n: (l, 0, n)),
        out_shape=jax.ShapeDtypeStruct((n_layers, 8, d6), F32),
        compiler_params=_cparams(("parallel", "parallel")),
    )(cc, ada_w, ada_b.reshape(n_layers, 1, d6))


def _norm_mod(x, g, shift, scale):
    ms = jnp.mean(x * x, axis=-1, keepdims=True)
    return (x * lax.rsqrt(ms + EPS) * g) * (1.0 + scale) + shift


def _sq_relu_mlp(h, w1_ref, w2_ref):
    acc = None
    for c in range(0, w1_ref.shape[1], FF_CHUNK):
        a = jnp.maximum(jnp.dot(h, w1_ref[:, c:c + FF_CHUNK], preferred_element_type=F32), 0.0)
        part = jnp.dot((a * a).astype(BF16), w2_ref[c:c + FF_CHUNK, :], preferred_element_type=F32)
        acc = part if acc is None else acc + part
    return acc


def _qkv_body(x_ref, mod_ref, g_ref, w_ref, q_ref, k_ref, v_ref):
    d = x_ref.shape[-1]
    h = _norm_mod(x_ref[...], g_ref[...], mod_ref[0:1, :], mod_ref[1:2, :]).astype(BF16)
    q_ref[...] = jnp.dot(h, w_ref[:, 0:d], preferred_element_type=F32).astype(BF16)
    k_ref[...] = jnp.dot(h, w_ref[:, d:2 * d], preferred_element_type=F32).astype(BF16)
    v_ref[...] = jnp.dot(h, w_ref[:, 2 * d:3 * d], preferred_element_type=F32).astype(BF16)


def _qkv(x, mods4, layer, ctx_row, g, w_bf16, tm):
    b, t, d = x.shape
    mod_idx = (lambda bi, ti: (layer, bi, 0, 0)) if ctx_row is None else (lambda bi, ti: (layer, ctx_row, 0, 0))
    tok = pl.BlockSpec((None, tm, d), lambda bi, ti: (bi, ti, 0))
    out = jax.ShapeDtypeStruct((b, t, d), BF16)
    return pl.pallas_call(
        _qkv_body,
        grid=(b, t // tm),
        in_specs=[
            tok,
            pl.BlockSpec((None, None, 6, d), mod_idx),
            pl.BlockSpec((1, d), lambda bi, ti: (0, 0)),
            _resident((d, 3 * d), lambda bi, ti: (0, 0)),
        ],
        out_specs=[tok, tok, tok],
        out_shape=[out, out, out],
        compiler_params=_cparams(("parallel", "parallel")),
    )(x, mods4, g, w_bf16)


def _bias_body(r_ref, o_ref, *, tn):
    col0 = pl.program_id(0) * tn
    e = col0 + lax.broadcasted_iota(jnp.int32, (64, tn), 1)
    r = lax.broadcasted_iota(jnp.int32, (64, tn), 0)
    cq = e // (2 * GRID_W)
    half = (e // GRID_W) % 2
    ck = e % GRID_W
    sel = ((r // 32 == half) & (r % 32 == ck - cq + WIN_COLS - 1)).astype(BF16)
    rv = r_ref[...]
    hi = rv.astype(BF16)
    rem = rv - hi.astype(F32)
    mid = rem.astype(BF16)
    lo = (rem - mid.astype(F32)).astype(BF16)
    acc = jnp.dot(hi, sel, preferred_element_type=F32)
    acc += jnp.dot(mid, sel, preferred_element_type=F32)
    acc += jnp.dot(lo, sel, preferred_element_type=F32)
    e1 = col0 + lax.broadcasted_iota(jnp.int32, (1, tn), 1)
    cq1 = e1 // (2 * GRID_W)
    ck1 = e1 % GRID_W
    start = jnp.clip(cq1 - WIN_COLS // 2, 0, GRID_W - WIN_COLS)
    ok = (ck1 >= start) & (ck1 < start + WIN_COLS)
    vals = acc * LOG2E + jnp.where(ok, 0.0, NEG_INF)
    n_tables = r_ref.shape[0]
    w = 2 * GRID_W
    for c in range(tn // w):
        cq_c = pl.program_id(0) * (tn // w) + c
        o_ref[pl.ds(cq_c, n_tables, stride=GRID_W), :] = vals[:, c * w:(c + 1) * w]


def _bias_table(rpb):
    h, nr, nc = rpb.shape
    rp = jnp.pad(rpb, ((0, 0), (0, 0), (0, 32 - nc)))
    r2 = jnp.concatenate([rp[:, 0:N_DR_TILES], rp[:, 1:N_DR_TILES + 1]], axis=-1)
    r2 = r2.reshape(h * N_DR_TILES, 64)
    n_cols = GRID_W * 2 * GRID_W
    tn = 2048
    out = pl.pallas_call(
        functools.partial(_bias_body, tn=tn),
        grid=(n_cols // tn,),
        in_specs=[pl.BlockSpec((h * N_DR_TILES, 64), lambda n: (0, 0))],
        out_specs=pl.BlockSpec((h * N_DR_TILES * GRID_W, 2 * GRID_W), lambda n: (0, 0)),
        out_shape=jax.ShapeDtypeStruct((h * N_DR_TILES * GRID_W, 2 * GRID_W), F32),
        compiler_params=_cparams(("arbitrary",)),
    )(r2)
    return out.reshape(h, N_DR_TILES, GRID_W, 2 * GRID_W)


def _attn_body(q_ref, k_ref, v_ref, qc_ref, kc_ref, vc_ref, gq_ref, gk_ref, bias_ref, o_ref, oc_ref,
               qn_s, kn_s, s_s, p_s, l_s, sc_s, pc_s, lc_s, *, n_rows):
    s_len = q_ref.shape[0]
    c_len = qc_ref.shape[0]
    hd = q_ref.shape[1] // 2
    kwin = WIN_ROWS * GRID_W
    nt = (((1,), (1,)), ((), ()))
    rps = ATTN_ROWS_PER_STEP
    n_blocks = n_rows // rps
    blk_tokens = rps * GRID_W

    lane = lax.broadcasted_iota(jnp.int32, (1, 2 * hd), 1)
    first = lane < hd
    ri = lax.broadcasted_iota(jnp.int32, (4 * hd, 4 * hd), 0)
    ci = lax.broadcasted_iota(jnp.int32, (4 * hd, 4 * hd), 1)
    mean_blk = jnp.where(ri // hd == ci // hd, 1.0 / hd, 0.0).astype(BF16)
    gain = gq_ref[...] * gk_ref[...] * (hd ** -0.5 * LOG2E)

    def normalise(src_q, src_k, src_off, dst_off, n):
        q = src_q[pl.ds(src_off, n), :].astype(F32)
        k = src_k[pl.ds(src_off, n), :].astype(F32)
        ms = jnp.dot(jnp.concatenate([q * q, k * k], axis=1).astype(BF16), mean_blk, preferred_element_type=F32)
        qn_s[pl.ds(dst_off, n), :] = (q * lax.rsqrt(ms[:, 0:2 * hd] + EPS) * gain).astype(BF16)
        kn_s[pl.ds(dst_off, n), :] = (k * lax.rsqrt(ms[:, 2 * hd:4 * hd] + EPS)).astype(BF16)

    def normalise_block(blk):
        off = pl.multiple_of(blk * blk_tokens, blk_tokens)
        normalise(q_ref, k_ref, off, off, blk_tokens)

    def kc():
        return kn_s[pl.ds(s_len, c_len), :]

    def stack_heads(q):
        zero = jnp.zeros_like(q)
        return jnp.concatenate([jnp.where(first, q, zero), jnp.where(first, zero, q)], axis=0)

    def pick_heads(acc):
        n = acc.shape[0] // 2
        return jnp.where(first, acc[:n], acc[n:])

    def win_start(r):
        return jnp.clip(r - WIN_ROWS // 2, 0, n_rows - WIN_ROWS)

    def scores(r, slot):
        rs = win_start(r)
        qq = stack_heads(qn_s[pl.ds(pl.multiple_of(r * GRID_W, GRID_W), GRID_W), :])
        kw = kn_s[pl.ds(pl.multiple_of(rs * GRID_W, GRID_W), kwin), :]
        t0 = rs - r + WIN_ROWS - 1
        bias = jnp.concatenate(
            [jnp.concatenate([bias_ref[hh, t0 + 2 * ii] for ii in range(WIN_ROWS // 2)], axis=1)
             for hh in range(2)], axis=0)
        s_s[slot, :, 0:kwin] = lax.dot_general(qq, kw, nt, preferred_element_type=F32) + bias
        s_s[slot, :, kwin:kwin + c_len] = lax.dot_general(qq, kc(), nt, preferred_element_type=F32)

    def softmax2(s):
        p = jnp.exp2(s - s.max(axis=-1, keepdims=True))
        return p.astype(BF16), 1.0 / p.sum(axis=-1, keepdims=True)

    def probs(slot):
        p_s[slot], l_s[slot] = softmax2(s_s[slot])

    def output(r, slot):
        vw = v_ref[pl.ds(pl.multiple_of(win_start(r) * GRID_W, GRID_W), kwin), :]
        acc = jnp.dot(p_s[slot], jnp.concatenate([vw, vc_ref[...]], axis=0), preferred_element_type=F32)
        o_ref[pl.ds(pl.multiple_of(r * GRID_W, GRID_W), GRID_W), :] = pick_heads(acc * l_s[slot]).astype(o_ref.dtype)

    def ctx_scores():
        qqc = stack_heads(qn_s[pl.ds(s_len, c_len), :])
        sc_s[...] = lax.dot_general(qqc, kc(), nt, preferred_element_type=F32)

    def ctx_probs():
        pc_s[...], lc_s[...] = softmax2(sc_s[...])

    def ctx_output():
        acc = jnp.dot(pc_s[...], vc_ref[...], preferred_element_type=F32)
        oc_ref[...] = pick_heads(acc * lc_s[...]).astype(oc_ref.dtype)

    def step(blk, par, do_out=True, do_probs=True, do_scores=True):
        for a in range(rps):
            if do_out:
                output((blk - 2) * rps + a, par * rps + a)
        for a in range(rps):
            if do_probs:
                probs((1 - par) * rps + a)
        for a in range(rps):
            if do_scores:
                scores(blk * rps + a, par * rps + a)
        if do_scores:
            normalise_block(jnp.minimum(blk + 2, n_blocks - 1))

    normalise(qc_ref, kc_ref, 0, s_len, c_len)
    normalise_block(0)
    normalise_block(1)
    step(0, 0, do_out=False, do_probs=False)
    ctx_scores()
    step(1, 1, do_out=False)
    ctx_probs()

    def loop_step(j, carry):
        step(2 * j, 0)
        step(2 * j + 1, 1)
        return carry

    lax.fori_loop(1, n_blocks // 2, loop_step, 0)
    step(n_blocks, 0, do_scores=False)
    ctx_output()
    step(n_blocks + 1, 1, do_probs=False, do_scores=False)


def _attention(q, k, v, qc, kc, vc, gq2, gk2, bias_tab):
    b, s_len, d = q.shape
    c_len = qc.shape[1]
    pw = gq2.shape[-1]
    n_rows = s_len // GRID_W
    n_keys = WIN_ROWS * GRID_W + c_len
    assert n_rows >= WIN_ROWS and n_rows % (2 * ATTN_ROWS_PER_STEP) == 0
    assert ATTN_ROWS_PER_STEP >= WIN_ROWS // 2
    lat =pl.BlockSpec((None, s_len, pw), lambda bi, pi: (bi, 0, pi))
    cx = pl.BlockSpec((None, c_len, pw), lambda bi, pi: (bi, 0, pi))
    gain = pl.BlockSpec((1, pw), lambda bi, pi: (0, 0))
    return pl.pallas_call(
        functools.partial(_attn_body, n_rows=n_rows),
        grid=(b, d // pw),
        in_specs=[lat, lat, lat, cx, cx, cx, gain, gain,
                  pl.BlockSpec((2, N_DR_TILES, GRID_W, 2 * GRID_W), lambda bi, pi: (pi, 0, 0, 0))],
        out_specs=[lat, cx],
        out_shape=[jax.ShapeDtypeStruct((b, s_len, d), BF16), jax.ShapeDtypeStruct((b, c_len, d), BF16)],
        scratch_shapes=[pltpu.VMEM((s_len + c_len, pw), BF16)] * 2 + [
            pltpu.VMEM((2 * ATTN_ROWS_PER_STEP, 2 * GRID_W, n_keys), F32),
            pltpu.VMEM((2 * ATTN_ROWS_PER_STEP, 2 * GRID_W, n_keys), BF16),
            pltpu.VMEM((2 * ATTN_ROWS_PER_STEP, 2 * GRID_W, 1), F32),
            pltpu.VMEM((2 * c_len, c_len), F32), pltpu.VMEM((2 * c_len, c_len), BF16),
            pltpu.VMEM((2 * c_len, 1), F32)],
        compiler_params=_cparams(("parallel", "parallel")),
    )(q, k, v, qc, kc, vc, gq2, gk2, bias_tab)


def _proj_mlp_body(x_ref, o_ref, mod_ref, g_ref, wo_ref, w1_ref, w2_ref, out_ref):
    y = jnp.dot(o_ref[...], wo_ref[...], preferred_element_type=F32)
    x = x_ref[...] + mod_ref[2:3, :] * y
    h = _norm_mod(x, g_ref[...], mod_ref[3:4, :], mod_ref[4:5, :]).astype(BF16)
    out_ref[...] = x + mod_ref[5:6, :] * _sq_relu_mlp(h, w1_ref, w2_ref)


def _proj_mlp(x, o, mods4, layer, ctx_row, g, wo, w1, w2, tm):
    b, t, d = x.shape
    d_ff = w1.shape[1]
    mod_idx = (lambda bi, ti: (layer, bi, 0, 0)) if ctx_row is None else (lambda bi, ti: (layer, ctx_row, 0, 0))
    tok = pl.BlockSpec((None, tm, d), lambda bi, ti: (bi, ti, 0))
    const = lambda bi, ti: (0, 0)
    return pl.pallas_call(
        _proj_mlp_body,
        grid=(b, t // tm),
        in_specs=[tok, tok, pl.BlockSpec((None, None, 6, d), mod_idx), pl.BlockSpec((1, d), const),
                  _resident((d, d), const), _resident((d, d_ff), const), _resident((d_ff, d), const)],
        out_specs=tok,
        out_shape=jax.ShapeDtypeStruct((b, t, d), F32),
        compiler_params=_cparams(("parallel", "parallel")),
    )(x, o, mods4, g, wo, w1, w2)


def _dft_tables(n, dg):
    l2 = FFT_L2
    l1 = n // l2
    kb = SUBLANES
    ar = np.arange
    a = 2.0 * np.pi * np.outer(ar(dg), ar(dg)) / dg
    norm = 1.0 / math.sqrt(n * dg)
    cd, sd = np.cos(a) * norm, np.sin(a) * norm
    al = 2.0 * np.pi * np.outer(ar(l1), ar(l1)) / l1
    c1, s1 = np.cos(al), np.sin(al)
    m1 = np.block([[c1, -s1], [-s1, -c1]])
    k = ar(l1)[:, None, None] + l1 * ar(l2)[None, :, None]
    be = 2.0 * np.pi * (k * ar(l2)[None, None, :] % n) / n
    m2 = np.concatenate([np.cos(be), np.sin(be)], axis=-1)
    m2 = m2.reshape(l1 // kb, kb, l2, 2, l2)
    big = np.zeros((l1 // kb, l2, kb, 2, l2, kb))
    for kk in range(kb):
        big[:, :, kk, :, :, kk] = m2[:, kk]
    m2 = big.reshape(l1 // kb, l2 * kb, 2 * l2 * kb)
    return (jnp.asarray(cd, F32), jnp.asarray(sd, F32), jnp.asarray(m1, F32), jnp.asarray(m2, F32))


def _put_cols(s_ref, first, val, rows=slice(None)):
    for c in range(val.shape[1] // LANES):
        s_ref[first + c, rows, :] = val[:, c * LANES:(c + 1) * LANES]


def _get_cols(s_ref, first, n_chunks, rows=slice(None)):
    return jnp.concatenate([s_ref[first + c, rows, :] for c in range(n_chunks)], axis=1)


def _fold_body(cd_ref, sd_ref, w_ref, o_ref):
    w = w_ref[...]
    hp = lax.Precision.HIGHEST
    o_ref[0] = jnp.dot(cd_ref[...], w, precision=hp, preferred_element_type=F32).astype(BF16)
    o_ref[1] = jnp.dot(sd_ref[...], w, precision=hp, preferred_element_type=F32).astype(BF16)


def _fold_fourier_weight(cd, sd, w):
    d = w.shape[0]
    dg = cd.shape[0]
    return pl.pallas_call(
        _fold_body,
        grid=(d // dg,),
        in_specs=[pl.BlockSpec((dg, dg), lambda g: (0, 0)), pl.BlockSpec((dg, dg), lambda g: (0, 0)),
                  pl.BlockSpec((dg, d), lambda g: (g, 0))],
        out_specs=pl.BlockSpec((2, dg, d), lambda g: (0, g, 0)),
        out_shape=jax.ShapeDtypeStruct((2, d, d), BF16),
        compiler_params=_cparams(("parallel",)),
    )(cd, sd, w)


def _dft1_body(x_ref, mod_ref, g_ref, w_ref, m_ref, z_ref, p_s):
    l1, nb, d = x_ref.shape
    x = x_ref[...].reshape(l1 * nb, d)
    h = _norm_mod(x, g_ref[...], mod_ref[0:1, :], mod_ref[1:2, :]).astype(BF16)
    nc = d // LANES
    _put_cols(p_s, 0, jnp.dot(h, w_ref[0], preferred_element_type=F32))
    _put_cols(p_s, nc, jnp.dot(h, w_ref[1], preferred_element_type=F32))
    m1 = m_ref[...].astype(BF16)
    for m in range(nb):
        rows = pl.ds(m, l1, stride=nb)
        pm = jnp.concatenate([_get_cols(p_s, 0, nc, rows), _get_cols(p_s, nc, nc, rows)], axis=0).astype(BF16)
        z_ref[m] = jnp.dot(m1, pm, preferred_element_type=F32)


def _fourier_stage1(x, mods4, layer, g, wf, m1):
    b, s_len, d = x.shape
    l1 = s_len // FFT_L2
    nb = SUBLANES
    return pl.pallas_call(
        _dft1_body,
        grid=(b, FFT_L2 // nb),
        in_specs=[
            pl.BlockSpec((None, l1, nb, d), lambda bi, ti: (bi, 0, ti, 0)),
            pl.BlockSpec((None, None, 6, d), lambda bi, ti: (layer, bi, 0, 0)),
            pl.BlockSpec((1, d), lambda bi, ti: (0, 0)),
            _resident((2, d, d), lambda bi, ti: (0, 0, 0)),
            pl.BlockSpec((2 * l1, 2 * l1), lambda bi, ti: (0, 0)),
        ],
        out_specs=pl.BlockSpec((None, nb, 2 * l1, d), lambda bi, ti: (bi, ti, 0, 0)),
        out_shape=jax.ShapeDtypeStruct((b, FFT_L2, 2 * l1, d), F32),
        scratch_shapes=[pltpu.VMEM((2 * d // LANES, l1 * nb, LANES), F32)],
        compiler_params=_cparams(("parallel", "parallel")),
    )(x.reshape(b, l1, FFT_L2, d), mods4, g, wf, m1)


def _dft2_mlp_body(zr_ref, zi_ref, x_ref, m_ref, mod_ref, g_ref, w1_ref, w2_ref, out_ref):
    l2, kb, d = x_ref.shape
    n = l2 * kb
    zz = jnp.concatenate([zr_ref[...].reshape(n, d), zi_ref[...].reshape(n, d)], axis=0).astype(BF16)
    y = jnp.dot(m_ref[...].astype(BF16), zz, preferred_element_type=F32)
    x = x_ref[...].reshape(n, d) + mod_ref[2:3, :] * y
    h = _norm_mod(x, g_ref[...], mod_ref[3:4, :], mod_ref[4:5, :]).astype(BF16)
    out = x + mod_ref[5:6, :] * _sq_relu_mlp(h, w1_ref, w2_ref)
    out_ref[...] = out.reshape(l2, kb, d)


def _fourier_stage2_mlp(z, x, mods4, layer, g, m2, w1, w2):
    b, s_len, d = x.shape
    l1 = s_len // FFT_L2
    kb = SUBLANES
    d_ff = w1.shape[1]
    zblk = lambda part: pl.BlockSpec((None, FFT_L2, None, kb, d), lambda bi, ki: (bi, 0, part, ki, 0))
    xblk = pl.BlockSpec((None, FFT_L2, kb, d), lambda bi, ki: (bi, 0, ki, 0))
    const = lambda bi, ki: (0, 0)
    z5 = z.reshape(b, FFT_L2, 2, l1, d)
    out = pl.pallas_call(
        _dft2_mlp_body,
        grid=(b, l1 // kb),
        in_specs=[
            zblk(0), zblk(1), xblk,
            pl.BlockSpec((None, FFT_L2 * kb, 2 * FFT_L2 * kb), lambda bi, ki: (ki, 0, 0)),
            pl.BlockSpec((None, None, 6, d), lambda bi, ki: (layer, bi, 0, 0)),
            pl.BlockSpec((1, d), const),
            _resident((d, d_ff), const), _resident((d_ff, d), const),
        ],
        out_specs=xblk,
        out_shape=jax.ShapeDtypeStruct((b, FFT_L2, l1, d), F32),
        compiler_params=_cparams(("parallel", "parallel")),
    )(z5, z5, x.reshape(b, FFT_L2, l1, d), m2, mods4, g, w1, w2)
    return out.reshape(b, s_len, d)


def kernel(x, c, ctx, c_ctx, ada_w, ada_b, norm_g, attn_wqkv, attn_wo, q_norm_g, k_norm_g, rpb,
           fourier_w, mlp_w1, mlp_w2):
    b, s_len, d = x.shape
    depth = ada_w.shape[0]
    hd = q_norm_g.shape[-1]
    assert depth == 2 and b + 1 <= 8 and d % (2 * hd) == 0
    assert s_len % (FFT_L2 * SUBLANES) == 0 and s_len % TOKEN_TILE == 0
    ctx_row = b

    cc = jnp.zeros((8, d), F32).at[:b].set(c).at[b].set(c_ctx)
    mods4 = _mods(cc, ada_w, ada_b).reshape(depth, 8, 6, d)

    for layer in range(depth):
        is_attn = layer % N_MIXERS == 0
        idx = layer // N_MIXERS
        need_ctx = layer < depth - 1
        g_pre = norm_g[layer, 0][None]
        g_post = norm_g[layer, 1][None]
        w1 = mlp_w1[layer].astype(BF16)
        w2 = mlp_w2[layer].astype(BF16)
        if is_attn:
            wqkv = attn_wqkv[idx].astype(BF16)
            wo = attn_wo[idx].astype(BF16)
            gq2 = jnp.tile(q_norm_g[idx], 2)[None]
            gk2 = jnp.tile(k_norm_g[idx], 2)[None]
            q, k, v = _qkv(x, mods4, layer, None, g_pre, wqkv, TOKEN_TILE)
            qc, kc, vc = _qkv(ctx, mods4, layer, ctx_row, g_pre, wqkv, ctx.shape[1])
            o, oc = _attention(q, k, v, qc, kc, vc, gq2, gk2, _bias_table(rpb[idx]))
            x = _proj_mlp(x, o, mods4, layer, None, g_post, wo, w1, w2, TOKEN_TILE)
            if need_ctx:
                ctx = _proj_mlp(ctx, oc, mods4, layer, ctx_row, g_post, wo, w1, w2, ctx.shape[1])
        else:
            assert not need_ctx, "a Fourier layer that still feeds a context stream is not supported"
            cd, sd, m1, m2 = _dft_tables(s_len, d // FOURIER_GROUPS)
            wf = _fold_fourier_weight(cd, sd, fourier_w[idx])
            z = _fourier_stage1(x, mods4, layer, g_pre, wf, m1)
            x = _fourier_stage2_mlp(z, x, mods4, layer, g_post, m2, w1, w2)
    return x
```

```python
import functools
import math

import numpy as np
import jax
import jax.numpy as jnp
from jax import lax
from jax.experimental import pallas as pl
from jax.experimental.pallas import tpu as pltpu

F32 = jnp.float32
BF16 = jnp.bfloat16

GRID_W = 64
WIN_ROWS = 8
WIN_COLS = 16
FOURIER_GROUPS = 4
N_MIXERS = 2
EPS = 1e-6
NEG_INF = float("-inf")
LOG2E = math.log2(math.e)

N_DR_TILES = 2 * WIN_ROWS - 2
ATTN_ROWS_PER_STEP = 4
FFT_L2 = 64
SUBLANES = 8
LANES = 128
TOKEN_TILE = 512
FF_CHUNK = 1024
VMEM_LIMIT = 56 * 1024 * 1024


def _cparams(sem):
    return pltpu.CompilerParams(dimension_semantics=sem, vmem_limit_bytes=VMEM_LIMIT)


def _resident(shape, index_map):
    return pl.BlockSpec(shape, index_map, pipeline_mode=pl.Buffered(1))


def _mods_body(c_ref, w_ref, b_ref, o_ref):
    c = c_ref[...]
    s = (c * jax.nn.sigmoid(c)).astype(BF16)
    o_ref[...] = jnp.dot(s, w_ref[...].astype(BF16), preferred_element_type=F32) + b_ref[...]


def _mods(cc, ada_w, ada_b):
    n_layers, d, d6 = ada_w.shape
    tn = 1536
    return pl.pallas_call(
        _mods_body,
        grid=(n_layers, d6 // tn),
        in_specs=[
            pl.BlockSpec((8, d), lambda l, n: (0, 0)),
            pl.BlockSpec((None, d, tn), lambda l, n: (l, 0, n)),
            pl.BlockSpec((None, 1, tn), lambda l, n: (l, 0, n)),
        ],
        out_specs=pl.BlockSpec((None, 8, tn), lambda l, n: (l, 0, n)),
        out_shape=jax.ShapeDtypeStruct((n_layers, 8, d6), F32),
        compiler_params=_cparams(("parallel", "parallel")),
    )(cc, ada_w, ada_b.reshape(n_layers, 1, d6))


def _norm_mod(x, g, shift, scale):
    ms = jnp.mean(x * x, axis=-1, keepdims=True)
    return (x * lax.rsqrt(ms + EPS) * g) * (1.0 + scale) + shift


def _sq_relu_mlp(h, w1_ref, w2_ref):
    acc = None
    for c in range(0, w1_ref.shape[1], FF_CHUNK):
        a = jnp.maximum(jnp.dot(h, w1_ref[:, c:c + FF_CHUNK], preferred_element_type=F32), 0.0)
        part = jnp.dot((a * a).astype(BF16), w2_ref[c:c + FF_CHUNK, :], preferred_element_type=F32)
        acc = part if acc is None else acc + part
    return acc


def _qkv_body(x_ref, mod_ref, g_ref, w_ref, q_ref, k_ref, v_ref):
    d = x_ref.shape[-1]
    h = _norm_mod(x_ref[...], g_ref[...], mod_ref[0:1, :], mod_ref[1:2, :]).astype(BF16)
    q_ref[...] = jnp.dot(h, w_ref[:, 0:d], preferred_element_type=F32).astype(BF16)
    k_ref[...] = jnp.dot(h, w_ref[:, d:2 * d], preferred_element_type=F32).astype(BF16)
    v_ref[...] = jnp.dot(h, w_ref[:, 2 * d:3 * d], preferred_element_type=F32).astype(BF16)


def _qkv(x, mods4, layer, ctx_row, g, w_bf16, tm):
    b, t, d = x.shape
    mod_idx = (lambda bi, ti: (layer, bi, 0, 0)) if ctx_row is None else (lambda bi, ti: (layer, ctx_row, 0, 0))
    tok = pl.BlockSpec((None, tm, d), lambda bi, ti: (bi, ti, 0))
    out = jax.ShapeDtypeStruct((b, t, d), BF16)
    return pl.pallas_call(
        _qkv_body,
        grid=(b, t // tm),
        in_specs=[
            tok,
            pl.BlockSpec((None, None, 6, d), mod_idx),
            pl.BlockSpec((1, d), lambda bi, ti: (0, 0)),
            _resident((d, 3 * d), lambda bi, ti: (0, 0)),
        ],
        out_specs=[tok, tok, tok],
        out_shape=[out, out, out],
        compiler_params=_cparams(("parallel", "parallel")),
    )(x, mods4, g, w_bf16)


def _qkv_cast_body(x_ref, mod_ref, g_ref, wqkv_ref, wo_ref, w1_ref, w2_ref,
                   q_ref, k_ref, v_ref, wqkv_o, wo_o, w1_o, w2_o):
    @pl.when((pl.program_id(0) == 0) & (pl.program_id(1) == 0))
    def _():
        wqkv_o[...] = wqkv_ref[...].astype(BF16)

    wo_o[...] = wo_ref[...].astype(BF16)
    w1_o[...] = w1_ref[...].astype(BF16)
    w2_o[...] = w2_ref[...].astype(BF16)
    _qkv_body(x_ref, mod_ref, g_ref, wqkv_o, q_ref, k_ref, v_ref)


def _qkv_and_weight_casts(x, mods4, layer, g, wqkv, wo, mlp_w1, mlp_w2, tm):
    b, t, d = x.shape
    nt = t // tm
    n_steps = b * nt
    n_layers, _, d_ff = mlp_w1.shape
    assert d % (2 * SUBLANES * n_steps) == 0
    step = lambda bi, ti: bi * nt + ti
    tok = pl.BlockSpec((None, tm, d), lambda bi, ti: (bi, ti, 0))
    tok_out = jax.ShapeDtypeStruct((b, t, d), BF16)
    wo_blk = pl.BlockSpec((d // n_steps, d), lambda bi, ti: (step(bi, ti), 0))
    w1_blk = pl.BlockSpec((n_layers, d // n_steps, d_ff), lambda bi, ti: (0, step(bi, ti), 0))
    w2_blk = pl.BlockSpec((n_layers, d_ff // n_steps, d), lambda bi, ti: (0, step(bi, ti), 0))
    return pl.pallas_call(
        _qkv_cast_body,
        grid=(b, nt),
        in_specs=[
            tok,
            pl.BlockSpec((None, None, 6, d), lambda bi, ti: (layer, bi, 0, 0)),
            pl.BlockSpec((1, d), lambda bi, ti: (0, 0)),
            _resident((d, 3 * d), lambda bi, ti: (0, 0)),
            wo_blk, w1_blk, w2_blk,
        ],
        out_specs=[tok, tok, tok, pl.BlockSpec((d, 3 * d), lambda bi, ti: (0, 0)), wo_blk, w1_blk, w2_blk],
        out_shape=[tok_out, tok_out, tok_out,
                   jax.ShapeDtypeStruct((d, 3 * d), BF16), jax.ShapeDtypeStruct((d, d), BF16),
                   jax.ShapeDtypeStruct(mlp_w1.shape, BF16), jax.ShapeDtypeStruct(mlp_w2.shape, BF16)],
        compiler_params=_cparams(("arbitrary", "arbitrary")),
    )(x, mods4, g, wqkv, wo, mlp_w1, mlp_w2)


def _bias_body(r_ref, o_ref, *, tn):
    col0 = pl.program_id(0) * tn
    e = col0 + lax.broadcasted_iota(jnp.int32, (64, tn), 1)
    r = lax.broadcasted_iota(jnp.int32, (64, tn), 0)
    cq = e // (2 * GRID_W)
    half = (e // GRID_W) % 2
    ck = e % GRID_W
    sel = ((r // 32 == half) & (r % 32 == ck - cq + WIN_COLS - 1)).astype(BF16)
    rv = r_ref[...]
    hi = rv.astype(BF16)
    rem = rv - hi.astype(F32)
    mid = rem.astype(BF16)
    lo = (rem - mid.astype(F32)).astype(BF16)
    acc = jnp.dot(hi, sel, preferred_element_type=F32)
    acc += jnp.dot(mid, sel, preferred_element_type=F32)
    acc += jnp.dot(lo, sel, preferred_element_type=F32)
    e1 = col0 + lax.broadcasted_iota(jnp.int32, (1, tn), 1)
    cq1 = e1 // (2 * GRID_W)
    ck1 = e1 % GRID_W
    start = jnp.clip(cq1 - WIN_COLS // 2, 0, GRID_W - WIN_COLS)
    ok = (ck1 >= start) & (ck1 < start + WIN_COLS)
    vals = acc * LOG2E + jnp.where(ok, 0.0, NEG_INF)
    n_tables = r_ref.shape[0]
    w = 2 * GRID_W
    for c in range(tn // w):
        cq_c = pl.program_id(0) * (tn // w) + c
        o_ref[pl.ds(cq_c, n_tables, stride=GRID_W), :] = vals[:, c * w:(c + 1) * w]


def _bias_table(rpb):
    h, nr, nc = rpb.shape
    rp = jnp.pad(rpb, ((0, 0), (0, 0), (0, 32 - nc)))
    r2 = jnp.concatenate([rp[:, 0:N_DR_TILES], rp[:, 1:N_DR_TILES + 1]], axis=-1)
    r2 = r2.reshape(h * N_DR_TILES, 64)
    n_cols = GRID_W * 2 * GRID_W
    tn = 2048
    out = pl.pallas_call(
        functools.partial(_bias_body, tn=tn),
        grid=(n_cols // tn,),
        in_specs=[pl.BlockSpec((h * N_DR_TILES, 64), lambda n: (0, 0))],
        out_specs=pl.BlockSpec((h * N_DR_TILES * GRID_W, 2 * GRID_W), lambda n: (0, 0)),
        out_shape=jax.ShapeDtypeStruct((h * N_DR_TILES * GRID_W, 2 * GRID_W), F32),
        compiler_params=_cparams(("arbitrary",)),
    )(r2)
    return out.reshape(h, N_DR_TILES, GRID_W, 2 * GRID_W)


def _attn_body(q_ref, k_ref, v_ref, qc_ref, kc_ref, vc_ref, gq_ref, gk_ref, bias_ref, o_ref, oc_ref,
               qn_s, kn_s, s_s, p_s, l_s, sc_s, pc_s, lc_s, *, n_rows):
    s_len = q_ref.shape[0]
    c_len = qc_ref.shape[0]
    hd = q_ref.shape[1] // 2
    kwin = WIN_ROWS * GRID_W
    nt = (((1,), (1,)), ((), ()))
    rps = ATTN_ROWS_PER_STEP
    n_blocks = n_rows // rps
    blk_tokens = rps * GRID_W

    lane = lax.broadcasted_iota(jnp.int32, (1, 2 * hd), 1)
    first = lane < hd
    ri = lax.broadcasted_iota(jnp.int32, (4 * hd, 4 * hd), 0)
    ci = lax.broadcasted_iota(jnp.int32, (4 * hd, 4 * hd), 1)
    mean_blk = jnp.where(ri // hd == ci // hd, 1.0 / hd, 0.0).astype(BF16)
    gain = gq_ref[...] * gk_ref[...] * (hd ** -0.5 * LOG2E)

    def normalise(src_q, src_k, src_off, dst_off, n):
        q = src_q[pl.ds(src_off, n), :].astype(F32)
        k = src_k[pl.ds(src_off, n), :].astype(F32)
        ms = jnp.dot(jnp.concatenate([q * q, k * k], axis=1).astype(BF16), mean_blk, preferred_element_type=F32)
        qn_s[pl.ds(dst_off, n), :] = (q * lax.rsqrt(ms[:, 0:2 * hd] + EPS) * gain).astype(BF16)
        kn_s[pl.ds(dst_off, n), :] = (k * lax.rsqrt(ms[:, 2 * hd:4 * hd] + EPS)).astype(BF16)

    def normalise_block(blk):
        off = pl.multiple_of(blk * blk_tokens, blk_tokens)
        normalise(q_ref, k_ref, off, off, blk_tokens)

    def kc():
        return kn_s[pl.ds(s_len, c_len), :]

    def stack_heads(q):
        zero = jnp.zeros_like(q)
        return jnp.concatenate([jnp.where(first, q, zero), jnp.where(first, zero, q)], axis=0)

    def pick_heads(acc):
        n = acc.shape[0] // 2
        return jnp.where(first, acc[:n], acc[n:])

    def win_start(r):
        return jnp.clip(r - WIN_ROWS // 2, 0, n_rows - WIN_ROWS)

    def scores(r, slot):
        rs = win_start(r)
        qq = stack_heads(qn_s[pl.ds(pl.multiple_of(r * GRID_W, GRID_W), GRID_W), :])
        kw = kn_s[pl.ds(pl.multiple_of(rs * GRID_W, GRID_W), kwin), :]
        t0 = rs - r + WIN_ROWS - 1
        bias = jnp.concatenate(
            [jnp.concatenate([bias_ref[hh, t0 + 2 * ii] for ii in range(WIN_ROWS // 2)], axis=1)
             for hh in range(2)], axis=0)
        s_s[slot, :, 0:kwin] = lax.dot_general(qq, kw, nt, preferred_element_type=F32) + bias
        s_s[slot, :, kwin:kwin + c_len] = lax.dot_general(qq, kc(), nt, preferred_element_type=F32)

    def softmax2(s):
        p = jnp.exp2(s - s.max(axis=-1, keepdims=True))
        return p.astype(BF16), 1.0 / p.sum(axis=-1, keepdims=True)

    def probs(slot):
        p_s[slot], l_s[slot] = softmax2(s_s[slot])

    def output(r, slot):
        vw = v_ref[pl.ds(pl.multiple_of(win_start(r) * GRID_W, GRID_W), kwin), :]
        acc = jnp.dot(p_s[slot], jnp.concatenate([vw, vc_ref[...]], axis=0), preferred_element_type=F32)
        o_ref[pl.ds(pl.multiple_of(r * GRID_W, GRID_W), GRID_W), :] = pick_heads(acc * l_s[slot]).astype(o_ref.dtype)

    def ctx_scores():
        qqc = stack_heads(qn_s[pl.ds(s_len, c_len), :])
        sc_s[...] = lax.dot_general(qqc, kc(), nt, preferred_element_type=F32)

    def ctx_probs():
        pc_s[...], lc_s[...] = softmax2(sc_s[...])

    def ctx_output():
        acc = jnp.dot(pc_s[...], vc_ref[...], preferred_element_type=F32)
        oc_ref[...] = pick_heads(acc * lc_s[...]).astype(oc_ref.dtype)

    def step(blk, par, do_out=True, do_probs=True, do_scores=True):
        for a in range(rps):
            if do_out:
                output((blk - 2) * rps + a, par * rps + a)
        for a in range(rps):
            if do_probs:
                probs((1 - par) * rps + a)
        for a in range(rps):
            if do_scores:
                scores(blk * rps + a, par * rps + a)
        if do_scores:
            normalise_block(jnp.minimum(blk + 2, n_blocks - 1))

    normalise(qc_ref, kc_ref, 0, s_len, c_len)
    normalise_block(0)
    normalise_block(1)
    step(0, 0, do_out=False, do_probs=False)
    ctx_scores()
    step(1, 1, do_out=False)
    ctx_probs()

    def loop_step(j, carry):
        step(2 * j, 0)
        step(2 * j + 1, 1)
        return carry

    lax.fori_loop(1, n_blocks // 2, loop_step, 0)
    step(n_blocks, 0, do_scores=False)
    ctx_output()
    step(n_blocks + 1, 1, do_probs=False, do_scores=False)


def _attention(q, k, v, qc, kc, vc, gq2, gk2, bias_tab):
    b, s_len, d = q.shape
    c_len = qc.shape[1]
    pw = gq2.shape[-1]
    n_rows = s_len // GRID_W
    n_keys = WIN_ROWS * GRID_W + c_len
    assert n_rows >= WIN_ROWS and n_rows % (2 * ATTN_ROWS_PER_STEP) == 0
    assert ATTN_ROWS_PER_STEP >= WIN_ROWS // 2
    lat =pl.BlockSpec((None, s_len, pw), lambda bi, pi: (bi, 0, pi))
    cx = pl.BlockSpec((None, c_len, pw), lambda bi, pi: (bi, 0, pi))
    gain = pl.BlockSpec((1, pw), lambda bi, pi: (0, 0))
    return pl.pallas_call(
        functools.partial(_attn_body, n_rows=n_rows),
        grid=(b, d // pw),
        in_specs=[lat, lat, lat, cx, cx, cx, gain, gain,
                  pl.BlockSpec((2, N_DR_TILES, GRID_W, 2 * GRID_W), lambda bi, pi: (pi, 0, 0, 0))],
        out_specs=[lat, cx],
        out_shape=[jax.ShapeDtypeStruct((b, s_len, d), BF16), jax.ShapeDtypeStruct((b, c_len, d), BF16)],
        scratch_shapes=[pltpu.VMEM((s_len + c_len, pw), BF16)] * 2 + [
            pltpu.VMEM((2 * ATTN_ROWS_PER_STEP, 2 * GRID_W, n_keys), F32),
            pltpu.VMEM((2 * ATTN_ROWS_PER_STEP, 2 * GRID_W, n_keys), BF16),
            pltpu.VMEM((2 * ATTN_ROWS_PER_STEP, 2 * GRID_W, 1), F32),
            pltpu.VMEM((2 * c_len, c_len), F32), pltpu.VMEM((2 * c_len, c_len), BF16),
            pltpu.VMEM((2 * c_len, 1), F32)],
        compiler_params=_cparams(("parallel", "parallel")),
    )(q, k, v, qc, kc, vc, gq2, gk2, bias_tab)


def _proj_mlp_body(x_ref, o_ref, mod_ref, g_ref, wo_ref, w1_ref, w2_ref, out_ref):
    y = jnp.dot(o_ref[...], wo_ref[...], preferred_element_type=F32)
    x = x_ref[...] + mod_ref[2:3, :] * y
    h = _norm_mod(x, g_ref[...], mod_ref[3:4, :], mod_ref[4:5, :]).astype(BF16)
    out_ref[...] = x + mod_ref[5:6, :] * _sq_relu_mlp(h, w1_ref, w2_ref)


def _proj_mlp(x, o, mods4, layer, ctx_row, g, wo, w1, w2, tm):
    b, t, d = x.shape
    d_ff = w1.shape[-1]
    mod_idx = (lambda bi, ti: (layer, bi, 0, 0)) if ctx_row is None else (lambda bi, ti: (layer, ctx_row, 0, 0))
    tok = pl.BlockSpec((None, tm, d), lambda bi, ti: (bi, ti, 0))
    const = lambda bi, ti: (0, 0)
    this_layer = lambda bi, ti: (layer, 0, 0)
    return pl.pallas_call(
        _proj_mlp_body,
        grid=(b, t // tm),
        in_specs=[tok, tok, pl.BlockSpec((None, None, 6, d), mod_idx), pl.BlockSpec((1, d), const),
                  _resident((d, d), const), _resident((None, d, d_ff), this_layer),
                  _resident((None, d_ff, d), this_layer)],
        out_specs=tok,
        out_shape=jax.ShapeDtypeStruct((b, t, d), F32),
        compiler_params=_cparams(("parallel", "parallel")),
    )(x, o, mods4, g, wo, w1, w2)


def _dft_tables(n, dg):
    l2 = FFT_L2
    l1 = n // l2
    kb = SUBLANES
    ar = np.arange
    a = 2.0 * np.pi * np.outer(ar(dg), ar(dg)) / dg
    norm = 1.0 / math.sqrt(n * dg)
    cd, sd = np.cos(a) * norm, np.sin(a) * norm
    al = 2.0 * np.pi * np.outer(ar(l1), ar(l1)) / l1
    c1, s1 = np.cos(al), np.sin(al)
    m1 = np.block([[c1, -s1], [-s1, -c1]])
    k = ar(l1)[:, None, None] + l1 * ar(l2)[None, :, None]
    be = 2.0 * np.pi * (k * ar(l2)[None, None, :] % n) / n
    m2 = np.concatenate([np.cos(be), np.sin(be)], axis=-1)
    m2 = m2.reshape(l1 // kb, kb, l2, 2, l2)
    big = np.zeros((l1 // kb, l2, kb, 2, l2, kb))
    for kk in range(kb):
        big[:, :, kk, :, :, kk] = m2[:, kk]
    m2 = big.reshape(l1 // kb, l2 * kb, 2 * l2 * kb)
    return (jnp.asarray(cd, F32), jnp.asarray(sd, F32), jnp.asarray(m1, F32), jnp.asarray(m2, F32))


def _put_cols(s_ref, first, val, rows=slice(None)):
    for c in range(val.shape[1] // LANES):
        s_ref[first + c, rows, :] = val[:, c * LANES:(c + 1) * LANES]


def _get_cols(s_ref, first, n_chunks, rows=slice(None)):
    return jnp.concatenate([s_ref[first + c, rows, :] for c in range(n_chunks)], axis=1)


def _fold_body(cd_ref, sd_ref, w_ref, o_ref):
    w = w_ref[...]
    hp = lax.Precision.HIGHEST
    o_ref[0] = jnp.dot(cd_ref[...], w, precision=hp, preferred_element_type=F32).astype(BF16)
    o_ref[1] = jnp.dot(sd_ref[...], w, precision=hp, preferred_element_type=F32).astype(BF16)


def _fold_fourier_weight(cd, sd, w):
    d = w.shape[0]
    dg = cd.shape[0]
    return pl.pallas_call(
        _fold_body,
        grid=(d // dg,),
        in_specs=[pl.BlockSpec((dg, dg), lambda g: (0, 0)), pl.BlockSpec((dg, dg), lambda g: (0, 0)),
                  pl.BlockSpec((dg, d), lambda g: (g, 0))],
        out_specs=pl.BlockSpec((2, dg, d), lambda g: (0, g, 0)),
        out_shape=jax.ShapeDtypeStruct((2, d, d), BF16),
        compiler_params=_cparams(("parallel",)),
    )(cd, sd, w)


def _dft1_body(x_ref, mod_ref, g_ref, w_ref, m_ref, z_ref, p_s):
    l1, nb, d = x_ref.shape
    x = x_ref[...].reshape(l1 * nb, d)
    h = _norm_mod(x, g_ref[...], mod_ref[0:1, :], mod_ref[1:2, :]).astype(BF16)
    nc = d // LANES
    _put_cols(p_s, 0, jnp.dot(h, w_ref[0], preferred_element_type=F32))
    _put_cols(p_s, nc, jnp.dot(h, w_ref[1], preferred_element_type=F32))
    m1 = m_ref[...].astype(BF16)
    for m in range(nb):
        rows = pl.ds(m, l1, stride=nb)
        pm = jnp.concatenate([_get_cols(p_s, 0, nc, rows), _get_cols(p_s, nc, nc, rows)], axis=0).astype(BF16)
        z_ref[m] = jnp.dot(m1, pm, preferred_element_type=F32)


def _fourier_stage1(x, mods4, layer, g, wf, m1):
    b, s_len, d = x.shape
    l1 = s_len // FFT_L2
    nb = SUBLANES
    return pl.pallas_call(
        _dft1_body,
        grid=(b, FFT_L2 // nb),
        in_specs=[
            pl.BlockSpec((None, l1, nb, d), lambda bi, ti: (bi, 0, ti, 0)),
            pl.BlockSpec((None, None, 6, d), lambda bi, ti: (layer, bi, 0, 0)),
            pl.BlockSpec((1, d), lambda bi, ti: (0, 0)),
            _resident((2, d, d), lambda bi, ti: (0, 0, 0)),
            pl.BlockSpec((2 * l1, 2 * l1), lambda bi, ti: (0, 0)),
        ],
        out_specs=pl.BlockSpec((None, nb, 2 * l1, d), lambda bi, ti: (bi, ti, 0, 0)),
        out_shape=jax.ShapeDtypeStruct((b, FFT_L2, 2 * l1, d), F32),
        scratch_shapes=[pltpu.VMEM((2 * d // LANES, l1 * nb, LANES), F32)],
        compiler_params=_cparams(("parallel", "parallel")),
    )(x.reshape(b, l1, FFT_L2, d), mods4, g, wf, m1)


def _dft2_mlp_body(zr_ref, zi_ref, x_ref, m_ref, mod_ref, g_ref, w1_ref, w2_ref, out_ref):
    l2, kb, d = x_ref.shape
    n = l2 * kb
    zz = jnp.concatenate([zr_ref[...].reshape(n, d), zi_ref[...].reshape(n, d)], axis=0).astype(BF16)
    y = jnp.dot(m_ref[...].astype(BF16), zz, preferred_element_type=F32)
    x = x_ref[...].reshape(n, d) + mod_ref[2:3, :] * y
    h = _norm_mod(x, g_ref[...], mod_ref[3:4, :], mod_ref[4:5, :]).astype(BF16)
    out = x + mod_ref[5:6, :] * _sq_relu_mlp(h, w1_ref, w2_ref)
    out_ref[...] = out.reshape(l2, kb, d)


def _fourier_stage2_mlp(z, x, mods4, layer, g, m2, w1, w2):
    b, s_len, d = x.shape
    l1 = s_len // FFT_L2
    kb = SUBLANES
    d_ff = w1.shape[-1]
    zblk = lambda part: pl.BlockSpec((None, FFT_L2, None, kb, d), lambda bi, ki: (bi, 0, part, ki, 0))
    xblk = pl.BlockSpec((None, FFT_L2, kb, d), lambda bi, ki: (bi, 0, ki, 0))
    const = lambda bi, ki: (0, 0)
    this_layer = lambda bi, ki: (layer, 0, 0)
    z5 = z.reshape(b, FFT_L2, 2, l1, d)
    out = pl.pallas_call(
        _dft2_mlp_body,
        grid=(b, l1 // kb),
        in_specs=[
            zblk(0), zblk(1), xblk,
            pl.BlockSpec((None, FFT_L2 * kb, 2 * FFT_L2 * kb), lambda bi, ki: (ki, 0, 0)),
            pl.BlockSpec((None, None, 6, d), lambda bi, ki: (layer, bi, 0, 0)),
            pl.BlockSpec((1, d), const),
            _resident((None, d, d_ff), this_layer), _resident((None, d_ff, d), this_layer),
        ],
        out_specs=xblk,
        out_shape=jax.ShapeDtypeStruct((b, FFT_L2, l1, d), F32),
        compiler_params=_cparams(("parallel", "parallel")),
    )(z5, z5, x.reshape(b, FFT_L2, l1, d), m2, mods4, g, w1, w2)
    return out.reshape(b, s_len, d)


def kernel(x, c, ctx, c_ctx, ada_w, ada_b, norm_g, attn_wqkv, attn_wo, q_norm_g, k_norm_g, rpb,
           fourier_w, mlp_w1, mlp_w2):
    b, s_len, d = x.shape
    depth = ada_w.shape[0]
    hd = q_norm_g.shape[-1]
    assert depth == 2 and b + 1 <= 8 and d % (2 * hd) == 0
    assert s_len % (FFT_L2 * SUBLANES) == 0 and s_len % TOKEN_TILE == 0
    ctx_row = b

    cc = jnp.zeros((8, d), F32).at[:b].set(c).at[b].set(c_ctx)
    mods4 = _mods(cc, ada_w, ada_b).reshape(depth, 8, 6, d)

    for layer in range(depth):
        is_attn = layer % N_MIXERS == 0
        idx = layer // N_MIXERS
        need_ctx = layer < depth - 1
        g_pre = norm_g[layer, 0][None]
        g_post = norm_g[layer, 1][None]
        if is_attn:
            assert layer == 0, "the first layer's QKV kernel also casts every layer's MLP weights"
            gq2 = jnp.tile(q_norm_g[idx], 2)[None]
            gk2 = jnp.tile(k_norm_g[idx], 2)[None]
            q, k, v, wqkv, wo, w1, w2 = _qkv_and_weight_casts(
                x, mods4, layer, g_pre, attn_wqkv[idx], attn_wo[idx], mlp_w1, mlp_w2, TOKEN_TILE)
            qc, kc, vc = _qkv(ctx, mods4, layer, ctx_row, g_pre, wqkv, ctx.shape[1])
            o, oc = _attention(q, k, v, qc, kc, vc, gq2, gk2, _bias_table(rpb[idx]))
            x = _proj_mlp(x, o, mods4, layer, None, g_post, wo, w1, w2, TOKEN_TILE)
            if need_ctx:
                ctx = _proj_mlp(ctx, oc, mods4, layer, ctx_row, g_post, wo, w1, w2, ctx.shape[1])
        else:
            assert not need_ctx, "a Fourier layer that still feeds a context stream is not supported"
            cd, sd, m1, m2 = _dft_tables(s_len, d // FOURIER_GROUPS)
            wf = _fold_fourier_weight(cd, sd, fourier_w[idx])
            z = _fourier_stage1(x, mods4, layer, g_pre, wf, m1)
            x = _fourier_stage2_mlp(z, x, mods4, layer, g_post, m2, w1, w2)
    return x
```

```python
import functools
import math

import numpy as np
import jax
import jax.numpy as jnp
from jax import lax
from jax.experimental import pallas as pl
from jax.experimental.pallas import tpu as pltpu

F32 = jnp.float32
BF16 = jnp.bfloat16

GRID_W = 64
WIN_ROWS = 8
WIN_COLS = 16
FOURIER_GROUPS = 4
N_MIXERS = 2
EPS = 1e-6
NEG_INF = float("-inf")
LOG2E = math.log2(math.e)

N_DR_TILES = 2 * WIN_ROWS - 2
ATTN_ROWS_PER_STEP = 4
ATTN_PAIRS_PER_STEP = 2
FFT_L2 = 64
SUBLANES = 8
LANES = 128
TOKEN_TILE = 512
FF_CHUNK = 1024
VMEM_LIMIT = 56 * 1024 * 1024


def _cparams(sem):
    return pltpu.CompilerParams(dimension_semantics=sem, vmem_limit_bytes=VMEM_LIMIT)


def _resident(shape, index_map):
    return pl.BlockSpec(shape, index_map, pipeline_mode=pl.Buffered(1))


def _mods_body(c_ref, w_ref, b_ref, o_ref):
    c = c_ref[...]
    s = (c * jax.nn.sigmoid(c)).astype(BF16)
    o_ref[...] = jnp.dot(s, w_ref[...].astype(BF16), preferred_element_type=F32) + b_ref[...]


def _mods(cc, ada_w, ada_b):
    n_layers, d, d6 = ada_w.shape
    tn = 1536
    return pl.pallas_call(
        _mods_body,
        grid=(n_layers, d6 // tn),
        in_specs=[
            pl.BlockSpec((8, d), lambda l, n: (0, 0)),
            pl.BlockSpec((None, d, tn), lambda l, n: (l, 0, n)),
            pl.BlockSpec((None, 1, tn), lambda l, n: (l, 0, n)),
        ],
        out_specs=pl.BlockSpec((None, 8, tn), lambda l, n: (l, 0, n)),
        out_shape=jax.ShapeDtypeStruct((n_layers, 8, d6), F32),
        compiler_params=_cparams(("parallel", "parallel")),
    )(cc, ada_w, ada_b.reshape(n_layers, 1, d6))


def _norm_mod(x, g, shift, scale):
    ms = jnp.mean(x * x, axis=-1, keepdims=True)
    return (x * lax.rsqrt(ms + EPS) * g) * (1.0 + scale) + shift


def _sq_relu_mlp(h, w1_ref, w2_ref):
    acc = None
    for c in range(0, w1_ref.shape[1], FF_CHUNK):
        a = jnp.maximum(jnp.dot(h, w1_ref[:, c:c + FF_CHUNK], preferred_element_type=F32), 0.0)
        part = jnp.dot((a * a).astype(BF16), w2_ref[c:c + FF_CHUNK, :], preferred_element_type=F32)
        acc = part if acc is None else acc + part
    return acc


def _qkv_body(x_ref, mod_ref, g_ref, w_ref, q_ref, k_ref, v_ref):
    d = x_ref.shape[-1]
    h = _norm_mod(x_ref[...], g_ref[...], mod_ref[0:1, :], mod_ref[1:2, :]).astype(BF16)
    for i, out_ref in enumerate((q_ref, k_ref, v_ref)):
        y = jnp.dot(h, w_ref[:, i * d:(i + 1) * d], preferred_element_type=F32).astype(BF16)
        for p in range(out_ref.shape[0]):
            out_ref[p] = y[:, p * LANES:(p + 1) * LANES]


def _pair_major(b, t, d, tm):
    spec = pl.BlockSpec((None, d // LANES, tm, LANES), lambda bi, ti: (bi, 0, ti, 0))
    return spec, jax.ShapeDtypeStruct((b, d // LANES, t, LANES), BF16)


def _qkv(x, mods4, layer, ctx_row, g, w_bf16, tm):
    b, t, d = x.shape
    mod_idx = (lambda bi, ti: (layer, bi, 0, 0)) if ctx_row is None else (lambda bi, ti: (layer, ctx_row, 0, 0))
    tok = pl.BlockSpec((None, tm, d), lambda bi, ti: (bi, ti, 0))
    out_spec, out = _pair_major(b, t, d, tm)
    return pl.pallas_call(
        _qkv_body,
        grid=(b, t // tm),
        in_specs=[
            tok,
            pl.BlockSpec((None, None, 6, d), mod_idx),
            pl.BlockSpec((1, d), lambda bi, ti: (0, 0)),
            _resident((d, 3 * d), lambda bi, ti: (0, 0)),
        ],
        out_specs=[out_spec, out_spec, out_spec],
        out_shape=[out, out, out],
        compiler_params=_cparams(("parallel", "parallel")),
    )(x, mods4, g, w_bf16)


def _qkv_cast_body(x_ref, mod_ref, g_ref, wqkv_ref, wo_ref, w1_ref, w2_ref,
                   q_ref, k_ref, v_ref, wqkv_o, wo_o, w1_o, w2_o):
    @pl.when((pl.program_id(0) == 0) & (pl.program_id(1) == 0))
    def _():
        wqkv_o[...] = wqkv_ref[...].astype(BF16)

    wo_o[...] = wo_ref[...].astype(BF16)
    w1_o[...] = w1_ref[...].astype(BF16)
    w2_o[...] = w2_ref[...].astype(BF16)
    _qkv_body(x_ref, mod_ref, g_ref, wqkv_o, q_ref, k_ref, v_ref)


def _qkv_and_weight_casts(x, mods4, layer, g, wqkv, wo, mlp_w1, mlp_w2, tm):
    b, t, d = x.shape
    nt = t // tm
    n_steps = b * nt
    n_layers, _, d_ff = mlp_w1.shape
    assert d % (2 * SUBLANES * n_steps) == 0
    step = lambda bi, ti: bi * nt + ti
    tok = pl.BlockSpec((None, tm, d), lambda bi, ti: (bi, ti, 0))
    qkv_spec, qkv_out = _pair_major(b, t, d, tm)
    wo_blk = pl.BlockSpec((d // n_steps, d), lambda bi, ti: (step(bi, ti), 0))
    w1_blk = pl.BlockSpec((n_layers, d // n_steps, d_ff), lambda bi, ti: (0, step(bi, ti), 0))
    w2_blk = pl.BlockSpec((n_layers, d_ff // n_steps, d), lambda bi, ti: (0, step(bi, ti), 0))
    return pl.pallas_call(
        _qkv_cast_body,
        grid=(b, nt),
        in_specs=[
            tok,
            pl.BlockSpec((None, None, 6, d), lambda bi, ti: (layer, bi, 0, 0)),
            pl.BlockSpec((1, d), lambda bi, ti: (0, 0)),
            _resident((d, 3 * d), lambda bi, ti: (0, 0)),
            wo_blk, w1_blk, w2_blk,
        ],
        out_specs=[qkv_spec, qkv_spec, qkv_spec,
                   pl.BlockSpec((d, 3 * d), lambda bi, ti: (0, 0)), wo_blk, w1_blk, w2_blk],
        out_shape=[qkv_out, qkv_out, qkv_out,
                   jax.ShapeDtypeStruct((d, 3 * d), BF16), jax.ShapeDtypeStruct((d, d), BF16),
                   jax.ShapeDtypeStruct(mlp_w1.shape, BF16), jax.ShapeDtypeStruct(mlp_w2.shape, BF16)],
        compiler_params=_cparams(("arbitrary", "arbitrary")),
    )(x, mods4, g, wqkv, wo, mlp_w1, mlp_w2)


def _bias_body(r_ref, o_ref, *, tn):
    col0 = pl.program_id(0) * tn
    e = col0 + lax.broadcasted_iota(jnp.int32, (64, tn), 1)
    r = lax.broadcasted_iota(jnp.int32, (64, tn), 0)
    cq = e // (2 * GRID_W)
    half = (e // GRID_W) % 2
    ck = e % GRID_W
    sel = ((r // 32 == half) & (r % 32 == ck - cq + WIN_COLS - 1)).astype(BF16)
    rv = r_ref[...]
    hi = rv.astype(BF16)
    rem = rv - hi.astype(F32)
    mid = rem.astype(BF16)
    lo = (rem - mid.astype(F32)).astype(BF16)
    acc = jnp.dot(hi, sel, preferred_element_type=F32)
    acc += jnp.dot(mid, sel, preferred_element_type=F32)
    acc += jnp.dot(lo, sel, preferred_element_type=F32)
    e1 = col0 + lax.broadcasted_iota(jnp.int32, (1, tn), 1)
    cq1 = e1 // (2 * GRID_W)
    ck1 = e1 % GRID_W
    start = jnp.clip(cq1 - WIN_COLS // 2, 0, GRID_W - WIN_COLS)
    ok = (ck1 >= start) & (ck1 < start + WIN_COLS)
    vals = acc * LOG2E + jnp.where(ok, 0.0, NEG_INF)
    n_tables = r_ref.shape[0]
    w = 2 * GRID_W
    for c in range(tn // w):
        cq_c = pl.program_id(0) * (tn // w) + c
        o_ref[pl.ds(cq_c, n_tables, stride=GRID_W), :] = vals[:, c * w:(c + 1) * w]


def _bias_table(rpb):
    h, nr, nc = rpb.shape
    rp = jnp.pad(rpb, ((0, 0), (0, 0), (0, 32 - nc)))
    r2 = jnp.concatenate([rp[:, 0:N_DR_TILES], rp[:, 1:N_DR_TILES + 1]], axis=-1)
    r2 = r2.reshape(h * N_DR_TILES, 64)
    n_cols = GRID_W * 2 * GRID_W
    tn = 2048
    out = pl.pallas_call(
        functools.partial(_bias_body, tn=tn),
        grid=(n_cols // tn,),
        in_specs=[pl.BlockSpec((h * N_DR_TILES, 64), lambda n: (0, 0))],
        out_specs=pl.BlockSpec((h * N_DR_TILES * GRID_W, 2 * GRID_W), lambda n: (0, 0)),
        out_shape=jax.ShapeDtypeStruct((h * N_DR_TILES * GRID_W, 2 * GRID_W), F32),
        compiler_params=_cparams(("arbitrary",)),
    )(r2)
    return out.reshape(h, N_DR_TILES, GRID_W, 2 * GRID_W)


def _attn_body(q_ref, k_ref, v_ref, qc_ref, kc_ref, vc_ref, gq_ref, gk_ref, bias_ref, o_ref, oc_ref,
               qn_s, kn_s, s_s, p_s, l_s, sc_s, pc_s, lc_s, *, n_rows):
    n_pairs, s_len, pw = q_ref.shape
    c_len = qc_ref.shape[1]
    hd = pw // 2
    kwin = WIN_ROWS * GRID_W
    nt = (((1,), (1,)), ((), ()))
    rps = ATTN_ROWS_PER_STEP
    n_blocks = n_rows // rps
    blk_tokens = rps * GRID_W

    lane = lax.broadcasted_iota(jnp.int32, (1, 2 * hd), 1)
    first = lane < hd
    ri = lax.broadcasted_iota(jnp.int32, (4 * hd, 4 * hd), 0)
    ci = lax.broadcasted_iota(jnp.int32, (4 * hd, 4 * hd), 1)
    mean_blk = jnp.where(ri // hd == ci // hd, 1.0 / hd, 0.0).astype(BF16)
    gain = gq_ref[...] * gk_ref[...] * (hd ** -0.5 * LOG2E)

    def normalise(src_q, src_k, pair, src_off, dst_off, n):
        q = src_q[pair, pl.ds(src_off, n), :].astype(F32)
        k = src_k[pair, pl.ds(src_off, n), :].astype(F32)
        ms = jnp.dot(jnp.concatenate([q * q, k * k], axis=1).astype(BF16), mean_blk, preferred_element_type=F32)
        qn_s[pl.ds(dst_off, n), :] = (q * lax.rsqrt(ms[:, 0:2 * hd] + EPS) * gain).astype(BF16)
        kn_s[pl.ds(dst_off, n), :] = (k * lax.rsqrt(ms[:, 2 * hd:4 * hd] + EPS)).astype(BF16)

    def normalise_block(pair, blk):
        off = pl.multiple_of(blk * blk_tokens, blk_tokens)
        normalise(q_ref, k_ref, pair, off, off, blk_tokens)

    def kc():
        return kn_s[pl.ds(s_len, c_len), :]

    def stack_heads(q):
        zero = jnp.zeros_like(q)
        return jnp.concatenate([jnp.where(first, q, zero), jnp.where(first, zero, q)], axis=0)

    def pick_heads(acc):
        n = acc.shape[0] // 2
        return jnp.where(first, acc[:n], acc[n:])

    def win_start(r):
        return jnp.clip(r - WIN_ROWS // 2, 0, n_rows - WIN_ROWS)

    def scores(pair, r, slot):
        rs = win_start(r)
        qq = stack_heads(qn_s[pl.ds(pl.multiple_of(r * GRID_W, GRID_W), GRID_W), :])
        kw = kn_s[pl.ds(pl.multiple_of(rs * GRID_W, GRID_W), kwin), :]
        t0 = rs - r + WIN_ROWS - 1
        bias = jnp.concatenate(
            [jnp.concatenate([bias_ref[2 * pair + hh, t0 + 2 * ii] for ii in range(WIN_ROWS // 2)], axis=1)
             for hh in range(2)], axis=0)
        s_s[slot, :, 0:kwin] = lax.dot_general(qq, kw, nt, preferred_element_type=F32) + bias
        s_s[slot, :, kwin:kwin + c_len] = lax.dot_general(qq, kc(), nt, preferred_element_type=F32)

    def softmax2(s):
        p = jnp.exp2(s - s.max(axis=-1, keepdims=True))
        return p.astype(BF16), 1.0 / p.sum(axis=-1, keepdims=True)

    def probs(slot):
        p_s[slot], l_s[slot] = softmax2(s_s[slot])

    def output(pair, r, slot):
        vw = v_ref[pair, pl.ds(pl.multiple_of(win_start(r) * GRID_W, GRID_W), kwin), :]
        acc = jnp.dot(p_s[slot], jnp.concatenate([vw, vc_ref[pair]], axis=0), preferred_element_type=F32)
        o_ref[pair, pl.ds(pl.multiple_of(r * GRID_W, GRID_W), GRID_W), :] = (
            pick_heads(acc * l_s[slot]).astype(o_ref.dtype))

    def ctx_scores():
        qqc = stack_heads(qn_s[pl.ds(s_len, c_len), :])
        sc_s[...] = lax.dot_general(qqc, kc(), nt, preferred_element_type=F32)

    def ctx_probs():
        pc_s[...], lc_s[...] = softmax2(sc_s[...])

    def ctx_output(pair):
        acc = jnp.dot(pc_s[...], vc_ref[pair], preferred_element_type=F32)
        oc_ref[pair] = pick_heads(acc * lc_s[...]).astype(oc_ref.dtype)

    def step(pair, blk, par, out_of=None, probs_of=True, do_scores=True):
        if out_of is None:
            out_of = (pair, blk - 2)
        if out_of is not False:
            for a in range(rps):
                output(out_of[0], out_of[1] * rps + a, par * rps + a)
        if probs_of:
            for a in range(rps):
                probs((1 - par) * rps + a)
        if do_scores:
            for a in range(rps):
                scores(pair, blk * rps + a, par * rps + a)
            normalise_block(pair, jnp.minimum(blk + 2, n_blocks - 1))

    for pair in range(n_pairs):
        normalise(qc_ref, kc_ref, pair, 0, s_len, c_len)
        normalise_block(pair, 0)
        normalise_block(pair, 1)
        if pair == 0:
            step(pair, 0, 0, out_of=False, probs_of=False)
            ctx_scores()
            step(pair, 1, 1, out_of=False)
        else:
            step(pair, 0, 0, out_of=(pair - 1, n_blocks - 2))
            ctx_output(pair - 1)
            ctx_scores()
            step(pair, 1, 1, out_of=(pair - 1, n_blocks - 1))
        ctx_probs()

        def loop_step(j, carry, pair=pair):
            step(pair, 2 * j, 0)
            step(pair, 2 * j + 1, 1)
            return carry

        lax.fori_loop(1, n_blocks // 2, loop_step, 0)

    last = n_pairs - 1
    step(last, n_blocks, 0, do_scores=False)
    ctx_output(last)
    step(last, n_blocks + 1, 1, probs_of=False, do_scores=False)


def _attention(q, k, v, qc, kc, vc, gq2, gk2, bias_tab):
    b, n_pairs, s_len, pw = q.shape
    c_len = qc.shape[2]
    pps = ATTN_PAIRS_PER_STEP
    n_rows = s_len // GRID_W
    n_keys = WIN_ROWS * GRID_W + c_len
    assert n_rows >= WIN_ROWS and n_rows % (2 * ATTN_ROWS_PER_STEP) == 0 and n_pairs % pps == 0
    assert ATTN_ROWS_PER_STEP >= WIN_ROWS // 2
    lat = pl.BlockSpec((None, pps, s_len, pw), lambda bi, pi: (bi, pi, 0, 0))
    cx = pl.BlockSpec((None, pps, c_len, pw), lambda bi, pi: (bi, pi, 0, 0))
    gain = pl.BlockSpec((1, pw), lambda bi, pi: (0, 0))
    return pl.pallas_call(
        functools.partial(_attn_body, n_rows=n_rows),
        grid=(b, n_pairs // pps),
        in_specs=[lat, lat, lat, cx, cx, cx, gain, gain,
                  pl.BlockSpec((2 * pps, N_DR_TILES, GRID_W, 2 * GRID_W), lambda bi, pi: (pi, 0, 0, 0))],
        out_specs=[lat, cx],
        out_shape=[jax.ShapeDtypeStruct(q.shape, BF16), jax.ShapeDtypeStruct(qc.shape, BF16)],
        scratch_shapes=[pltpu.VMEM((s_len + c_len, pw), BF16)] * 2 + [
            pltpu.VMEM((2 * ATTN_ROWS_PER_STEP, 2 * GRID_W, n_keys), F32),
            pltpu.VMEM((2 * ATTN_ROWS_PER_STEP, 2 * GRID_W, n_keys), BF16),
            pltpu.VMEM((2 * ATTN_ROWS_PER_STEP, 2 * GRID_W, 1), F32),
            pltpu.VMEM((2 * c_len, c_len), F32), pltpu.VMEM((2 * c_len, c_len), BF16),
            pltpu.VMEM((2 * c_len, 1), F32)],
        compiler_params=_cparams(("parallel", "parallel")),
    )(q, k, v, qc, kc, vc, gq2, gk2, bias_tab)


def _proj_mlp_body(x_ref, o_ref, mod_ref, g_ref, wo_ref, w1_ref, w2_ref, out_ref):
    o = jnp.concatenate([o_ref[p] for p in range(o_ref.shape[0])], axis=1)
    y = jnp.dot(o, wo_ref[...], preferred_element_type=F32)
    x = x_ref[...] + mod_ref[2:3, :] * y
    h = _norm_mod(x, g_ref[...], mod_ref[3:4, :], mod_ref[4:5, :]).astype(BF16)
    out_ref[...] = x + mod_ref[5:6, :] * _sq_relu_mlp(h, w1_ref, w2_ref)


def _proj_mlp(x, o, mods4, layer, ctx_row, g, wo, w1, w2, tm):
    b, t, d = x.shape
    d_ff = w1.shape[-1]
    mod_idx = (lambda bi, ti: (layer, bi, 0, 0)) if ctx_row is None else (lambda bi, ti: (layer, ctx_row, 0, 0))
    tok = pl.BlockSpec((None, tm, d), lambda bi, ti: (bi, ti, 0))
    const = lambda bi, ti: (0, 0)
    this_layer = lambda bi, ti: (layer, 0, 0)
    o_spec, _ = _pair_major(b, t, d, tm)
    return pl.pallas_call(
        _proj_mlp_body,
        grid=(b, t // tm),
        in_specs=[tok, o_spec, pl.BlockSpec((None, None, 6, d), mod_idx), pl.BlockSpec((1, d), const),
                  _resident((d, d), const), _resident((None, d, d_ff), this_layer),
                  _resident((None, d_ff, d), this_layer)],
        out_specs=tok,
        out_shape=jax.ShapeDtypeStruct((b, t, d), F32),
        compiler_params=_cparams(("parallel", "parallel")),
    )(x, o, mods4, g, wo, w1, w2)


def _dft_tables(n, dg):
    l2 = FFT_L2
    l1 = n // l2
    kb = SUBLANES
    ar = np.arange
    a = 2.0 * np.pi * np.outer(ar(dg), ar(dg)) / dg
    norm = 1.0 / math.sqrt(n * dg)
    cd, sd = np.cos(a) * norm, np.sin(a) * norm
    al = 2.0 * np.pi * np.outer(ar(l1), ar(l1)) / l1
    c1, s1 = np.cos(al), np.sin(al)
    m1 = np.block([[c1, -s1], [-s1, -c1]])
    k = ar(l1)[:, None, None] + l1 * ar(l2)[None, :, None]
    be = 2.0 * np.pi * (k * ar(l2)[None, None, :] % n) / n
    m2 = np.concatenate([np.cos(be), np.sin(be)], axis=-1)
    m2 = m2.reshape(l1 // kb, kb, l2, 2, l2)
    big = np.zeros((l1 // kb, l2, kb, 2, l2, kb))
    for kk in range(kb):
        big[:, :, kk, :, :, kk] = m2[:, kk]
    m2 = big.reshape(l1 // kb, l2 * kb, 2 * l2 * kb)
    return (jnp.asarray(cd, F32), jnp.asarray(sd, F32), jnp.asarray(m1, F32), jnp.asarray(m2, F32))


def _put_cols(s_ref, first, val, rows=slice(None)):
    for c in range(val.shape[1] // LANES):
        s_ref[first + c, rows, :] = val[:, c * LANES:(c + 1) * LANES]


def _get_cols(s_ref, first, n_chunks, rows=slice(None)):
    return jnp.concatenate([s_ref[first + c, rows, :] for c in range(n_chunks)], axis=1)


def _fold_body(cd_ref, sd_ref, w_ref, o_ref):
    w = w_ref[...]
    hp = lax.Precision.HIGHEST
    o_ref[0] = jnp.dot(cd_ref[...], w, precision=hp, preferred_element_type=F32).astype(BF16)
    o_ref[1] = jnp.dot(sd_ref[...], w, precision=hp, preferred_element_type=F32).astype(BF16)


def _fold_fourier_weight(cd, sd, w):
    d = w.shape[0]
    dg = cd.shape[0]
    return pl.pallas_call(
        _fold_body,
        grid=(d // dg,),
        in_specs=[pl.BlockSpec((dg, dg), lambda g: (0, 0)), pl.BlockSpec((dg, dg), lambda g: (0, 0)),
                  pl.BlockSpec((dg, d), lambda g: (g, 0))],
        out_specs=pl.BlockSpec((2, dg, d), lambda g: (0, g, 0)),
        out_shape=jax.ShapeDtypeStruct((2, d, d), BF16),
        compiler_params=_cparams(("parallel",)),
    )(cd, sd, w)


def _dft1_body(x_ref, mod_ref, g_ref, w_ref, m_ref, z_ref, p_s):
    l1, nb, d = x_ref.shape
    x = x_ref[...].reshape(l1 * nb, d)
    h = _norm_mod(x, g_ref[...], mod_ref[0:1, :], mod_ref[1:2, :]).astype(BF16)
    nc = d // LANES
    _put_cols(p_s, 0, jnp.dot(h, w_ref[0], preferred_element_type=F32))
    _put_cols(p_s, nc, jnp.dot(h, w_ref[1], preferred_element_type=F32))
    m1 = m_ref[...].astype(BF16)
    for m in range(nb):
        rows = pl.ds(m, l1, stride=nb)
        pm = jnp.concatenate([_get_cols(p_s, 0, nc, rows), _get_cols(p_s, nc, nc, rows)], axis=0).astype(BF16)
        z_ref[m] = jnp.dot(m1, pm, preferred_element_type=F32)


def _fourier_stage1(x, mods4, layer, g, wf, m1):
    b, s_len, d = x.shape
    l1 = s_len // FFT_L2
    nb = SUBLANES
    return pl.pallas_call(
        _dft1_body,
        grid=(b, FFT_L2 // nb),
        in_specs=[
            pl.BlockSpec((None, l1, nb, d), lambda bi, ti: (bi, 0, ti, 0)),
            pl.BlockSpec((None, None, 6, d), lambda bi, ti: (layer, bi, 0, 0)),
            pl.BlockSpec((1, d), lambda bi, ti: (0, 0)),
            _resident((2, d, d), lambda bi, ti: (0, 0, 0)),
            pl.BlockSpec((2 * l1, 2 * l1), lambda bi, ti: (0, 0)),
        ],
        out_specs=pl.BlockSpec((None, nb, 2 * l1, d), lambda bi, ti: (bi, ti, 0, 0)),
        out_shape=jax.ShapeDtypeStruct((b, FFT_L2, 2 * l1, d), F32),
        scratch_shapes=[pltpu.VMEM((2 * d // LANES, l1 * nb, LANES), F32)],
        compiler_params=_cparams(("parallel", "parallel")),
    )(x.reshape(b, l1, FFT_L2, d), mods4, g, wf, m1)


def _dft2_mlp_body(zr_ref, zi_ref, x_ref, m_ref, mod_ref, g_ref, w1_ref, w2_ref, out_ref):
    l2, kb, d = x_ref.shape
    n = l2 * kb
    zz = jnp.concatenate([zr_ref[...].reshape(n, d), zi_ref[...].reshape(n, d)], axis=0).astype(BF16)
    y = jnp.dot(m_ref[...].astype(BF16), zz, preferred_element_type=F32)
    x = x_ref[...].reshape(n, d) + mod_ref[2:3, :] * y
    h = _norm_mod(x, g_ref[...], mod_ref[3:4, :], mod_ref[4:5, :]).astype(BF16)
    out = x + mod_ref[5:6, :] * _sq_relu_mlp(h, w1_ref, w2_ref)
    out_ref[...] = out.reshape(l2, kb, d)


def _fourier_stage2_mlp(z, x, mods4, layer, g, m2, w1, w2):
    b, s_len, d = x.shape
    l1 = s_len // FFT_L2
    kb = SUBLANES
    d_ff = w1.shape[-1]
    zblk = lambda part: pl.BlockSpec((None, FFT_L2, None, kb, d), lambda bi, ki: (bi, 0, part, ki, 0))
    xblk = pl.BlockSpec((None, FFT_L2, kb, d), lambda bi, ki: (bi, 0, ki, 0))
    const = lambda bi, ki: (0, 0)
    this_layer = lambda bi, ki: (layer, 0, 0)
    z5 = z.reshape(b, FFT_L2, 2, l1, d)
    out = pl.pallas_call(
        _dft2_mlp_body,
        grid=(b, l1 // kb),
        in_specs=[
            zblk(0), zblk(1), xblk,
            pl.BlockSpec((None, FFT_L2 * kb, 2 * FFT_L2 * kb), lambda bi, ki: (ki, 0, 0)),
            pl.BlockSpec((None, None, 6, d), lambda bi, ki: (layer, bi, 0, 0)),
            pl.BlockSpec((1, d), const),
            _resident((None, d, d_ff), this_layer), _resident((None, d_ff, d), this_layer),
        ],
        out_specs=xblk,
        out_shape=jax.ShapeDtypeStruct((b, FFT_L2, l1, d), F32),
        compiler_params=_cparams(("parallel", "parallel")),
    )(z5, z5, x.reshape(b, FFT_L2, l1, d), m2, mods4, g, w1, w2)
    return out.reshape(b, s_len, d)


def kernel(x, c, ctx, c_ctx, ada_w, ada_b, norm_g, attn_wqkv, attn_wo, q_norm_g, k_norm_g, rpb,
           fourier_w, mlp_w1, mlp_w2):
    b, s_len, d = x.shape
    depth = ada_w.shape[0]
    hd = q_norm_g.shape[-1]
    assert depth == 2 and b + 1 <= 8 and 2 * hd == LANES and d % LANES == 0
    assert s_len % (FFT_L2 * SUBLANES) == 0 and s_len % TOKEN_TILE == 0
    ctx_row = b

    cc = jnp.zeros((8, d), F32).at[:b].set(c).at[b].set(c_ctx)
    mods4 = _mods(cc, ada_w, ada_b).reshape(depth, 8, 6, d)

    for layer in range(depth):
        is_attn = layer % N_MIXERS == 0
        idx = layer // N_MIXERS
        need_ctx = layer < depth - 1
        g_pre = norm_g[layer, 0][None]
        g_post = norm_g[layer, 1][None]
        if is_attn:
            assert layer == 0, "the first layer's QKV kernel also casts every layer's MLP weights"
            gq2 = jnp.tile(q_norm_g[idx], 2)[None]
            gk2 = jnp.tile(k_norm_g[idx], 2)[None]
            q, k, v, wqkv, wo, w1, w2 = _qkv_and_weight_casts(
                x, mods4, layer, g_pre, attn_wqkv[idx], attn_wo[idx], mlp_w1, mlp_w2, TOKEN_TILE)
            qc, kc, vc = _qkv(ctx, mods4, layer, ctx_row, g_pre, wqkv, ctx.shape[1])
            o, oc = _attention(q, k, v, qc, kc, vc, gq2, gk2, _bias_table(rpb[idx]))
            x = _proj_mlp(x, o, mods4, layer, None, g_post, wo, w1, w2, TOKEN_TILE)
            if need_ctx:
                ctx = _proj_mlp(ctx, oc, mods4, layer, ctx_row, g_post, wo, w1, w2, ctx.shape[1])
        else:
            assert not need_ctx, "a Fourier layer that still feeds a context stream is not supported"
            cd, sd, m1, m2 = _dft_tables(s_len, d // FOURIER_GROUPS)
            wf = _fold_fourier_weight(cd, sd, fourier_w[idx])
            z = _fourier_stage1(x, mods4, layer, g_pre, wf, m1)
            x = _fourier_stage2_mlp(z, x, mods4, layer, g_post, m2, w1, w2)
    return x
```

```python
import functools
import math

import numpy as np
import jax
import jax.numpy as jnp
from jax import lax
from jax.experimental import pallas as pl
from jax.experimental.pallas import tpu as pltpu

F32 = jnp.float32
BF16 = jnp.bfloat16

GRID_W = 64
WIN_ROWS = 8
WIN_COLS = 16
FOURIER_GROUPS = 4
N_MIXERS = 2
EPS = 1e-6
NEG_INF = float("-inf")
LOG2E = math.log2(math.e)

N_DR_TILES = 2 * WIN_ROWS - 2
ATTN_ROWS_PER_STEP = 4
ATTN_PAIRS_PER_STEP = 2
FFT_L2 = 64
SUBLANES = 8
LANES = 128
TOKEN_TILE = 512
FF_CHUNK = 1024
VMEM_LIMIT = 56 * 1024 * 1024


def _cparams(sem):
    return pltpu.CompilerParams(dimension_semantics=sem, vmem_limit_bytes=VMEM_LIMIT)


def _resident(shape, index_map):
    return pl.BlockSpec(shape, index_map, pipeline_mode=pl.Buffered(1))


def _mods_body(c_ref, w_ref, b_ref, o_ref):
    c = c_ref[...]
    s = (c * jax.nn.sigmoid(c)).astype(BF16)
    o_ref[...] = jnp.dot(s, w_ref[...].astype(BF16), preferred_element_type=F32) + b_ref[...]


def _prep_body(c_ref, w_ref, b_ref, r_ref, cd_ref, sd_ref, fw_ref, mods_ref, bias_ref, wf_ref, *,
               n_bias_chunks, bias_tn):
    i = pl.program_id(0)
    _mods_body(c_ref, w_ref, b_ref, mods_ref)

    @pl.when(i < n_bias_chunks)
    def _():
        _bias_chunk(r_ref, bias_ref, i, bias_tn)

    @pl.when(i >= n_bias_chunks)
    def _():
        _fold_body(cd_ref, sd_ref, fw_ref, wf_ref)


def _prepare_parameters(cc, ada_w, ada_b, rpb, cd, sd, fourier_w):
    n_layers, d, d6 = ada_w.shape
    h = rpb.shape[0]
    dg = cd.shape[0]
    tn = 1536
    nt = d6 // tn
    bias_tn = 2048
    n_bias_chunks = GRID_W * 2 * GRID_W // bias_tn
    n_groups = d // dg
    assert n_layers * nt == n_bias_chunks + n_groups
    group = lambda i: jnp.maximum(i - n_bias_chunks, 0)
    n_bias_rows = h * N_DR_TILES * GRID_W
    mods, bias, wf = pl.pallas_call(
        functools.partial(_prep_body, n_bias_chunks=n_bias_chunks, bias_tn=bias_tn),
        grid=(n_layers * nt,),
        in_specs=[
            pl.BlockSpec((8, d), lambda i: (0, 0)),
            pl.BlockSpec((None, d, tn), lambda i: (i // nt, 0, i % nt)),
            pl.BlockSpec((None, 1, tn), lambda i: (i // nt, 0, i % nt)),
            pl.BlockSpec((h * N_DR_TILES, 64), lambda i: (0, 0)),
            pl.BlockSpec((dg, dg), lambda i: (0, 0)), pl.BlockSpec((dg, dg), lambda i: (0, 0)),
            pl.BlockSpec((dg, d), lambda i: (group(i), 0)),
        ],
        out_specs=[
            pl.BlockSpec((None, 8, tn), lambda i: (i // nt, 0, i % nt)),
            pl.BlockSpec((n_bias_rows, 2 * GRID_W), lambda i: (0, 0)),
            pl.BlockSpec((2, dg, d), lambda i: (0, group(i), 0)),
        ],
        out_shape=[jax.ShapeDtypeStruct((n_layers, 8, d6), F32),
                   jax.ShapeDtypeStruct((n_bias_rows, 2 * GRID_W), F32),
                   jax.ShapeDtypeStruct((2, d, d), BF16)],
        compiler_params=_cparams(("arbitrary",)),
    )(cc, ada_w, ada_b.reshape(n_layers, 1, d6), _bias_operand(rpb), cd, sd, fourier_w)
    return mods, bias.reshape(h, N_DR_TILES, GRID_W, 2 * GRID_W), wf


def _norm_mod(x, g, shift, scale):
    ms = jnp.mean(x * x, axis=-1, keepdims=True)
    return (x * lax.rsqrt(ms + EPS) * g) * (1.0 + scale) + shift


def _sq_relu_mlp(h, w1_ref, w2_ref):
    acc = None
    for c in range(0, w1_ref.shape[1], FF_CHUNK):
        a = jnp.maximum(jnp.dot(h, w1_ref[:, c:c + FF_CHUNK], preferred_element_type=F32), 0.0)
        part = jnp.dot((a * a).astype(BF16), w2_ref[c:c + FF_CHUNK, :], preferred_element_type=F32)
        acc = part if acc is None else acc + part
    return acc


def _qkv_body(x_ref, mod_ref, g_ref, w_ref, q_ref, k_ref, v_ref):
    d = x_ref.shape[-1]
    h = _norm_mod(x_ref[...], g_ref[...], mod_ref[0:1, :], mod_ref[1:2, :]).astype(BF16)
    for i, out_ref in enumerate((q_ref, k_ref, v_ref)):
        y = jnp.dot(h, w_ref[:, i * d:(i + 1) * d], preferred_element_type=F32).astype(BF16)
        for p in range(out_ref.shape[0]):
            out_ref[p] = y[:, p * LANES:(p + 1) * LANES]


def _pair_major(b, t, d, tm):
    spec = pl.BlockSpec((None, d // LANES, tm, LANES), lambda bi, ti: (bi, 0, ti, 0))
    return spec, jax.ShapeDtypeStruct((b, d // LANES, t, LANES), BF16)


def _qkv(x, mods4, layer, ctx_row, g, w_bf16, tm):
    b, t, d = x.shape
    mod_idx = (lambda bi, ti: (layer, bi, 0, 0)) if ctx_row is None else (lambda bi, ti: (layer, ctx_row, 0, 0))
    tok = pl.BlockSpec((None, tm, d), lambda bi, ti: (bi, ti, 0))
    out_spec, out = _pair_major(b, t, d, tm)
    return pl.pallas_call(
        _qkv_body,
        grid=(b, t // tm),
        in_specs=[
            tok,
            pl.BlockSpec((None, None, 6, d), mod_idx),
            pl.BlockSpec((1, d), lambda bi, ti: (0, 0)),
            _resident((d, 3 * d), lambda bi, ti: (0, 0)),
        ],
        out_specs=[out_spec, out_spec, out_spec],
        out_shape=[out, out, out],
        compiler_params=_cparams(("parallel", "parallel")),
    )(x, mods4, g, w_bf16)


def _qkv_cast_body(x_ref, mod_ref, g_ref, wqkv_ref, wo_ref, w1_ref, w2_ref,
                   q_ref, k_ref, v_ref, wqkv_o, wo_o, w1_o, w2_o):
    @pl.when((pl.program_id(0) == 0) & (pl.program_id(1) == 0))
    def _():
        wqkv_o[...] = wqkv_ref[...].astype(BF16)

    wo_o[...] = wo_ref[...].astype(BF16)
    w1_o[...] = w1_ref[...].astype(BF16)
    w2_o[...] = w2_ref[...].astype(BF16)
    _qkv_body(x_ref, mod_ref, g_ref, wqkv_o, q_ref, k_ref, v_ref)


def _qkv_and_weight_casts(x, mods4, layer, g, wqkv, wo, mlp_w1, mlp_w2, tm):
    b, t, d = x.shape
    nt = t // tm
    n_steps = b * nt
    n_layers, _, d_ff = mlp_w1.shape
    assert d % (2 * SUBLANES * n_steps) == 0
    step = lambda bi, ti: bi * nt + ti
    tok = pl.BlockSpec((None, tm, d), lambda bi, ti: (bi, ti, 0))
    qkv_spec, qkv_out = _pair_major(b, t, d, tm)
    wo_blk = pl.BlockSpec((d // n_steps, d), lambda bi, ti: (step(bi, ti), 0))
    w1_blk = pl.BlockSpec((n_layers, d // n_steps, d_ff), lambda bi, ti: (0, step(bi, ti), 0))
    w2_blk = pl.BlockSpec((n_layers, d_ff // n_steps, d), lambda bi, ti: (0, step(bi, ti), 0))
    return pl.pallas_call(
        _qkv_cast_body,
        grid=(b, nt),
        in_specs=[
            tok,
            pl.BlockSpec((None, None, 6, d), lambda bi, ti: (layer, bi, 0, 0)),
            pl.BlockSpec((1, d), lambda bi, ti: (0, 0)),
            _resident((d, 3 * d), lambda bi, ti: (0, 0)),
            wo_blk, w1_blk, w2_blk,
        ],
        out_specs=[qkv_spec, qkv_spec, qkv_spec,
                   pl.BlockSpec((d, 3 * d), lambda bi, ti: (0, 0)), wo_blk, w1_blk, w2_blk],
        out_shape=[qkv_out, qkv_out, qkv_out,
                   jax.ShapeDtypeStruct((d, 3 * d), BF16), jax.ShapeDtypeStruct((d, d), BF16),
                   jax.ShapeDtypeStruct(mlp_w1.shape, BF16), jax.ShapeDtypeStruct(mlp_w2.shape, BF16)],
        compiler_params=_cparams(("arbitrary", "arbitrary")),
    )(x, mods4, g, wqkv, wo, mlp_w1, mlp_w2)


def _bias_chunk(r_ref, o_ref, chunk, tn):
    col0 = chunk * tn
    e = col0 + lax.broadcasted_iota(jnp.int32, (64, tn), 1)
    r = lax.broadcasted_iota(jnp.int32, (64, tn), 0)
    cq = e // (2 * GRID_W)
    half = (e // GRID_W) % 2
    ck = e % GRID_W
    sel = ((r // 32 == half) & (r % 32 == ck - cq + WIN_COLS - 1)).astype(BF16)
    rv = r_ref[...]
    hi = rv.astype(BF16)
    rem = rv - hi.astype(F32)
    mid = rem.astype(BF16)
    lo = (rem - mid.astype(F32)).astype(BF16)
    acc = jnp.dot(hi, sel, preferred_element_type=F32)
    acc += jnp.dot(mid, sel, preferred_element_type=F32)
    acc += jnp.dot(lo, sel, preferred_element_type=F32)
    e1 = col0 + lax.broadcasted_iota(jnp.int32, (1, tn), 1)
    cq1 = e1 // (2 * GRID_W)
    ck1 = e1 % GRID_W
    start = jnp.clip(cq1 - WIN_COLS // 2, 0, GRID_W - WIN_COLS)
    ok = (ck1 >= start) & (ck1 < start + WIN_COLS)
    vals = acc * LOG2E + jnp.where(ok, 0.0, NEG_INF)
    n_tables = r_ref.shape[0]
    w = 2 * GRID_W
    for c in range(tn // w):
        o_ref[pl.ds(chunk * (tn // w) + c, n_tables, stride=GRID_W), :] = vals[:, c * w:(c + 1) * w]


def _bias_operand(rpb):
    h, nr, nc = rpb.shape
    rp = jnp.pad(rpb, ((0, 0), (0, 0), (0, 32 - nc)))
    r2 = jnp.concatenate([rp[:, 0:N_DR_TILES], rp[:, 1:N_DR_TILES + 1]], axis=-1)
    return r2.reshape(h * N_DR_TILES, 64)


def _attn_body(q_ref, k_ref, v_ref, qc_ref, kc_ref, vc_ref, gq_ref, gk_ref, bias_ref, o_ref, oc_ref,
               qn_s, kn_s, s_s, p_s, l_s, sc_s, pc_s, lc_s, *, n_rows):
    n_pairs, s_len, pw = q_ref.shape
    c_len = qc_ref.shape[1]
    hd = pw // 2
    kwin = WIN_ROWS * GRID_W
    nt = (((1,), (1,)), ((), ()))
    rps = ATTN_ROWS_PER_STEP
    n_blocks = n_rows // rps
    blk_tokens = rps * GRID_W

    lane = lax.broadcasted_iota(jnp.int32, (1, 2 * hd), 1)
    first = lane < hd
    ri = lax.broadcasted_iota(jnp.int32, (4 * hd, 4 * hd), 0)
    ci = lax.broadcasted_iota(jnp.int32, (4 * hd, 4 * hd), 1)
    mean_blk = jnp.where(ri // hd == ci // hd, 1.0 / hd, 0.0).astype(BF16)
    gain = gq_ref[...] * gk_ref[...] * (hd ** -0.5 * LOG2E)

    def normalise(src_q, src_k, pair, src_off, dst_off, n):
        q = src_q[pair, pl.ds(src_off, n), :].astype(F32)
        k = src_k[pair, pl.ds(src_off, n), :].astype(F32)
        ms = jnp.dot(jnp.concatenate([q * q, k * k], axis=1).astype(BF16), mean_blk, preferred_element_type=F32)
        qn_s[pl.ds(dst_off, n), :] = (q * lax.rsqrt(ms[:, 0:2 * hd] + EPS) * gain).astype(BF16)
        kn_s[pl.ds(dst_off, n), :] = (k * lax.rsqrt(ms[:, 2 * hd:4 * hd] + EPS)).astype(BF16)

    def normalise_block(pair, blk):
        off = pl.multiple_of(blk * blk_tokens, blk_tokens)
        normalise(q_ref, k_ref, pair, off, off, blk_tokens)

    def kc():
        return kn_s[pl.ds(s_len, c_len), :]

    def stack_heads(q):
        zero = jnp.zeros_like(q)
        return jnp.concatenate([jnp.where(first, q, zero), jnp.where(first, zero, q)], axis=0)

    def pick_heads(acc):
        n = acc.shape[0] // 2
        return jnp.where(first, acc[:n], acc[n:])

    def win_start(r):
        return jnp.clip(r - WIN_ROWS // 2, 0, n_rows - WIN_ROWS)

    def scores(pair, r, slot):
        rs = win_start(r)
        qq = stack_heads(qn_s[pl.ds(pl.multiple_of(r * GRID_W, GRID_W), GRID_W), :])
        kw = kn_s[pl.ds(pl.multiple_of(rs * GRID_W, GRID_W), kwin), :]
        t0 = rs - r + WIN_ROWS - 1
        bias = jnp.concatenate(
            [jnp.concatenate([bias_ref[2 * pair + hh, t0 + 2 * ii] for ii in range(WIN_ROWS // 2)], axis=1)
             for hh in range(2)], axis=0)
        s_s[slot, :, 0:kwin] = lax.dot_general(qq, kw, nt, preferred_element_type=F32) + bias
        s_s[slot, :, kwin:kwin + c_len] = lax.dot_general(qq, kc(), nt, preferred_element_type=F32)

    def softmax2(s):
        p = jnp.exp2(s - s.max(axis=-1, keepdims=True))
        return p.astype(BF16), 1.0 / p.sum(axis=-1, keepdims=True)

    def probs(slot):
        p_s[slot], l_s[slot] = softmax2(s_s[slot])

    def output(pair, r, slot):
        vw = v_ref[pair, pl.ds(pl.multiple_of(win_start(r) * GRID_W, GRID_W), kwin), :]
        acc = jnp.dot(p_s[slot], jnp.concatenate([vw, vc_ref[pair]], axis=0), preferred_element_type=F32)
        o_ref[pair, pl.ds(pl.multiple_of(r * GRID_W, GRID_W), GRID_W), :] = (
            pick_heads(acc * l_s[slot]).astype(o_ref.dtype))

    def ctx_scores():
        qqc = stack_heads(qn_s[pl.ds(s_len, c_len), :])
        sc_s[...] = lax.dot_general(qqc, kc(), nt, preferred_element_type=F32)

    def ctx_probs():
        pc_s[...], lc_s[...] = softmax2(sc_s[...])

    def ctx_output(pair):
        acc = jnp.dot(pc_s[...], vc_ref[pair], preferred_element_type=F32)
        oc_ref[pair] = pick_heads(acc * lc_s[...]).astype(oc_ref.dtype)

    def step(pair, blk, par, out_of=None, probs_of=True, do_scores=True):
        if out_of is None:
            out_of = (pair, blk - 2)
        if out_of is not False:
            for a in range(rps):
                output(out_of[0], out_of[1] * rps + a, par * rps + a)
        if probs_of:
            for a in range(rps):
                probs((1 - par) * rps + a)
        if do_scores:
            for a in range(rps):
                scores(pair, blk * rps + a, par * rps + a)
            normalise_block(pair, jnp.minimum(blk + 2, n_blocks - 1))

    for pair in range(n_pairs):
        normalise(qc_ref, kc_ref, pair, 0, s_len, c_len)
        normalise_block(pair, 0)
        normalise_block(pair, 1)
        if pair == 0:
            step(pair, 0, 0, out_of=False, probs_of=False)
            ctx_scores()
            step(pair, 1, 1, out_of=False)
        else:
            step(pair, 0, 0, out_of=(pair - 1, n_blocks - 2))
            ctx_output(pair - 1)
            ctx_scores()
            step(pair, 1, 1, out_of=(pair - 1, n_blocks - 1))
        ctx_probs()

        def loop_step(j, carry, pair=pair):
            step(pair, 2 * j, 0)
            step(pair, 2 * j + 1, 1)
            return carry

        lax.fori_loop(1, n_blocks // 2, loop_step, 0)

    last = n_pairs - 1
    step(last, n_blocks, 0, do_scores=False)
    ctx_output(last)
    step(last, n_blocks + 1, 1, probs_of=False, do_scores=False)


def _attention(q, k, v, qc, kc, vc, gq2, gk2, bias_tab):
    b, n_pairs, s_len, pw = q.shape
    c_len = qc.shape[2]
    pps = ATTN_PAIRS_PER_STEP
    n_rows = s_len // GRID_W
    n_keys = WIN_ROWS * GRID_W + c_len
    assert n_rows >= WIN_ROWS and n_rows % (2 * ATTN_ROWS_PER_STEP) == 0 and n_pairs % pps == 0
    assert ATTN_ROWS_PER_STEP >= WIN_ROWS // 2
    lat = pl.BlockSpec((None, pps, s_len, pw), lambda bi, pi: (bi, pi, 0, 0))
    cx = pl.BlockSpec((None, pps, c_len, pw), lambda bi, pi: (bi, pi, 0, 0))
    gain = pl.BlockSpec((1, pw), lambda bi, pi: (0, 0))
    return pl.pallas_call(
        functools.partial(_attn_body, n_rows=n_rows),
        grid=(b, n_pairs // pps),
        in_specs=[lat, lat, lat, cx, cx, cx, gain, gain,
                  pl.BlockSpec((2 * pps, N_DR_TILES, GRID_W, 2 * GRID_W), lambda bi, pi: (pi, 0, 0, 0))],
        out_specs=[lat, cx],
        out_shape=[jax.ShapeDtypeStruct(q.shape, BF16), jax.ShapeDtypeStruct(qc.shape, BF16)],
        scratch_shapes=[pltpu.VMEM((s_len + c_len, pw), BF16)] * 2 + [
            pltpu.VMEM((2 * ATTN_ROWS_PER_STEP, 2 * GRID_W, n_keys), F32),
            pltpu.VMEM((2 * ATTN_ROWS_PER_STEP, 2 * GRID_W, n_keys), BF16),
            pltpu.VMEM((2 * ATTN_ROWS_PER_STEP, 2 * GRID_W, 1), F32),
            pltpu.VMEM((2 * c_len, c_len), F32), pltpu.VMEM((2 * c_len, c_len), BF16),
            pltpu.VMEM((2 * c_len, 1), F32)],
        compiler_params=_cparams(("parallel", "parallel")),
    )(q, k, v, qc, kc, vc, gq2, gk2, bias_tab)


def _proj_mlp_body(x_ref, o_ref, mod_ref, g_ref, wo_ref, w1_ref, w2_ref, out_ref):
    o = jnp.concatenate([o_ref[p] for p in range(o_ref.shape[0])], axis=1)
    y = jnp.dot(o, wo_ref[...], preferred_element_type=F32)
    x = x_ref[...] + mod_ref[2:3, :] * y
    h = _norm_mod(x, g_ref[...], mod_ref[3:4, :], mod_ref[4:5, :]).astype(BF16)
    out_ref[...] = x + mod_ref[5:6, :] * _sq_relu_mlp(h, w1_ref, w2_ref)


def _proj_mlp(x, o, mods4, layer, ctx_row, g, wo, w1, w2, tm):
    b, t, d = x.shape
    d_ff = w1.shape[-1]
    mod_idx = (lambda bi, ti: (layer, bi, 0, 0)) if ctx_row is None else (lambda bi, ti: (layer, ctx_row, 0, 0))
    tok = pl.BlockSpec((None, tm, d), lambda bi, ti: (bi, ti, 0))
    const = lambda bi, ti: (0, 0)
    this_layer = lambda bi, ti: (layer, 0, 0)
    o_spec, _ = _pair_major(b, t, d, tm)
    return pl.pallas_call(
        _proj_mlp_body,
        grid=(b, t // tm),
        in_specs=[tok, o_spec, pl.BlockSpec((None, None, 6, d), mod_idx), pl.BlockSpec((1, d), const),
                  _resident((d, d), const), _resident((None, d, d_ff), this_layer),
                  _resident((None, d_ff, d), this_layer)],
        out_specs=tok,
        out_shape=jax.ShapeDtypeStruct((b, t, d), F32),
        compiler_params=_cparams(("parallel", "parallel")),
    )(x, o, mods4, g, wo, w1, w2)


def _dft_tables(n, dg):
    l2 = FFT_L2
    l1 = n // l2
    kb = SUBLANES
    ar = np.arange
    a = 2.0 * np.pi * np.outer(ar(dg), ar(dg)) / dg
    norm = 1.0 / math.sqrt(n * dg)
    cd, sd = np.cos(a) * norm, np.sin(a) * norm
    al = 2.0 * np.pi * np.outer(ar(l1), ar(l1)) / l1
    c1, s1 = np.cos(al), np.sin(al)
    m1 = np.block([[c1, -s1], [-s1, -c1]])
    k = ar(l1)[:, None, None] + l1 * ar(l2)[None, :, None]
    be = 2.0 * np.pi * (k * ar(l2)[None, None, :] % n) / n
    m2 = np.concatenate([np.cos(be), np.sin(be)], axis=-1)
    m2 = m2.reshape(l1 // kb, kb, l2, 2, l2)
    big = np.zeros((l1 // kb, l2, kb, 2, l2, kb))
    for kk in range(kb):
        big[:, :, kk, :, :, kk] = m2[:, kk]
    m2 = big.reshape(l1 // kb, l2 * kb, 2 * l2 * kb)
    return (jnp.asarray(cd, F32), jnp.asarray(sd, F32), jnp.asarray(m1, F32), jnp.asarray(m2, F32))


def _put_cols(s_ref, first, val, rows=slice(None)):
    for c in range(val.shape[1] // LANES):
        s_ref[first + c, rows, :] = val[:, c * LANES:(c + 1) * LANES]


def _get_cols(s_ref, first, n_chunks, rows=slice(None)):
    return jnp.concatenate([s_ref[first + c, rows, :] for c in range(n_chunks)], axis=1)


def _fold_body(cd_ref, sd_ref, w_ref, o_ref):
    w = w_ref[...]
    hp = lax.Precision.HIGHEST
    o_ref[0] = jnp.dot(cd_ref[...], w, precision=hp, preferred_element_type=F32).astype(BF16)
    o_ref[1] = jnp.dot(sd_ref[...], w, precision=hp, preferred_element_type=F32).astype(BF16)


def _dft1_body(x_ref, mod_ref, g_ref, w_ref, m_ref, z_ref, p_s):
    l1, nb, d = x_ref.shape
    x = x_ref[...].reshape(l1 * nb, d)
    h = _norm_mod(x, g_ref[...], mod_ref[0:1, :], mod_ref[1:2, :]).astype(BF16)
    nc = d // LANES
    _put_cols(p_s, 0, jnp.dot(h, w_ref[0], preferred_element_type=F32))
    _put_cols(p_s, nc, jnp.dot(h, w_ref[1], preferred_element_type=F32))
    m1 = m_ref[...].astype(BF16)
    for m in range(nb):
        rows = pl.ds(m, l1, stride=nb)
        pm = jnp.concatenate([_get_cols(p_s, 0, nc, rows), _get_cols(p_s, nc, nc, rows)], axis=0).astype(BF16)
        z_ref[m] = jnp.dot(m1, pm, preferred_element_type=F32)


def _fourier_stage1(x, mods4, layer, g, wf, m1):
    b, s_len, d = x.shape
    l1 = s_len // FFT_L2
    nb = SUBLANES
    return pl.pallas_call(
        _dft1_body,
        grid=(b, FFT_L2 // nb),
        in_specs=[
            pl.BlockSpec((None, l1, nb, d), lambda bi, ti: (bi, 0, ti, 0)),
            pl.BlockSpec((None, None, 6, d), lambda bi, ti: (layer, bi, 0, 0)),
            pl.BlockSpec((1, d), lambda bi, ti: (0, 0)),
            _resident((2, d, d), lambda bi, ti: (0, 0, 0)),
            pl.BlockSpec((2 * l1, 2 * l1), lambda bi, ti: (0, 0)),
        ],
        out_specs=pl.BlockSpec((None, nb, 2 * l1, d), lambda bi, ti: (bi, ti, 0, 0)),
        out_shape=jax.ShapeDtypeStruct((b, FFT_L2, 2 * l1, d), F32),
        scratch_shapes=[pltpu.VMEM((2 * d // LANES, l1 * nb, LANES), F32)],
        compiler_params=_cparams(("parallel", "parallel")),
    )(x.reshape(b, l1, FFT_L2, d), mods4, g, wf, m1)


def _dft2_mlp_body(zr_ref, zi_ref, x_ref, m_ref, mod_ref, g_ref, w1_ref, w2_ref, out_ref):
    l2, kb, d = x_ref.shape
    n = l2 * kb
    zz = jnp.concatenate([zr_ref[...].reshape(n, d), zi_ref[...].reshape(n, d)], axis=0).astype(BF16)
    y = jnp.dot(m_ref[...].astype(BF16), zz, preferred_element_type=F32)
    x = x_ref[...].reshape(n, d) + mod_ref[2:3, :] * y
    h = _norm_mod(x, g_ref[...], mod_ref[3:4, :], mod_ref[4:5, :]).astype(BF16)
    out = x + mod_ref[5:6, :] * _sq_relu_mlp(h, w1_ref, w2_ref)
    out_ref[...] = out.reshape(l2, kb, d)


def _fourier_stage2_mlp(z, x, mods4, layer, g, m2, w1, w2):
    b, s_len, d = x.shape
    l1 = s_len // FFT_L2
    kb = SUBLANES
    d_ff = w1.shape[-1]
    zblk = lambda part: pl.BlockSpec((None, FFT_L2, None, kb, d), lambda bi, ki: (bi, 0, part, ki, 0))
    xblk = pl.BlockSpec((None, FFT_L2, kb, d), lambda bi, ki: (bi, 0, ki, 0))
    const = lambda bi, ki: (0, 0)
    this_layer = lambda bi, ki: (layer, 0, 0)
    z5 = z.reshape(b, FFT_L2, 2, l1, d)
    out = pl.pallas_call(
        _dft2_mlp_body,
        grid=(b, l1 // kb),
        in_specs=[
            zblk(0), zblk(1), xblk,
            pl.BlockSpec((None, FFT_L2 * kb, 2 * FFT_L2 * kb), lambda bi, ki: (ki, 0, 0)),
            pl.BlockSpec((None, None, 6, d), lambda bi, ki: (layer, bi, 0, 0)),
            pl.BlockSpec((1, d), const),
            _resident((None, d, d_ff), this_layer), _resident((None, d_ff, d), this_layer),
        ],
        out_specs=xblk,
        out_shape=jax.ShapeDtypeStruct((b, FFT_L2, l1, d), F32),
        compiler_params=_cparams(("parallel", "parallel")),
    )(z5, z5, x.reshape(b, FFT_L2, l1, d), m2, mods4, g, w1, w2)
    return out.reshape(b, s_len, d)


def kernel(x, c, ctx, c_ctx, ada_w, ada_b, norm_g, attn_wqkv, attn_wo, q_norm_g, k_norm_g, rpb,
           fourier_w, mlp_w1, mlp_w2):
    b, s_len, d = x.shape
    depth = ada_w.shape[0]
    hd = q_norm_g.shape[-1]
    assert depth == 2 and b + 1 <= 8 and 2 * hd == LANES and d % LANES == 0
    assert s_len % (FFT_L2 * SUBLANES) == 0 and s_len % TOKEN_TILE == 0
    ctx_row = b

    assert rpb.shape[0] == 1 and fourier_w.shape[0] == 1
    cc = jnp.zeros((8, d), F32).at[:b].set(c).at[b].set(c_ctx)
    cd, sd, m1, m2 = _dft_tables(s_len, d // FOURIER_GROUPS)
    mods, bias_tab, wf = _prepare_parameters(cc, ada_w, ada_b, rpb[0], cd, sd, fourier_w[0])
    mods4 = mods.reshape(depth, 8, 6, d)

    for layer in range(depth):
        is_attn = layer % N_MIXERS == 0
        idx = layer // N_MIXERS
        need_ctx = layer < depth - 1
        g_pre = norm_g[layer, 0][None]
        g_post = norm_g[layer, 1][None]
        if is_attn:
            assert layer == 0, "the first layer's QKV kernel also casts every layer's MLP weights"
            gq2 = jnp.tile(q_norm_g[idx], 2)[None]
            gk2 = jnp.tile(k_norm_g[idx], 2)[None]
            q, k, v, wqkv, wo, w1, w2 = _qkv_and_weight_casts(
                x, mods4, layer, g_pre, attn_wqkv[idx], attn_wo[idx], mlp_w1, mlp_w2, TOKEN_TILE)
            qc, kc, vc = _qkv(ctx, mods4, layer, ctx_row, g_pre, wqkv, ctx.shape[1])
            o, oc = _attention(q, k, v, qc, kc, vc, gq2, gk2, bias_tab)
            x = _proj_mlp(x, o, mods4, layer, None, g_post, wo, w1, w2, TOKEN_TILE)
            if need_ctx:
                ctx = _proj_mlp(ctx, oc, mods4, layer, ctx_row, g_post, wo, w1, w2, ctx.shape[1])
        else:
            assert not need_ctx, "a Fourier layer that still feeds a context stream is not supported"
            z = _fourier_stage1(x, mods4, layer, g_pre, wf, m1)
            x = _fourier_stage2_mlp(z, x, mods4, layer, g_post, m2, w1, w2)
    return x
```

```python
import functools
import math

import numpy as np
import jax
import jax.numpy as jnp
from jax import lax
from jax.experimental import pallas as pl
from jax.experimental.pallas import tpu as pltpu

F32 = jnp.float32
BF16 = jnp.bfloat16

GRID_W = 64
WIN_ROWS = 8
WIN_COLS = 16
FOURIER_GROUPS = 4
N_MIXERS = 2
EPS = 1e-6
NEG_INF = float("-inf")
LOG2E = math.log2(math.e)

SUBLANES = 8
LANES = 128
VMEM_LIMIT = 56 * 1024 * 1024

N_DR_TILES = 2 * WIN_ROWS - 2
DC_PAD = 32
MODS_COL_TILE = 1536
ATTN_ROWS_PER_STEP = 4
ATTN_PAIRS_PER_STEP = 2
FFT_L2 = 64
TOKEN_TILE = 512
FF_CHUNK = 1024


def _cparams(sem):
    return pltpu.CompilerParams(dimension_semantics=sem, vmem_limit_bytes=VMEM_LIMIT)


def _resident(shape, index_map):
    return pl.BlockSpec(shape, index_map, pipeline_mode=pl.Buffered(1))


def _mods_body(c_ref, w_ref, b_ref, o_ref):
    c = c_ref[...]
    s = (c * jax.nn.sigmoid(c)).astype(BF16)
    o_ref[...] = jnp.dot(s, w_ref[...].astype(BF16), preferred_element_type=F32) + b_ref[...]


def _prep_body(c_ref, w_ref, b_ref, r_ref, cd_ref, sd_ref, fw_ref, mods_ref, bias_ref, wf_ref, *, bias_tn):
    _mods_body(c_ref, w_ref, b_ref, mods_ref)
    _bias_chunk(r_ref, bias_ref, pl.program_id(0), bias_tn)
    _fold_body(cd_ref, sd_ref, fw_ref, wf_ref)


def _prepare_parameters(cc, ada_w, ada_b, rpb, cd, sd, fourier_w):
    n_layers, d, d6 = ada_w.shape
    h = rpb.shape[0]
    dg = cd.shape[0]
    rows = cc.shape[0]
    tn = MODS_COL_TILE
    nt = d6 // tn
    n_steps = n_layers * nt
    bias_tn = GRID_W * 2 * GRID_W // n_steps
    n_groups = d // dg
    per_group = n_steps // n_groups
    fcols = d // per_group
    assert bias_tn % (2 * GRID_W) == 0 and n_steps % n_groups == 0 and fcols % LANES == 0
    n_bias_rows = h * N_DR_TILES * GRID_W
    mods, bias, wf = pl.pallas_call(
        functools.partial(_prep_body, bias_tn=bias_tn),
        grid=(n_steps,),
        in_specs=[
            pl.BlockSpec((rows, d), lambda i: (0, 0)),
            pl.BlockSpec((None, d, tn), lambda i: (i // nt, 0, i % nt)),
            pl.BlockSpec((None, 1, tn), lambda i: (i // nt, 0, i % nt)),
            pl.BlockSpec((h * N_DR_TILES, 2 * DC_PAD), lambda i: (0, 0)),
            pl.BlockSpec((dg, dg), lambda i: (0, 0)), pl.BlockSpec((dg, dg), lambda i: (0, 0)),
            pl.BlockSpec((dg, fcols), lambda i: (i // per_group, i % per_group)),
        ],
        out_specs=[
            pl.BlockSpec((None, rows, tn), lambda i: (i // nt, 0, i % nt)),
            pl.BlockSpec((n_bias_rows, 2 * GRID_W), lambda i: (0, 0)),
            pl.BlockSpec((2, dg, fcols), lambda i: (0, i // per_group, i % per_group)),
        ],
        out_shape=[jax.ShapeDtypeStruct((n_layers, rows, d6), F32),
                   jax.ShapeDtypeStruct((n_bias_rows, 2 * GRID_W), F32),
                   jax.ShapeDtypeStruct((2, d, d), BF16)],
        compiler_params=_cparams(("arbitrary",)),
    )(cc, ada_w, ada_b.reshape(n_layers, 1, d6), _bias_operand(rpb), cd, sd, fourier_w)
    return mods, bias.reshape(h, N_DR_TILES, GRID_W, 2 * GRID_W), wf


def _norm_mod(x, g, shift, scale):
    ms = jnp.mean(x * x, axis=-1, keepdims=True)
    return (x * lax.rsqrt(ms + EPS) * g) * (1.0 + scale) + shift


def _sq_relu_mlp(h, w1_ref, w2_ref):
    acc = None
    for c in range(0, w1_ref.shape[1], FF_CHUNK):
        a = jnp.maximum(jnp.dot(h, w1_ref[:, c:c + FF_CHUNK], preferred_element_type=F32), 0.0)
        part = jnp.dot((a * a).astype(BF16), w2_ref[c:c + FF_CHUNK, :], preferred_element_type=F32)
        acc = part if acc is None else acc + part
    return acc


def _qkv_body(x_ref, mod_ref, g_ref, w_ref, q_ref, k_ref, v_ref):
    d = x_ref.shape[-1]
    h = _norm_mod(x_ref[...], g_ref[...], mod_ref[0:1, :], mod_ref[1:2, :]).astype(BF16)
    for i, out_ref in enumerate((q_ref, k_ref, v_ref)):
        y = jnp.dot(h, w_ref[:, i * d:(i + 1) * d], preferred_element_type=F32).astype(BF16)
        for p in range(out_ref.shape[0]):
            out_ref[p] = y[:, p * LANES:(p + 1) * LANES]


def _pair_major(b, t, d, tm):
    spec = pl.BlockSpec((None, d // LANES, tm, LANES), lambda bi, ti: (bi, 0, ti, 0))
    return spec, jax.ShapeDtypeStruct((b, d // LANES, t, LANES), BF16)


def _qkv(x, mods4, layer, ctx_row, g, w_bf16, tm):
    b, t, d = x.shape
    mod_idx = (lambda bi, ti: (layer, bi, 0, 0)) if ctx_row is None else (lambda bi, ti: (layer, ctx_row, 0, 0))
    tok = pl.BlockSpec((None, tm, d), lambda bi, ti: (bi, ti, 0))
    out_spec, out = _pair_major(b, t, d, tm)
    return pl.pallas_call(
        _qkv_body,
        grid=(b, t // tm),
        in_specs=[
            tok,
            pl.BlockSpec((None, None, 6, d), mod_idx),
            pl.BlockSpec((1, d), lambda bi, ti: (0, 0)),
            _resident((d, 3 * d), lambda bi, ti: (0, 0)),
        ],
        out_specs=[out_spec, out_spec, out_spec],
        out_shape=[out, out, out],
        compiler_params=_cparams(("parallel", "parallel")),
    )(x, mods4, g, w_bf16)


def _qkv_cast_body(x_ref, mod_ref, g_ref, wqkv_ref, wo_ref, w1_ref, w2_ref,
                   q_ref, k_ref, v_ref, wqkv_o, wo_o, w1_o, w2_o):
    @pl.when((pl.program_id(0) == 0) & (pl.program_id(1) == 0))
    def _():
        wqkv_o[...] = wqkv_ref[...].astype(BF16)

    wo_o[...] = wo_ref[...].astype(BF16)
    w1_o[...] = w1_ref[...].astype(BF16)
    w2_o[...] = w2_ref[...].astype(BF16)
    _qkv_body(x_ref, mod_ref, g_ref, wqkv_o, q_ref, k_ref, v_ref)


def _qkv_and_weight_casts(x, mods4, layer, g, wqkv, wo, mlp_w1, mlp_w2, tm):
    b, t, d = x.shape
    nt = t // tm
    n_steps = b * nt
    n_layers, _, d_ff = mlp_w1.shape
    assert d % (2 * SUBLANES * n_steps) == 0
    step = lambda bi, ti: bi * nt + ti
    tok = pl.BlockSpec((None, tm, d), lambda bi, ti: (bi, ti, 0))
    qkv_spec, qkv_out = _pair_major(b, t, d, tm)
    wo_blk = pl.BlockSpec((d // n_steps, d), lambda bi, ti: (step(bi, ti), 0))
    w1_blk = pl.BlockSpec((n_layers, d // n_steps, d_ff), lambda bi, ti: (0, step(bi, ti), 0))
    w2_blk = pl.BlockSpec((n_layers, d_ff // n_steps, d), lambda bi, ti: (0, step(bi, ti), 0))
    return pl.pallas_call(
        _qkv_cast_body,
        grid=(b, nt),
        in_specs=[
            tok,
            pl.BlockSpec((None, None, 6, d), lambda bi, ti: (layer, bi, 0, 0)),
            pl.BlockSpec((1, d), lambda bi, ti: (0, 0)),
            _resident((d, 3 * d), lambda bi, ti: (0, 0)),
            wo_blk, w1_blk, w2_blk,
        ],
        out_specs=[qkv_spec, qkv_spec, qkv_spec,
                   pl.BlockSpec((d, 3 * d), lambda bi, ti: (0, 0)), wo_blk, w1_blk, w2_blk],
        out_shape=[qkv_out, qkv_out, qkv_out,
                   jax.ShapeDtypeStruct((d, 3 * d), BF16), jax.ShapeDtypeStruct((d, d), BF16),
                   jax.ShapeDtypeStruct(mlp_w1.shape, BF16), jax.ShapeDtypeStruct(mlp_w2.shape, BF16)],
        compiler_params=_cparams(("arbitrary", "arbitrary")),
    )(x, mods4, g, wqkv, wo, mlp_w1, mlp_w2)


def _bias_chunk(r_ref, o_ref, chunk, tn):
    col0 = chunk * tn
    e = col0 + lax.broadcasted_iota(jnp.int32, (2 * DC_PAD, tn), 1)
    r = lax.broadcasted_iota(jnp.int32, (2 * DC_PAD, tn), 0)
    cq = e // (2 * GRID_W)
    half = (e // GRID_W) % 2
    ck = e % GRID_W
    sel = ((r // DC_PAD == half) & (r % DC_PAD == ck - cq + WIN_COLS - 1)).astype(BF16)
    rv = r_ref[...]
    hi = rv.astype(BF16)
    rem = rv - hi.astype(F32)
    mid = rem.astype(BF16)
    lo = (rem - mid.astype(F32)).astype(BF16)
    acc = jnp.dot(hi, sel, preferred_element_type=F32)
    acc += jnp.dot(mid, sel, preferred_element_type=F32)
    acc += jnp.dot(lo, sel, preferred_element_type=F32)
    e1 = col0 + lax.broadcasted_iota(jnp.int32, (1, tn), 1)
    cq1 = e1 // (2 * GRID_W)
    ck1 = e1 % GRID_W
    start = jnp.clip(cq1 - WIN_COLS // 2, 0, GRID_W - WIN_COLS)
    ok = (ck1 >= start) & (ck1 < start + WIN_COLS)
    vals = acc * LOG2E + jnp.where(ok, 0.0, NEG_INF)
    n_tables = r_ref.shape[0]
    w = 2 * GRID_W
    for c in range(tn // w):
        o_ref[pl.ds(chunk * (tn // w) + c, n_tables, stride=GRID_W), :] = vals[:, c * w:(c + 1) * w]


def _bias_operand(rpb):
    h, nr, nc = rpb.shape
    assert nr == 2 * WIN_ROWS - 1 and nc == 2 * WIN_COLS - 1 <= DC_PAD
    rp = jnp.pad(rpb, ((0, 0), (0, 0), (0, DC_PAD - nc)))
    r2 = jnp.concatenate([rp[:, 0:N_DR_TILES], rp[:, 1:N_DR_TILES + 1]], axis=-1)
    return r2.reshape(h * N_DR_TILES, 2 * DC_PAD)


def _attn_body(q_ref, k_ref, v_ref, qc_ref, kc_ref, vc_ref, gq_ref, gk_ref, bias_ref, o_ref, oc_ref,
               qn_s, kn_s, s_s, p_s, l_s, sc_s, pc_s, lc_s, *, n_rows):
    n_pairs, s_len, pw = q_ref.shape
    c_len = qc_ref.shape[1]
    hd = pw // 2
    kwin = WIN_ROWS * GRID_W
    nt = (((1,), (1,)), ((), ()))
    rps = ATTN_ROWS_PER_STEP
    n_blocks = n_rows // rps
    blk_tokens = rps * GRID_W

    lane = lax.broadcasted_iota(jnp.int32, (1, 2 * hd), 1)
    first = lane < hd
    ri = lax.broadcasted_iota(jnp.int32, (4 * hd, 4 * hd), 0)
    ci = lax.broadcasted_iota(jnp.int32, (4 * hd, 4 * hd), 1)
    mean_blk = jnp.where(ri // hd == ci // hd, 1.0 / hd, 0.0).astype(BF16)
    gain = gq_ref[...] * gk_ref[...] * (hd ** -0.5 * LOG2E)

    def normalise(src_q, src_k, pair, src_off, dst_off, n):
        q = src_q[pair, pl.ds(src_off, n), :].astype(F32)
        k = src_k[pair, pl.ds(src_off, n), :].astype(F32)
        ms = jnp.dot(jnp.concatenate([q * q, k * k], axis=1).astype(BF16), mean_blk, preferred_element_type=F32)
        qn_s[pl.ds(dst_off, n), :] = (q * lax.rsqrt(ms[:, 0:2 * hd] + EPS) * gain).astype(BF16)
        kn_s[pl.ds(dst_off, n), :] = (k * lax.rsqrt(ms[:, 2 * hd:4 * hd] + EPS)).astype(BF16)

    def normalise_block(pair, blk):
        off = pl.multiple_of(blk * blk_tokens, blk_tokens)
        normalise(q_ref, k_ref, pair, off, off, blk_tokens)

    def kc():
        return kn_s[pl.ds(s_len, c_len), :]

    def stack_heads(q):
        zero = jnp.zeros_like(q)
        return jnp.concatenate([jnp.where(first, q, zero), jnp.where(first, zero, q)], axis=0)

    def pick_heads(acc):
        n = acc.shape[0] // 2
        return jnp.where(first, acc[:n], acc[n:])

    def win_start(r):
        return jnp.clip(r - WIN_ROWS // 2, 0, n_rows - WIN_ROWS)

    def scores(pair, r, slot):
        rs = win_start(r)
        qq = stack_heads(qn_s[pl.ds(pl.multiple_of(r * GRID_W, GRID_W), GRID_W), :])
        kw = kn_s[pl.ds(pl.multiple_of(rs * GRID_W, GRID_W), kwin), :]
        t0 = rs - r + WIN_ROWS - 1
        bias = jnp.concatenate(
            [jnp.concatenate([bias_ref[2 * pair + hh, t0 + 2 * ii] for ii in range(WIN_ROWS // 2)], axis=1)
             for hh in range(2)], axis=0)
        s_s[slot, :, 0:kwin] = lax.dot_general(qq, kw, nt, preferred_element_type=F32) + bias
        s_s[slot, :, kwin:kwin + c_len] = lax.dot_general(qq, kc(), nt, preferred_element_type=F32)

    def softmax2(s):
        p = jnp.exp2(s - s.max(axis=-1, keepdims=True))
        return p.astype(BF16), 1.0 / p.sum(axis=-1, keepdims=True)

    def probs(slot):
        p_s[slot], l_s[slot] = softmax2(s_s[slot])

    def output(pair, r, slot):
        vw = v_ref[pair, pl.ds(pl.multiple_of(win_start(r) * GRID_W, GRID_W), kwin), :]
        acc = jnp.dot(p_s[slot], jnp.concatenate([vw, vc_ref[pair]], axis=0), preferred_element_type=F32)
        o_ref[pair, pl.ds(pl.multiple_of(r * GRID_W, GRID_W), GRID_W), :] = (
            pick_heads(acc * l_s[slot]).astype(o_ref.dtype))

    def ctx_scores():
        qqc = stack_heads(qn_s[pl.ds(s_len, c_len), :])
        sc_s[...] = lax.dot_general(qqc, kc(), nt, preferred_element_type=F32)

    def ctx_probs():
        pc_s[...], lc_s[...] = softmax2(sc_s[...])

    def ctx_output(pair):
        acc = jnp.dot(pc_s[...], vc_ref[pair], preferred_element_type=F32)
        oc_ref[pair] = pick_heads(acc * lc_s[...]).astype(oc_ref.dtype)

    def step(pair, blk, par, out_of=None, probs_of=True, do_scores=True):
        if out_of is None:
            out_of = (pair, blk - 2)
        if out_of is not False:
            for a in range(rps):
                output(out_of[0], out_of[1] * rps + a, par * rps + a)
        if probs_of:
            for a in range(rps):
                probs((1 - par) * rps + a)
        if do_scores:
            for a in range(rps):
                scores(pair, blk * rps + a, par * rps + a)
            normalise_block(pair, jnp.minimum(blk + 2, n_blocks - 1))

    for pair in range(n_pairs):
        normalise(qc_ref, kc_ref, pair, 0, s_len, c_len)
        normalise_block(pair, 0)
        normalise_block(pair, 1)
        if pair == 0:
            step(pair, 0, 0, out_of=False, probs_of=False)
            ctx_scores()
            step(pair, 1, 1, out_of=False)
        else:
            step(pair, 0, 0, out_of=(pair - 1, n_blocks - 2))
            ctx_output(pair - 1)
            ctx_scores()
            step(pair, 1, 1, out_of=(pair - 1, n_blocks - 1))
        ctx_probs()

        def loop_step(j, carry, pair=pair):
            step(pair, 2 * j, 0)
            step(pair, 2 * j + 1, 1)
            return carry

        lax.fori_loop(1, n_blocks // 2, loop_step, 0)

    last = n_pairs - 1
    step(last, n_blocks, 0, do_scores=False)
    ctx_output(last)
    step(last, n_blocks + 1, 1, probs_of=False, do_scores=False)


def _attention(q, k, v, qc, kc, vc, gq2, gk2, bias_tab):
    b, n_pairs, s_len, pw = q.shape
    c_len = qc.shape[2]
    pps = ATTN_PAIRS_PER_STEP
    n_rows = s_len // GRID_W
    n_keys = WIN_ROWS * GRID_W + c_len
    assert n_rows >= WIN_ROWS and n_rows % (2 * ATTN_ROWS_PER_STEP) == 0 and n_pairs % pps == 0
    assert ATTN_ROWS_PER_STEP >= WIN_ROWS // 2
    lat = pl.BlockSpec((None, pps, s_len, pw), lambda bi, pi: (bi, pi, 0, 0))
    cx = pl.BlockSpec((None, pps, c_len, pw), lambda bi, pi: (bi, pi, 0, 0))
    gain = pl.BlockSpec((1, pw), lambda bi, pi: (0, 0))
    return pl.pallas_call(
        functools.partial(_attn_body, n_rows=n_rows),
        grid=(b, n_pairs // pps),
        in_specs=[lat, lat, lat, cx, cx, cx, gain, gain,
                  pl.BlockSpec((2 * pps, N_DR_TILES, GRID_W, 2 * GRID_W), lambda bi, pi: (pi, 0, 0, 0))],
        out_specs=[lat, cx],
        out_shape=[jax.ShapeDtypeStruct(q.shape, BF16), jax.ShapeDtypeStruct(qc.shape, BF16)],
        scratch_shapes=[pltpu.VMEM((s_len + c_len, pw), BF16)] * 2 + [
            pltpu.VMEM((2 * ATTN_ROWS_PER_STEP, 2 * GRID_W, n_keys), F32),
            pltpu.VMEM((2 * ATTN_ROWS_PER_STEP, 2 * GRID_W, n_keys), BF16),
            pltpu.VMEM((2 * ATTN_ROWS_PER_STEP, 2 * GRID_W, 1), F32),
            pltpu.VMEM((2 * c_len, c_len), F32), pltpu.VMEM((2 * c_len, c_len), BF16),
            pltpu.VMEM((2 * c_len, 1), F32)],
        compiler_params=_cparams(("parallel", "parallel")),
    )(q, k, v, qc, kc, vc, gq2, gk2, bias_tab)


def _proj_mlp_body(x_ref, o_ref, mod_ref, g_ref, wo_ref, w1_ref, w2_ref, out_ref):
    o = jnp.concatenate([o_ref[p] for p in range(o_ref.shape[0])], axis=1)
    y = jnp.dot(o, wo_ref[...], preferred_element_type=F32)
    x = x_ref[...] + mod_ref[2:3, :] * y
    h = _norm_mod(x, g_ref[...], mod_ref[3:4, :], mod_ref[4:5, :]).astype(BF16)
    out_ref[...] = x + mod_ref[5:6, :] * _sq_relu_mlp(h, w1_ref, w2_ref)


def _proj_mlp(x, o, mods4, layer, ctx_row, g, wo, w1, w2, tm):
    b, t, d = x.shape
    d_ff = w1.shape[-1]
    mod_idx = (lambda bi, ti: (layer, bi, 0, 0)) if ctx_row is None else (lambda bi, ti: (layer, ctx_row, 0, 0))
    tok = pl.BlockSpec((None, tm, d), lambda bi, ti: (bi, ti, 0))
    const = lambda bi, ti: (0, 0)
    this_layer = lambda bi, ti: (layer, 0, 0)
    o_spec, _ = _pair_major(b, t, d, tm)
    return pl.pallas_call(
        _proj_mlp_body,
        grid=(b, t // tm),
        in_specs=[tok, o_spec, pl.BlockSpec((None, None, 6, d), mod_idx), pl.BlockSpec((1, d), const),
                  _resident((d, d), const), _resident((None, d, d_ff), this_layer),
                  _resident((None, d_ff, d), this_layer)],
        out_specs=tok,
        out_shape=jax.ShapeDtypeStruct((b, t, d), F32),
        compiler_params=_cparams(("parallel", "parallel")),
    )(x, o, mods4, g, wo, w1, w2)


def _dft_tables(n, dg):
    l2 = FFT_L2
    l1 = n // l2
    kb = SUBLANES
    ar = np.arange
    a = 2.0 * np.pi * np.outer(ar(dg), ar(dg)) / dg
    norm = 1.0 / math.sqrt(n * dg)
    cd, sd = np.cos(a) * norm, np.sin(a) * norm
    al = 2.0 * np.pi * np.outer(ar(l1), ar(l1)) / l1
    c1, s1 = np.cos(al), np.sin(al)
    m1 = np.block([[c1, -s1], [-s1, -c1]])
    k = ar(l1)[:, None, None] + l1 * ar(l2)[None, :, None]
    be = 2.0 * np.pi * (k * ar(l2)[None, None, :] % n) / n
    m2 = np.concatenate([np.cos(be), np.sin(be)], axis=-1)
    m2 = m2.reshape(l1 // kb, kb, l2, 2, l2)
    big = np.zeros((l1 // kb, l2, kb, 2, l2, kb))
    for kk in range(kb):
        big[:, :, kk, :, :, kk] = m2[:, kk]
    m2 = big.reshape(l1 // kb, l2 * kb, 2 * l2 * kb)
    return (jnp.asarray(cd, F32), jnp.asarray(sd, F32), jnp.asarray(m1, F32), jnp.asarray(m2, F32))


def _fold_body(cd_ref, sd_ref, w_ref, o_ref):
    w = w_ref[...]
    hp = lax.Precision.HIGHEST
    o_ref[0] = jnp.dot(cd_ref[...], w, precision=hp, preferred_element_type=F32).astype(BF16)
    o_ref[1] = jnp.dot(sd_ref[...], w, precision=hp, preferred_element_type=F32).astype(BF16)


def _dft1_body(x_ref, mod_ref, g_ref, w_ref, m_ref, z_ref):
    l1, nb, d = x_ref.shape
    x = x_ref[...].reshape(l1 * nb, d)
    h = _norm_mod(x, g_ref[...], mod_ref[0:1, :], mod_ref[1:2, :]).astype(BF16)

    def project(part):
        p = jnp.dot(h, w_ref[part], preferred_element_type=F32)
        return jnp.swapaxes(p.reshape(l1, nb, d), 0, 1).reshape(nb * l1, d).astype(BF16)

    pc, ps = project(0), project(1)
    m1 = m_ref[...].astype(BF16)
    zs = []
    for m in range(nb):
        rows = slice(m * l1, (m + 1) * l1)
        pm = jnp.concatenate([pc[rows], ps[rows]], axis=0)
        zs.append(jnp.dot(m1, pm, preferred_element_type=F32))
    for kblk in range(l1 // nb):
        for part in range(2):
            r0 = part * l1 + kblk * nb
            z_ref[kblk, part] = jnp.concatenate([z[r0:r0 + nb] for z in zs], axis=0).astype(BF16)


def _fourier_stage1(x, mods4, layer, g, wf, m1):
    b, s_len, d = x.shape
    l1 = s_len // FFT_L2
    nb = SUBLANES
    assert (nb * nb) % (2 * SUBLANES) == 0
    return pl.pallas_call(
        _dft1_body,
        grid=(b, FFT_L2 // nb),
        in_specs=[
            pl.BlockSpec((None, l1, nb, d), lambda bi, ti: (bi, 0, ti, 0)),
            pl.BlockSpec((None, None, 6, d), lambda bi, ti: (layer, bi, 0, 0)),
            pl.BlockSpec((1, d), lambda bi, ti: (0, 0)),
            _resident((2, d, d), lambda bi, ti: (0, 0, 0)),
            pl.BlockSpec((2 * l1, 2 * l1), lambda bi, ti: (0, 0)),
        ],
        out_specs=pl.BlockSpec((None, l1 // nb, 2, nb * nb, d), lambda bi, ti: (bi, 0, 0, ti, 0)),
        out_shape=jax.ShapeDtypeStruct((b, l1 // nb, 2, FFT_L2 * nb, d), BF16),
        compiler_params=_cparams(("parallel", "parallel")),
    )(x.reshape(b, l1, FFT_L2, d), mods4, g, wf, m1)


def _dft2_mlp_body(z_ref, x_ref, m_ref, mod_ref, g_ref, w1_ref, w2_ref, out_ref):
    l2, kb, d = x_ref.shape
    n = l2 * kb
    y = jnp.dot(m_ref[...].astype(BF16), z_ref[...].reshape(2 * n, d), preferred_element_type=F32)
    x = x_ref[...].reshape(n, d) + mod_ref[2:3, :] * y
    h = _norm_mod(x, g_ref[...], mod_ref[3:4, :], mod_ref[4:5, :]).astype(BF16)
    out = x + mod_ref[5:6, :] * _sq_relu_mlp(h, w1_ref, w2_ref)
    out_ref[...] = out.reshape(l2, kb, d)


def _fourier_stage2_mlp(z, x, mods4, layer, g, m2, w1, w2):
    b, s_len, d = x.shape
    l1 = s_len // FFT_L2
    kb = SUBLANES
    d_ff = w1.shape[-1]
    xblk = pl.BlockSpec((None, FFT_L2, kb, d), lambda bi, ki: (bi, 0, ki, 0))
    const = lambda bi, ki: (0, 0)
    this_layer = lambda bi, ki: (layer, 0, 0)
    out = pl.pallas_call(
        _dft2_mlp_body,
        grid=(b, l1 // kb),
        in_specs=[
            pl.BlockSpec((None, None, 2, FFT_L2 * kb, d), lambda bi, ki: (bi, ki, 0, 0, 0)), xblk,
            pl.BlockSpec((None, FFT_L2 * kb, 2 * FFT_L2 * kb), lambda bi, ki: (ki, 0, 0)),
            pl.BlockSpec((None, None, 6, d), lambda bi, ki: (layer, bi, 0, 0)),
            pl.BlockSpec((1, d), const),
            _resident((None, d, d_ff), this_layer), _resident((None, d_ff, d), this_layer),
        ],
        out_specs=xblk,
        out_shape=jax.ShapeDtypeStruct((b, FFT_L2, l1, d), F32),
        compiler_params=_cparams(("parallel", "parallel")),
    )(z, x.reshape(b, FFT_L2, l1, d), m2, mods4, g, w1, w2)
    return out.reshape(b, s_len, d)


def kernel(x, c, ctx, c_ctx, ada_w, ada_b, norm_g, attn_wqkv, attn_wo, q_norm_g, k_norm_g, rpb,
           fourier_w, mlp_w1, mlp_w2):
    b, s_len, d = x.shape
    depth = ada_w.shape[0]
    hd = q_norm_g.shape[-1]
    assert depth == 2 and b + 1 <= SUBLANES and 2 * hd == LANES and d % LANES == 0
    assert s_len % (FFT_L2 * SUBLANES) == 0 and s_len % TOKEN_TILE == 0
    ctx_row = b

    assert rpb.shape[0] == 1 and fourier_w.shape[0] == 1
    cc = jnp.zeros((SUBLANES, d), F32).at[:b].set(c).at[b].set(c_ctx)
    cd, sd, m1, m2 = _dft_tables(s_len, d // FOURIER_GROUPS)
    mods, bias_tab, wf = _prepare_parameters(cc, ada_w, ada_b, rpb[0], cd, sd, fourier_w[0])
    mods4 = mods.reshape(depth, SUBLANES, 6, d)

    for layer in range(depth):
        is_attn = layer % N_MIXERS == 0
        idx = layer // N_MIXERS
        need_ctx = layer < depth - 1
        g_pre = norm_g[layer, 0][None]
        g_post = norm_g[layer, 1][None]
        if is_attn:
            assert layer == 0, "the first layer's QKV kernel also casts every layer's MLP weights"
            gq2 = jnp.tile(q_norm_g[idx], 2)[None]
            gk2 = jnp.tile(k_norm_g[idx], 2)[None]
            q, k, v, wqkv, wo, w1, w2 = _qkv_and_weight_casts(
                x, mods4, layer, g_pre, attn_wqkv[idx], attn_wo[idx], mlp_w1, mlp_w2, TOKEN_TILE)
            qc, kc, vc = _qkv(ctx, mods4, layer, ctx_row, g_pre, wqkv, ctx.shape[1])
            o, oc = _attention(q, k, v, qc, kc, vc, gq2, gk2, bias_tab)
            x = _proj_mlp(x, o, mods4, layer, None, g_post, wo, w1, w2, TOKEN_TILE)
            if need_ctx:
                ctx = _proj_mlp(ctx, oc, mods4, layer, ctx_row, g_post, wo, w1, w2, ctx.shape[1])
        else:
            assert not need_ctx, "a Fourier layer that still feeds a context stream is not supported"
            z = _fourier_stage1(x, mods4, layer, g_pre, wf, m1)
            x = _fourier_stage2_mlp(z, x, mods4, layer, g_post, m2, w1, w2)
    return x
```

```python
import functools
import math

import numpy as np
import jax
import jax.numpy as jnp
from jax import lax
from jax.experimental import pallas as pl
from jax.experimental.pallas import tpu as pltpu

F32 = jnp.float32
BF16 = jnp.bfloat16

GRID_W = 64
WIN_ROWS = 8
WIN_COLS = 16
FOURIER_GROUPS = 4
N_MIXERS = 2
EPS = 1e-6
NEG_INF = float("-inf")
LOG2E = math.log2(math.e)

SUBLANES = 8
LANES = 128
VMEM_LIMIT = 56 * 1024 * 1024

N_DR_TILES = 2 * WIN_ROWS - 2
DC_PAD = 32
MODS_COL_TILE = 1536
ATTN_ROWS_PER_STEP = 4
ATTN_PAIRS_PER_STEP = 2
FFT_L2 = 64
TOKEN_TILE = 1024
FF_CHUNK = 1024


def _cparams(sem):
    return pltpu.CompilerParams(dimension_semantics=sem, vmem_limit_bytes=VMEM_LIMIT)


def _resident(shape, index_map):
    return pl.BlockSpec(shape, index_map, pipeline_mode=pl.Buffered(1))


def _mods_body(c_ref, w_ref, b_ref, o_ref):
    c = c_ref[...]
    s = (c * jax.nn.sigmoid(c)).astype(BF16)
    o_ref[...] = jnp.dot(s, w_ref[...].astype(BF16), preferred_element_type=F32) + b_ref[...]


def _prep_body(c_ref, w_ref, b_ref, r_ref, cd_ref, sd_ref, fw_ref, mods_ref, bias_ref, wf_ref, *, bias_tn):
    _mods_body(c_ref, w_ref, b_ref, mods_ref)
    _bias_chunk(r_ref, bias_ref, pl.program_id(0), bias_tn)
    _fold_body(cd_ref, sd_ref, fw_ref, wf_ref)


def _prepare_parameters(cc, ada_w, ada_b, rpb, cd, sd, fourier_w):
    n_layers, d, d6 = ada_w.shape
    h = rpb.shape[0]
    dg = cd.shape[0]
    rows = cc.shape[0]
    tn = MODS_COL_TILE
    nt = d6 // tn
    n_steps = n_layers * nt
    bias_tn = GRID_W * 2 * GRID_W // n_steps
    n_groups = d // dg
    per_group = n_steps // n_groups
    fcols = d // per_group
    assert bias_tn % (2 * GRID_W) == 0 and n_steps % n_groups == 0 and fcols % LANES == 0
    n_bias_rows = h * N_DR_TILES * GRID_W
    mods, bias, wf = pl.pallas_call(
        functools.partial(_prep_body, bias_tn=bias_tn),
        grid=(n_steps,),
        in_specs=[
            pl.BlockSpec((rows, d), lambda i: (0, 0)),
            pl.BlockSpec((None, d, tn), lambda i: (i // nt, 0, i % nt)),
            pl.BlockSpec((None, 1, tn), lambda i: (i // nt, 0, i % nt)),
            pl.BlockSpec((h * N_DR_TILES, 2 * DC_PAD), lambda i: (0, 0)),
            pl.BlockSpec((dg, dg), lambda i: (0, 0)), pl.BlockSpec((dg, dg), lambda i: (0, 0)),
            pl.BlockSpec((dg, fcols), lambda i: (i // per_group, i % per_group)),
        ],
        out_specs=[
            pl.BlockSpec((None, rows, tn), lambda i: (i // nt, 0, i % nt)),
            pl.BlockSpec((n_bias_rows, 2 * GRID_W), lambda i: (0, 0)),
            pl.BlockSpec((2, dg, fcols), lambda i: (0, i // per_group, i % per_group)),
        ],
        out_shape=[jax.ShapeDtypeStruct((n_layers, rows, d6), F32),
                   jax.ShapeDtypeStruct((n_bias_rows, 2 * GRID_W), F32),
                   jax.ShapeDtypeStruct((2, d, d), BF16)],
        compiler_params=_cparams(("arbitrary",)),
    )(cc, ada_w, ada_b.reshape(n_layers, 1, d6), _bias_operand(rpb), cd, sd, fourier_w)
    return mods, bias.reshape(h, N_DR_TILES, GRID_W, 2 * GRID_W), wf


def _norm_mod(x, g, shift, scale):
    ms = jnp.mean(x * x, axis=-1, keepdims=True)
    return (x * lax.rsqrt(ms + EPS) * g) * (1.0 + scale) + shift


def _sq_relu_mlp(h, w1_ref, w2_ref):
    acc = None
    for c in range(0, w1_ref.shape[1], FF_CHUNK):
        a = jnp.maximum(jnp.dot(h, w1_ref[:, c:c + FF_CHUNK], preferred_element_type=F32), 0.0)
        part = jnp.dot((a * a).astype(BF16), w2_ref[c:c + FF_CHUNK, :], preferred_element_type=F32)
        acc = part if acc is None else acc + part
    return acc


def _qkv_body(x_ref, mod_ref, g_ref, w_ref, q_ref, k_ref, v_ref):
    d = x_ref.shape[-1]
    h = _norm_mod(x_ref[...], g_ref[...], mod_ref[0:1, :], mod_ref[1:2, :]).astype(BF16)
    for i, out_ref in enumerate((q_ref, k_ref, v_ref)):
        y = jnp.dot(h, w_ref[:, i * d:(i + 1) * d], preferred_element_type=F32).astype(BF16)
        for p in range(out_ref.shape[0]):
            out_ref[p] = y[:, p * LANES:(p + 1) * LANES]


def _pair_major(b, t, d, tm):
    spec = pl.BlockSpec((None, d // LANES, tm, LANES), lambda bi, ti: (bi, 0, ti, 0))
    return spec, jax.ShapeDtypeStruct((b, d // LANES, t, LANES), BF16)


def _qkv(x, mods4, layer, ctx_row, g, w_bf16, tm):
    b, t, d = x.shape
    mod_idx = (lambda bi, ti: (layer, bi, 0, 0)) if ctx_row is None else (lambda bi, ti: (layer, ctx_row, 0, 0))
    tok = pl.BlockSpec((None, tm, d), lambda bi, ti: (bi, ti, 0))
    out_spec, out = _pair_major(b, t, d, tm)
    return pl.pallas_call(
        _qkv_body,
        grid=(b, t // tm),
        in_specs=[
            tok,
            pl.BlockSpec((None, None, 6, d), mod_idx),
            pl.BlockSpec((1, d), lambda bi, ti: (0, 0)),
            _resident((d, 3 * d), lambda bi, ti: (0, 0)),
        ],
        out_specs=[out_spec, out_spec, out_spec],
        out_shape=[out, out, out],
        compiler_params=_cparams(("parallel", "parallel")),
    )(x, mods4, g, w_bf16)


def _qkv_cast_body(x_ref, mod_ref, g_ref, wqkv_ref, wo_ref, w1_ref, w2_ref,
                   q_ref, k_ref, v_ref, wqkv_o, wo_o, w1_o, w2_o):
    @pl.when((pl.program_id(0) == 0) & (pl.program_id(1) == 0))
    def _():
        wqkv_o[...] = wqkv_ref[...].astype(BF16)

    wo_o[...] = wo_ref[...].astype(BF16)
    w1_o[...] = w1_ref[...].astype(BF16)
    w2_o[...] = w2_ref[...].astype(BF16)
    _qkv_body(x_ref, mod_ref, g_ref, wqkv_o, q_ref, k_ref, v_ref)


def _qkv_and_weight_casts(x, mods4, layer, g, wqkv, wo, mlp_w1, mlp_w2, tm):
    b, t, d = x.shape
    nt = t // tm
    n_steps = b * nt
    n_layers, _, d_ff = mlp_w1.shape
    assert d % (2 * SUBLANES * n_steps) == 0
    step = lambda bi, ti: bi * nt + ti
    tok = pl.BlockSpec((None, tm, d), lambda bi, ti: (bi, ti, 0))
    qkv_spec, qkv_out = _pair_major(b, t, d, tm)
    wo_blk = pl.BlockSpec((d // n_steps, d), lambda bi, ti: (step(bi, ti), 0))
    w1_blk = pl.BlockSpec((n_layers, d // n_steps, d_ff), lambda bi, ti: (0, step(bi, ti), 0))
    w2_blk = pl.BlockSpec((n_layers, d_ff // n_steps, d), lambda bi, ti: (0, step(bi, ti), 0))
    return pl.pallas_call(
        _qkv_cast_body,
        grid=(b, nt),
        in_specs=[
            tok,
            pl.BlockSpec((None, None, 6, d), lambda bi, ti: (layer, bi, 0, 0)),
            pl.BlockSpec((1, d), lambda bi, ti: (0, 0)),
            _resident((d, 3 * d), lambda bi, ti: (0, 0)),
            wo_blk, w1_blk, w2_blk,
        ],
        out_specs=[qkv_spec, qkv_spec, qkv_spec,
                   pl.BlockSpec((d, 3 * d), lambda bi, ti: (0, 0)), wo_blk, w1_blk, w2_blk],
        out_shape=[qkv_out, qkv_out, qkv_out,
                   jax.ShapeDtypeStruct((d, 3 * d), BF16), jax.ShapeDtypeStruct((d, d), BF16),
                   jax.ShapeDtypeStruct(mlp_w1.shape, BF16), jax.ShapeDtypeStruct(mlp_w2.shape, BF16)],
        compiler_params=_cparams(("arbitrary", "arbitrary")),
    )(x, mods4, g, wqkv, wo, mlp_w1, mlp_w2)


def _bias_chunk(r_ref, o_ref, chunk, tn):
    col0 = chunk * tn
    e = col0 + lax.broadcasted_iota(jnp.int32, (2 * DC_PAD, tn), 1)
    r = lax.broadcasted_iota(jnp.int32, (2 * DC_PAD, tn), 0)
    cq = e // (2 * GRID_W)
    half = (e // GRID_W) % 2
    ck = e % GRID_W
    sel = ((r // DC_PAD == half) & (r % DC_PAD == ck - cq + WIN_COLS - 1)).astype(BF16)
    rv = r_ref[...]
    hi = rv.astype(BF16)
    rem = rv - hi.astype(F32)
    mid = rem.astype(BF16)
    lo = (rem - mid.astype(F32)).astype(BF16)
    acc = jnp.dot(hi, sel, preferred_element_type=F32)
    acc += jnp.dot(mid, sel, preferred_element_type=F32)
    acc += jnp.dot(lo, sel, preferred_element_type=F32)
    e1 = col0 + lax.broadcasted_iota(jnp.int32, (1, tn), 1)
    cq1 = e1 // (2 * GRID_W)
    ck1 = e1 % GRID_W
    start = jnp.clip(cq1 - WIN_COLS // 2, 0, GRID_W - WIN_COLS)
    ok = (ck1 >= start) & (ck1 < start + WIN_COLS)
    vals = acc * LOG2E + jnp.where(ok, 0.0, NEG_INF)
    n_tables = r_ref.shape[0]
    w = 2 * GRID_W
    for c in range(tn // w):
        o_ref[pl.ds(chunk * (tn // w) + c, n_tables, stride=GRID_W), :] = vals[:, c * w:(c + 1) * w]


def _bias_operand(rpb):
    h, nr, nc = rpb.shape
    assert nr == 2 * WIN_ROWS - 1 and nc == 2 * WIN_COLS - 1 <= DC_PAD
    rp = jnp.pad(rpb, ((0, 0), (0, 0), (0, DC_PAD - nc)))
    r2 = jnp.concatenate([rp[:, 0:N_DR_TILES], rp[:, 1:N_DR_TILES + 1]], axis=-1)
    return r2.reshape(h * N_DR_TILES, 2 * DC_PAD)


def _attn_body(q_ref, k_ref, v_ref, qc_ref, kc_ref, vc_ref, gq_ref, gk_ref, bias_ref, o_ref, oc_ref,
               qn_s, kn_s, s_s, p_s, l_s, sc_s, pc_s, lc_s, *, n_rows):
    n_pairs, s_len, pw = q_ref.shape
    c_len = qc_ref.shape[1]
    hd = pw // 2
    kwin = WIN_ROWS * GRID_W
    nt = (((1,), (1,)), ((), ()))
    rps = ATTN_ROWS_PER_STEP
    n_blocks = n_rows // rps
    blk_tokens = rps * GRID_W

    lane = lax.broadcasted_iota(jnp.int32, (1, 2 * hd), 1)
    first = lane < hd
    ri = lax.broadcasted_iota(jnp.int32, (4 * hd, 4 * hd), 0)
    ci = lax.broadcasted_iota(jnp.int32, (4 * hd, 4 * hd), 1)
    mean_blk = jnp.where(ri // hd == ci // hd, 1.0 / hd, 0.0).astype(BF16)
    gain = gq_ref[...] * gk_ref[...] * (hd ** -0.5 * LOG2E)

    def normalise(src_q, src_k, pair, src_off, dst_off, n):
        q = src_q[pair, pl.ds(src_off, n), :].astype(F32)
        k = src_k[pair, pl.ds(src_off, n), :].astype(F32)
        ms = jnp.dot(jnp.concatenate([q * q, k * k], axis=1).astype(BF16), mean_blk, preferred_element_type=F32)
        qn_s[pl.ds(dst_off, n), :] = (q * lax.rsqrt(ms[:, 0:2 * hd] + EPS) * gain).astype(BF16)
        kn_s[pl.ds(dst_off, n), :] = (k * lax.rsqrt(ms[:, 2 * hd:4 * hd] + EPS)).astype(BF16)

    def normalise_block(pair, blk):
        off = pl.multiple_of(blk * blk_tokens, blk_tokens)
        normalise(q_ref, k_ref, pair, off, off, blk_tokens)

    def kc():
        return kn_s[pl.ds(s_len, c_len), :]

    def stack_heads(q):
        zero = jnp.zeros_like(q)
        return jnp.concatenate([jnp.where(first, q, zero), jnp.where(first, zero, q)], axis=0)

    def pick_heads(acc):
        n = acc.shape[0] // 2
        return jnp.where(first, acc[:n], acc[n:])

    def win_start(r):
        return jnp.clip(r - WIN_ROWS // 2, 0, n_rows - WIN_ROWS)

    def scores(pair, r, slot):
        rs = win_start(r)
        qq = stack_heads(qn_s[pl.ds(pl.multiple_of(r * GRID_W, GRID_W), GRID_W), :])
        kw = kn_s[pl.ds(pl.multiple_of(rs * GRID_W, GRID_W), kwin), :]
        t0 = rs - r + WIN_ROWS - 1
        bias = jnp.concatenate(
            [jnp.concatenate([bias_ref[2 * pair + hh, t0 + 2 * ii] for ii in range(WIN_ROWS // 2)], axis=1)
             for hh in range(2)], axis=0)
        s_s[slot, :, 0:kwin] = lax.dot_general(qq, kw, nt, preferred_element_type=F32) + bias
        s_s[slot, :, kwin:kwin + c_len] = lax.dot_general(qq, kc(), nt, preferred_element_type=F32)

    def softmax2(s):
        p = jnp.exp2(s - s.max(axis=-1, keepdims=True))
        return p.astype(BF16), 1.0 / p.sum(axis=-1, keepdims=True)

    def probs(slot):
        p_s[slot], l_s[slot] = softmax2(s_s[slot])

    def output(pair, r, slot):
        vw = v_ref[pair, pl.ds(pl.multiple_of(win_start(r) * GRID_W, GRID_W), kwin), :]
        acc = jnp.dot(p_s[slot], jnp.concatenate([vw, vc_ref[pair]], axis=0), preferred_element_type=F32)
        o_ref[pair, pl.ds(pl.multiple_of(r * GRID_W, GRID_W), GRID_W), :] = (
            pick_heads(acc * l_s[slot]).astype(o_ref.dtype))

    def ctx_scores():
        qqc = stack_heads(qn_s[pl.ds(s_len, c_len), :])
        sc_s[...] = lax.dot_general(qqc, kc(), nt, preferred_element_type=F32)

    def ctx_probs():
        pc_s[...], lc_s[...] = softmax2(sc_s[...])

    def ctx_output(pair):
        acc = jnp.dot(pc_s[...], vc_ref[pair], preferred_element_type=F32)
        oc_ref[pair] = pick_heads(acc * lc_s[...]).astype(oc_ref.dtype)

    def step(pair, blk, par, out_of=None, probs_of=True, do_scores=True):
        if out_of is None:
            out_of = (pair, blk - 2)
        if out_of is not False:
            for a in range(rps):
                output(out_of[0], out_of[1] * rps + a, par * rps + a)
        if probs_of:
            for a in range(rps):
                probs((1 - par) * rps + a)
        if do_scores:
            for a in range(rps):
                scores(pair, blk * rps + a, par * rps + a)
            normalise_block(pair, jnp.minimum(blk + 2, n_blocks - 1))

    for pair in range(n_pairs):
        normalise(qc_ref, kc_ref, pair, 0, s_len, c_len)
        normalise_block(pair, 0)
        normalise_block(pair, 1)
        if pair == 0:
            step(pair, 0, 0, out_of=False, probs_of=False)
            ctx_scores()
            step(pair, 1, 1, out_of=False)
        else:
            step(pair, 0, 0, out_of=(pair - 1, n_blocks - 2))
            ctx_output(pair - 1)
            ctx_scores()
            step(pair, 1, 1, out_of=(pair - 1, n_blocks - 1))
        ctx_probs()

        def loop_step(j, carry, pair=pair):
            step(pair, 2 * j, 0)
            step(pair, 2 * j + 1, 1)
            return carry

        lax.fori_loop(1, n_blocks // 2, loop_step, 0)

    last = n_pairs - 1
    step(last, n_blocks, 0, do_scores=False)
    ctx_output(last)
    step(last, n_blocks + 1, 1, probs_of=False, do_scores=False)


def _attention(q, k, v, qc, kc, vc, gq2, gk2, bias_tab):
    b, n_pairs, s_len, pw = q.shape
    c_len = qc.shape[2]
    pps = ATTN_PAIRS_PER_STEP
    n_rows = s_len // GRID_W
    n_keys = WIN_ROWS * GRID_W + c_len
    assert n_rows >= WIN_ROWS and n_rows % (2 * ATTN_ROWS_PER_STEP) == 0 and n_pairs % pps == 0
    assert ATTN_ROWS_PER_STEP >= WIN_ROWS // 2
    lat = pl.BlockSpec((None, pps, s_len, pw), lambda bi, pi: (bi, pi, 0, 0))
    cx = pl.BlockSpec((None, pps, c_len, pw), lambda bi, pi: (bi, pi, 0, 0))
    gain = pl.BlockSpec((1, pw), lambda bi, pi: (0, 0))
    return pl.pallas_call(
        functools.partial(_attn_body, n_rows=n_rows),
        grid=(b, n_pairs // pps),
        in_specs=[lat, lat, lat, cx, cx, cx, gain, gain,
                  pl.BlockSpec((2 * pps, N_DR_TILES, GRID_W, 2 * GRID_W), lambda bi, pi: (pi, 0, 0, 0))],
        out_specs=[lat, cx],
        out_shape=[jax.ShapeDtypeStruct(q.shape, BF16), jax.ShapeDtypeStruct(qc.shape, BF16)],
        scratch_shapes=[pltpu.VMEM((s_len + c_len, pw), BF16)] * 2 + [
            pltpu.VMEM((2 * ATTN_ROWS_PER_STEP, 2 * GRID_W, n_keys), F32),
            pltpu.VMEM((2 * ATTN_ROWS_PER_STEP, 2 * GRID_W, n_keys), BF16),
            pltpu.VMEM((2 * ATTN_ROWS_PER_STEP, 2 * GRID_W, 1), F32),
            pltpu.VMEM((2 * c_len, c_len), F32), pltpu.VMEM((2 * c_len, c_len), BF16),
            pltpu.VMEM((2 * c_len, 1), F32)],
        compiler_params=_cparams(("parallel", "parallel")),
    )(q, k, v, qc, kc, vc, gq2, gk2, bias_tab)


def _proj_mlp_body(x_ref, o_ref, mod_ref, g_ref, wo_ref, w1_ref, w2_ref, out_ref):
    o = jnp.concatenate([o_ref[p] for p in range(o_ref.shape[0])], axis=1)
    y = jnp.dot(o, wo_ref[...], preferred_element_type=F32)
    x = x_ref[...] + mod_ref[2:3, :] * y
    h = _norm_mod(x, g_ref[...], mod_ref[3:4, :], mod_ref[4:5, :]).astype(BF16)
    out_ref[...] = x + mod_ref[5:6, :] * _sq_relu_mlp(h, w1_ref, w2_ref)


def _proj_mlp(x, o, mods4, layer, ctx_row, g, wo, w1, w2, tm):
    b, t, d = x.shape
    d_ff = w1.shape[-1]
    mod_idx = (lambda bi, ti: (layer, bi, 0, 0)) if ctx_row is None else (lambda bi, ti: (layer, ctx_row, 0, 0))
    tok = pl.BlockSpec((None, tm, d), lambda bi, ti: (bi, ti, 0))
    const = lambda bi, ti: (0, 0)
    this_layer = lambda bi, ti: (layer, 0, 0)
    o_spec, _ = _pair_major(b, t, d, tm)
    return pl.pallas_call(
        _proj_mlp_body,
        grid=(b, t // tm),
        in_specs=[tok, o_spec, pl.BlockSpec((None, None, 6, d), mod_idx), pl.BlockSpec((1, d), const),
                  _resident((d, d), const), _resident((None, d, d_ff), this_layer),
                  _resident((None, d_ff, d), this_layer)],
        out_specs=tok,
        out_shape=jax.ShapeDtypeStruct((b, t, d), F32),
        compiler_params=_cparams(("parallel", "parallel")),
    )(x, o, mods4, g, wo, w1, w2)


def _dft_tables(n, dg):
    l2 = FFT_L2
    l1 = n // l2
    kb = SUBLANES
    ar = np.arange
    a = 2.0 * np.pi * np.outer(ar(dg), ar(dg)) / dg
    norm = 1.0 / math.sqrt(n * dg)
    cd, sd = np.cos(a) * norm, np.sin(a) * norm
    al = 2.0 * np.pi * np.outer(ar(l1), ar(l1)) / l1
    c1, s1 = np.cos(al), np.sin(al)
    m1 = np.block([[c1, -s1], [-s1, -c1]])
    k = ar(l1)[:, None, None] + l1 * ar(l2)[None, :, None]
    be = 2.0 * np.pi * (k * ar(l2)[None, None, :] % n) / n
    m2 = np.concatenate([np.cos(be), np.sin(be)], axis=-1)
    m2 = m2.reshape(l1 // kb, kb, l2, 2, l2)
    big = np.zeros((l1 // kb, l2, kb, 2, l2, kb))
    for kk in range(kb):
        big[:, :, kk, :, :, kk] = m2[:, kk]
    m2 = big.reshape(l1 // kb, l2 * kb, 2 * l2 * kb)
    return (jnp.asarray(cd, F32), jnp.asarray(sd, F32), jnp.asarray(m1, F32), jnp.asarray(m2, F32))


def _fold_body(cd_ref, sd_ref, w_ref, o_ref):
    w = w_ref[...]
    hp = lax.Precision.HIGHEST
    o_ref[0] = jnp.dot(cd_ref[...], w, precision=hp, preferred_element_type=F32).astype(BF16)
    o_ref[1] = jnp.dot(sd_ref[...], w, precision=hp, preferred_element_type=F32).astype(BF16)


def _dft1_body(x_ref, mod_ref, g_ref, w_ref, m_ref, z_ref):
    l1, nb, d = x_ref.shape
    x = x_ref[...].reshape(l1 * nb, d)
    h = _norm_mod(x, g_ref[...], mod_ref[0:1, :], mod_ref[1:2, :]).astype(BF16)

    def project(part):
        p = jnp.dot(h, w_ref[part], preferred_element_type=F32)
        return jnp.swapaxes(p.reshape(l1, nb, d), 0, 1).reshape(nb * l1, d).astype(BF16)

    pc, ps = project(0), project(1)
    m1 = m_ref[...].astype(BF16)
    zs = []
    for m in range(nb):
        rows = slice(m * l1, (m + 1) * l1)
        pm = jnp.concatenate([pc[rows], ps[rows]], axis=0)
        zs.append(jnp.dot(m1, pm, preferred_element_type=F32))
    for kblk in range(l1 // nb):
        for part in range(2):
            r0 = part * l1 + kblk * nb
            z_ref[kblk, part] = jnp.concatenate([z[r0:r0 + nb] for z in zs], axis=0).astype(BF16)


def _fourier_stage1(x, mods4, layer, g, wf, m1):
    b, s_len, d = x.shape
    l1 = s_len // FFT_L2
    nb = SUBLANES
    assert (nb * nb) % (2 * SUBLANES) == 0
    return pl.pallas_call(
        _dft1_body,
        grid=(b, FFT_L2 // nb),
        in_specs=[
            pl.BlockSpec((None, l1, nb, d), lambda bi, ti: (bi, 0, ti, 0)),
            pl.BlockSpec((None, None, 6, d), lambda bi, ti: (layer, bi, 0, 0)),
            pl.BlockSpec((1, d), lambda bi, ti: (0, 0)),
            _resident((2, d, d), lambda bi, ti: (0, 0, 0)),
            pl.BlockSpec((2 * l1, 2 * l1), lambda bi, ti: (0, 0)),
        ],
        out_specs=pl.BlockSpec((None, l1 // nb, 2, nb * nb, d), lambda bi, ti: (bi, 0, 0, ti, 0)),
        out_shape=jax.ShapeDtypeStruct((b, l1 // nb, 2, FFT_L2 * nb, d), BF16),
        compiler_params=_cparams(("parallel", "parallel")),
    )(x.reshape(b, l1, FFT_L2, d), mods4, g, wf, m1)


def _dft2_mlp_body(z_ref, x_ref, m_ref, mod_ref, g_ref, w1_ref, w2_ref, out_ref):
    l2, kb, d = x_ref.shape
    n = l2 * kb
    y = jnp.dot(m_ref[...].astype(BF16), z_ref[...].reshape(2 * n, d), preferred_element_type=F32)
    x = x_ref[...].reshape(n, d) + mod_ref[2:3, :] * y
    h = _norm_mod(x, g_ref[...], mod_ref[3:4, :], mod_ref[4:5, :]).astype(BF16)
    out = x + mod_ref[5:6, :] * _sq_relu_mlp(h, w1_ref, w2_ref)
    out_ref[...] = out.reshape(l2, kb, d)


def _fourier_stage2_mlp(z, x, mods4, layer, g, m2, w1, w2):
    b, s_len, d = x.shape
    l1 = s_len // FFT_L2
    kb = SUBLANES
    d_ff = w1.shape[-1]
    xblk = pl.BlockSpec((None, FFT_L2, kb, d), lambda bi, ki: (bi, 0, ki, 0))
    const = lambda bi, ki: (0, 0)
    this_layer = lambda bi, ki: (layer, 0, 0)
    out = pl.pallas_call(
        _dft2_mlp_body,
        grid=(b, l1 // kb),
        in_specs=[
            pl.BlockSpec((None, None, 2, FFT_L2 * kb, d), lambda bi, ki: (bi, ki, 0, 0, 0)), xblk,
            pl.BlockSpec((None, FFT_L2 * kb, 2 * FFT_L2 * kb), lambda bi, ki: (ki, 0, 0)),
            pl.BlockSpec((None, None, 6, d), lambda bi, ki: (layer, bi, 0, 0)),
            pl.BlockSpec((1, d), const),
            _resident((None, d, d_ff), this_layer), _resident((None, d_ff, d), this_layer),
        ],
        out_specs=xblk,
        out_shape=jax.ShapeDtypeStruct((b, FFT_L2, l1, d), F32),
        compiler_params=_cparams(("parallel", "parallel")),
    )(z, x.reshape(b, FFT_L2, l1, d), m2, mods4, g, w1, w2)
    return out.reshape(b, s_len, d)


def kernel(x, c, ctx, c_ctx, ada_w, ada_b, norm_g, attn_wqkv, attn_wo, q_norm_g, k_norm_g, rpb,
           fourier_w, mlp_w1, mlp_w2):
    b, s_len, d = x.shape
    depth = ada_w.shape[0]
    hd = q_norm_g.shape[-1]
    assert depth == 2 and b + 1 <= SUBLANES and 2 * hd == LANES and d % LANES == 0
    assert s_len % (FFT_L2 * SUBLANES) == 0 and s_len % TOKEN_TILE == 0
    ctx_row = b

    assert rpb.shape[0] == 1 and fourier_w.shape[0] == 1
    cc = jnp.zeros((SUBLANES, d), F32).at[:b].set(c).at[b].set(c_ctx)
    cd, sd, m1, m2 = _dft_tables(s_len, d // FOURIER_GROUPS)
    mods, bias_tab, wf = _prepare_parameters(cc, ada_w, ada_b, rpb[0], cd, sd, fourier_w[0])
    mods4 = mods.reshape(depth, SUBLANES, 6, d)

    for layer in range(depth):
        is_attn = layer % N_MIXERS == 0
        idx = layer // N_MIXERS
        need_ctx = layer < depth - 1
        g_pre = norm_g[layer, 0][None]
        g_post = norm_g[layer, 1][None]
        if is_attn:
            assert layer == 0, "the first layer's QKV kernel also casts every layer's MLP weights"
            gq2 = jnp.tile(q_norm_g[idx], 2)[None]
            gk2 = jnp.tile(k_norm_g[idx], 2)[None]
            q, k, v, wqkv, wo, w1, w2 = _qkv_and_weight_casts(
                x, mods4, layer, g_pre, attn_wqkv[idx], attn_wo[idx], mlp_w1, mlp_w2, TOKEN_TILE)
            qc, kc, vc = _qkv(ctx, mods4, layer, ctx_row, g_pre, wqkv, ctx.shape[1])
            o, oc = _attention(q, k, v, qc, kc, vc, gq2, gk2, bias_tab)
            x = _proj_mlp(x, o, mods4, layer, None, g_post, wo, w1, w2, TOKEN_TILE)
            if need_ctx:
                ctx = _proj_mlp(ctx, oc, mods4, layer, ctx_row, g_post, wo, w1, w2, ctx.shape[1])
        else:
            assert not need_ctx, "a Fourier layer that still feeds a context stream is not supported"
            z = _fourier_stage1(x, mods4, layer, g_pre, wf, m1)
            x = _fourier_stage2_mlp(z, x, mods4, layer, g_post, m2, w1, w2)
    return x
```

```python
import functools
import math

import numpy as np
import jax
import jax.numpy as jnp
from jax import lax
from jax.experimental import pallas as pl
from jax.experimental.pallas import tpu as pltpu

F32 = jnp.float32
BF16 = jnp.bfloat16

GRID_W = 64
WIN_ROWS = 8
WIN_COLS = 16
FOURIER_GROUPS = 4
N_MIXERS = 2
EPS = 1e-6
NEG_INF = float("-inf")
LOG2E = math.log2(math.e)

SUBLANES = 8
LANES = 128
VMEM_LIMIT = 56 * 1024 * 1024

N_DR_TILES = 2 * WIN_ROWS - 2
DC_PAD = 32
MODS_COL_TILE = 1536
ATTN_ROWS_PER_STEP = 4
ATTN_PAIRS_PER_STEP = 2
FFT_L2 = 64
TOKEN_TILE = 1024
QKV_TOKEN_TILE = 512
FF_CHUNK = 1024


def _cparams(sem):
    return pltpu.CompilerParams(dimension_semantics=sem, vmem_limit_bytes=VMEM_LIMIT)


def _resident(shape, index_map):
    return pl.BlockSpec(shape, index_map, pipeline_mode=pl.Buffered(1))


def _mods_body(c_ref, w_ref, b_ref, o_ref):
    c = c_ref[...]
    s = (c * jax.nn.sigmoid(c)).astype(BF16)
    o_ref[...] = jnp.dot(s, w_ref[...].astype(BF16), preferred_element_type=F32) + b_ref[...]


def _prep_body(c_ref, w_ref, b_ref, r_ref, cd_ref, sd_ref, fw_ref, mods_ref, bias_ref, wf_ref, *, bias_tn):
    _mods_body(c_ref, w_ref, b_ref, mods_ref)
    _bias_chunk(r_ref, bias_ref, pl.program_id(0), bias_tn)
    _fold_body(cd_ref, sd_ref, fw_ref, wf_ref)


def _prepare_parameters(cc, ada_w, ada_b, rpb, cd, sd, fourier_w):
    n_layers, d, d6 = ada_w.shape
    h = rpb.shape[0]
    dg = cd.shape[0]
    rows = cc.shape[0]
    tn = MODS_COL_TILE
    nt = d6 // tn
    n_steps = n_layers * nt
    bias_tn = GRID_W * 2 * GRID_W // n_steps
    n_groups = d // dg
    per_group = n_steps // n_groups
    fcols = d // per_group
    assert bias_tn % (2 * GRID_W) == 0 and n_steps % n_groups == 0 and fcols % LANES == 0
    n_bias_rows = h * N_DR_TILES * GRID_W
    mods, bias, wf = pl.pallas_call(
        functools.partial(_prep_body, bias_tn=bias_tn),
        grid=(n_steps,),
        in_specs=[
            pl.BlockSpec((rows, d), lambda i: (0, 0)),
            pl.BlockSpec((None, d, tn), lambda i: (i // nt, 0, i % nt)),
            pl.BlockSpec((None, 1, tn), lambda i: (i // nt, 0, i % nt)),
            pl.BlockSpec((h * N_DR_TILES, 2 * DC_PAD), lambda i: (0, 0)),
            pl.BlockSpec((dg, dg), lambda i: (0, 0)), pl.BlockSpec((dg, dg), lambda i: (0, 0)),
            pl.BlockSpec((dg, fcols), lambda i: (i // per_group, i % per_group)),
        ],
        out_specs=[
            pl.BlockSpec((None, rows, tn), lambda i: (i // nt, 0, i % nt)),
            pl.BlockSpec((n_bias_rows, 2 * GRID_W), lambda i: (0, 0)),
            pl.BlockSpec((2, dg, fcols), lambda i: (0, i // per_group, i % per_group)),
        ],
        out_shape=[jax.ShapeDtypeStruct((n_layers, rows, d6), F32),
                   jax.ShapeDtypeStruct((n_bias_rows, 2 * GRID_W), F32),
                   jax.ShapeDtypeStruct((2, d, d), BF16)],
        compiler_params=_cparams(("arbitrary",)),
    )(cc, ada_w, ada_b.reshape(n_layers, 1, d6), _bias_operand(rpb), cd, sd, fourier_w)
    return mods, bias.reshape(h, N_DR_TILES, GRID_W, 2 * GRID_W), wf


def _norm_mod(x, g, shift, scale):
    ms = jnp.mean(x * x, axis=-1, keepdims=True)
    return (x * lax.rsqrt(ms + EPS) * g) * (1.0 + scale) + shift


def _sq_relu_mlp(h, w1_ref, w2_ref):
    acc = None
    for c in range(0, w1_ref.shape[1], FF_CHUNK):
        a = jnp.maximum(jnp.dot(h, w1_ref[:, c:c + FF_CHUNK], preferred_element_type=F32), 0.0)
        part = jnp.dot((a * a).astype(BF16), w2_ref[c:c + FF_CHUNK, :], preferred_element_type=F32)
        acc = part if acc is None else acc + part
    return acc


def _qkv_body(x_ref, mod_ref, g_ref, gq_ref, gk_ref, w_ref, q_ref, k_ref, v_ref):
    d = x_ref.shape[-1]
    hd = LANES // 2
    h = _norm_mod(x_ref[...], g_ref[...], mod_ref[0:1, :], mod_ref[1:2, :]).astype(BF16)
    q = jnp.dot(h, w_ref[:, 0:d], preferred_element_type=F32)
    k = jnp.dot(h, w_ref[:, d:2 * d], preferred_element_type=F32)
    v = jnp.dot(h, w_ref[:, 2 * d:3 * d], preferred_element_type=F32).astype(BF16)
    ri = lax.broadcasted_iota(jnp.int32, (2 * LANES, 2 * LANES), 0)
    ci = lax.broadcasted_iota(jnp.int32, (2 * LANES, 2 * LANES), 1)
    mean_blk = jnp.where(ri // hd == ci // hd, 1.0 / hd, 0.0).astype(BF16)
    gain = gq_ref[...] * gk_ref[...] * (hd ** -0.5 * LOG2E)
    for p in range(q_ref.shape[0]):
        lanes = slice(p * LANES, (p + 1) * LANES)
        qp, kp = q[:, lanes], k[:, lanes]
        ms = jnp.dot(jnp.concatenate([qp * qp, kp * kp], axis=1).astype(BF16), mean_blk,
                     preferred_element_type=F32)
        q_ref[p] = (qp * lax.rsqrt(ms[:, 0:LANES] + EPS) * gain).astype(BF16)
        k_ref[p] = (kp * lax.rsqrt(ms[:, LANES:2 * LANES] + EPS)).astype(BF16)
        v_ref[p] = v[:, lanes]


def _pair_major(b, t, d, tm):
    spec = pl.BlockSpec((None, d // LANES, tm, LANES), lambda bi, ti: (bi, 0, ti, 0))
    return spec, jax.ShapeDtypeStruct((b, d // LANES, t, LANES), BF16)


def _qkv(x, mods4, layer, ctx_row, g, gq2, gk2, w_bf16, tm):
    b, t, d = x.shape
    mod_idx = (lambda bi, ti: (layer, bi, 0, 0)) if ctx_row is None else (lambda bi, ti: (layer, ctx_row, 0, 0))
    tok = pl.BlockSpec((None, tm, d), lambda bi, ti: (bi, ti, 0))
    out_spec, out = _pair_major(b, t, d, tm)
    gain = pl.BlockSpec((1, LANES), lambda bi, ti: (0, 0))
    return pl.pallas_call(
        _qkv_body,
        grid=(b, t // tm),
        in_specs=[
            tok,
            pl.BlockSpec((None, None, 6, d), mod_idx),
            pl.BlockSpec((1, d), lambda bi, ti: (0, 0)),
            gain, gain,
            _resident((d, 3 * d), lambda bi, ti: (0, 0)),
        ],
        out_specs=[out_spec, out_spec, out_spec],
        out_shape=[out, out, out],
        compiler_params=_cparams(("parallel", "parallel")),
    )(x, mods4, g, gq2, gk2, w_bf16)


def _qkv_cast_body(x_ref, mod_ref, g_ref, gq_ref, gk_ref, wqkv_ref, wo_ref, w1_ref, w2_ref,
                   q_ref, k_ref, v_ref, wqkv_o, wo_o, w1_o, w2_o):
    @pl.when((pl.program_id(0) == 0) & (pl.program_id(1) == 0))
    def _():
        wqkv_o[...] = wqkv_ref[...].astype(BF16)

    wo_o[...] = wo_ref[...].astype(BF16)
    w1_o[...] = w1_ref[...].astype(BF16)
    w2_o[...] = w2_ref[...].astype(BF16)
    _qkv_body(x_ref, mod_ref, g_ref, gq_ref, gk_ref, wqkv_o, q_ref, k_ref, v_ref)


def _qkv_and_weight_casts(x, mods4, layer, g, gq2, gk2, wqkv, wo, mlp_w1, mlp_w2, tm):
    b, t, d = x.shape
    nt = t // tm
    n_steps = b * nt
    n_layers, _, d_ff = mlp_w1.shape
    assert d % (2 * SUBLANES * n_steps) == 0
    step = lambda bi, ti: bi * nt + ti
    tok = pl.BlockSpec((None, tm, d), lambda bi, ti: (bi, ti, 0))
    qkv_spec, qkv_out = _pair_major(b, t, d, tm)
    wo_blk = pl.BlockSpec((d // n_steps, d), lambda bi, ti: (step(bi, ti), 0))
    w1_blk = pl.BlockSpec((n_layers, d // n_steps, d_ff), lambda bi, ti: (0, step(bi, ti), 0))
    w2_blk = pl.BlockSpec((n_layers, d_ff // n_steps, d), lambda bi, ti: (0, step(bi, ti), 0))
    gain = pl.BlockSpec((1, LANES), lambda bi, ti: (0, 0))
    return pl.pallas_call(
        _qkv_cast_body,
        grid=(b, nt),
        in_specs=[
            tok,
            pl.BlockSpec((None, None, 6, d), lambda bi, ti: (layer, bi, 0, 0)),
            pl.BlockSpec((1, d), lambda bi, ti: (0, 0)),
            gain, gain,
            _resident((d, 3 * d), lambda bi, ti: (0, 0)),
            wo_blk, w1_blk, w2_blk,
        ],
        out_specs=[qkv_spec, qkv_spec, qkv_spec,
                   pl.BlockSpec((d, 3 * d), lambda bi, ti: (0, 0)), wo_blk, w1_blk, w2_blk],
        out_shape=[qkv_out, qkv_out, qkv_out,
                   jax.ShapeDtypeStruct((d, 3 * d), BF16), jax.ShapeDtypeStruct((d, d), BF16),
                   jax.ShapeDtypeStruct(mlp_w1.shape, BF16), jax.ShapeDtypeStruct(mlp_w2.shape, BF16)],
        compiler_params=_cparams(("arbitrary", "arbitrary")),
    )(x, mods4, g, gq2, gk2, wqkv, wo, mlp_w1, mlp_w2)


def _bias_chunk(r_ref, o_ref, chunk, tn):
    col0 = chunk * tn
    e = col0 + lax.broadcasted_iota(jnp.int32, (2 * DC_PAD, tn), 1)
    r = lax.broadcasted_iota(jnp.int32, (2 * DC_PAD, tn), 0)
    cq = e // (2 * GRID_W)
    half = (e // GRID_W) % 2
    ck = e % GRID_W
    sel = ((r // DC_PAD == half) & (r % DC_PAD == ck - cq + WIN_COLS - 1)).astype(BF16)
    rv = r_ref[...]
    hi = rv.astype(BF16)
    rem = rv - hi.astype(F32)
    mid = rem.astype(BF16)
    lo = (rem - mid.astype(F32)).astype(BF16)
    acc = jnp.dot(hi, sel, preferred_element_type=F32)
    acc += jnp.dot(mid, sel, preferred_element_type=F32)
    acc += jnp.dot(lo, sel, preferred_element_type=F32)
    e1 = col0 + lax.broadcasted_iota(jnp.int32, (1, tn), 1)
    cq1 = e1 // (2 * GRID_W)
    ck1 = e1 % GRID_W
    start = jnp.clip(cq1 - WIN_COLS // 2, 0, GRID_W - WIN_COLS)
    ok = (ck1 >= start) & (ck1 < start + WIN_COLS)
    vals = acc * LOG2E + jnp.where(ok, 0.0, NEG_INF)
    n_tables = r_ref.shape[0]
    w = 2 * GRID_W
    for c in range(tn // w):
        o_ref[pl.ds(chunk * (tn // w) + c, n_tables, stride=GRID_W), :] = vals[:, c * w:(c + 1) * w]


def _bias_operand(rpb):
    h, nr, nc = rpb.shape
    assert nr == 2 * WIN_ROWS - 1 and nc == 2 * WIN_COLS - 1 <= DC_PAD
    rp = jnp.pad(rpb, ((0, 0), (0, 0), (0, DC_PAD - nc)))
    r2 = jnp.concatenate([rp[:, 0:N_DR_TILES], rp[:, 1:N_DR_TILES + 1]], axis=-1)
    return r2.reshape(h * N_DR_TILES, 2 * DC_PAD)


def _attn_body(q_ref, k_ref, v_ref, qc_ref, kc_ref, vc_ref, bias_ref, o_ref, oc_ref,
               s_s, p_s, l_s, sc_s, pc_s, lc_s, *, n_rows):
    n_pairs, s_len, pw = q_ref.shape
    c_len = qc_ref.shape[1]
    hd = pw // 2
    kwin = WIN_ROWS * GRID_W
    nt = (((1,), (1,)), ((), ()))
    rps = ATTN_ROWS_PER_STEP
    n_blocks = n_rows // rps

    lane = lax.broadcasted_iota(jnp.int32, (1, 2 * hd), 1)
    first = lane < hd

    def stack_heads(q):
        zero = jnp.zeros_like(q)
        return jnp.concatenate([jnp.where(first, q, zero), jnp.where(first, zero, q)], axis=0)

    def pick_heads(acc):
        n = acc.shape[0] // 2
        return jnp.where(first, acc[:n], acc[n:])

    def win_start(r):
        return jnp.clip(r - WIN_ROWS // 2, 0, n_rows - WIN_ROWS)

    def scores(pair, r, slot):
        rs = win_start(r)
        qq = stack_heads(q_ref[pair, pl.ds(pl.multiple_of(r * GRID_W, GRID_W), GRID_W), :])
        kw = k_ref[pair, pl.ds(pl.multiple_of(rs * GRID_W, GRID_W), kwin), :]
        t0 = rs - r + WIN_ROWS - 1
        bias = jnp.concatenate(
            [jnp.concatenate([bias_ref[2 * pair + hh, t0 + 2 * ii] for ii in range(WIN_ROWS // 2)], axis=1)
             for hh in range(2)], axis=0)
        s_s[slot, :, 0:kwin] = lax.dot_general(qq, kw, nt, preferred_element_type=F32) + bias
        s_s[slot, :, kwin:kwin + c_len] = lax.dot_general(qq, kc_ref[pair], nt, preferred_element_type=F32)

    def softmax2(s):
        p = jnp.exp2(s - s.max(axis=-1, keepdims=True))
        return p.astype(BF16), 1.0 / p.sum(axis=-1, keepdims=True)

    def probs(slot):
        p_s[slot], l_s[slot] = softmax2(s_s[slot])

    def output(pair, r, slot):
        vw = v_ref[pair, pl.ds(pl.multiple_of(win_start(r) * GRID_W, GRID_W), kwin), :]
        acc = jnp.dot(p_s[slot], jnp.concatenate([vw, vc_ref[pair]], axis=0), preferred_element_type=F32)
        o_ref[pair, pl.ds(pl.multiple_of(r * GRID_W, GRID_W), GRID_W), :] = (
            pick_heads(acc * l_s[slot]).astype(o_ref.dtype))

    def ctx_scores(pair):
        sc_s[...] = lax.dot_general(stack_heads(qc_ref[pair]), kc_ref[pair], nt, preferred_element_type=F32)

    def ctx_probs():
        pc_s[...], lc_s[...] = softmax2(sc_s[...])

    def ctx_output(pair):
        acc = jnp.dot(pc_s[...], vc_ref[pair], preferred_element_type=F32)
        oc_ref[pair] = pick_heads(acc * lc_s[...]).astype(oc_ref.dtype)

    def step(pair, blk, par, out_of=None, probs_of=True, do_scores=True):
        if out_of is None:
            out_of = (pair, blk - 2)
        if out_of is not False:
            for a in range(rps):
                output(out_of[0], out_of[1] * rps + a, par * rps + a)
        if probs_of:
            for a in range(rps):
                probs((1 - par) * rps + a)
        if do_scores:
            for a in range(rps):
                scores(pair, blk * rps + a, par * rps + a)

    for pair in range(n_pairs):
        if pair == 0:
            step(pair, 0, 0, out_of=False, probs_of=False)
            ctx_scores(pair)
            step(pair, 1, 1, out_of=False)
        else:
            step(pair, 0, 0, out_of=(pair - 1, n_blocks - 2))
            ctx_output(pair - 1)
            ctx_scores(pair)
            step(pair, 1, 1, out_of=(pair - 1, n_blocks - 1))
        ctx_probs()

        def loop_step(j, carry, pair=pair):
            step(pair, 2 * j, 0)
            step(pair, 2 * j + 1, 1)
            return carry

        lax.fori_loop(1, n_blocks // 2, loop_step, 0)

    last = n_pairs - 1
    step(last, n_blocks, 0, do_scores=False)
    ctx_output(last)
    step(last, n_blocks + 1, 1, probs_of=False, do_scores=False)


def _attention(q, k, v, qc, kc, vc, bias_tab):
    b, n_pairs, s_len, pw = q.shape
    c_len = qc.shape[2]
    pps = ATTN_PAIRS_PER_STEP
    n_rows = s_len // GRID_W
    n_keys = WIN_ROWS * GRID_W + c_len
    assert n_rows >= WIN_ROWS and n_rows % (2 * ATTN_ROWS_PER_STEP) == 0 and n_pairs % pps == 0
    lat = pl.BlockSpec((None, pps, s_len, pw), lambda bi, pi: (bi, pi, 0, 0))
    cx = pl.BlockSpec((None, pps, c_len, pw), lambda bi, pi: (bi, pi, 0, 0))
    return pl.pallas_call(
        functools.partial(_attn_body, n_rows=n_rows),
        grid=(b, n_pairs // pps),
        in_specs=[lat, lat, lat, cx, cx, cx,
                  pl.BlockSpec((2 * pps, N_DR_TILES, GRID_W, 2 * GRID_W), lambda bi, pi: (pi, 0, 0, 0))],
        out_specs=[lat, cx],
        out_shape=[jax.ShapeDtypeStruct(q.shape, BF16), jax.ShapeDtypeStruct(qc.shape, BF16)],
        scratch_shapes=[
            pltpu.VMEM((2 * ATTN_ROWS_PER_STEP, 2 * GRID_W, n_keys), F32),
            pltpu.VMEM((2 * ATTN_ROWS_PER_STEP, 2 * GRID_W, n_keys), BF16),
            pltpu.VMEM((2 * ATTN_ROWS_PER_STEP, 2 * GRID_W, 1), F32),
            pltpu.VMEM((2 * c_len, c_len), F32), pltpu.VMEM((2 * c_len, c_len), BF16),
            pltpu.VMEM((2 * c_len, 1), F32)],
        compiler_params=_cparams(("parallel", "parallel")),
    )(q, k, v, qc, kc, vc, bias_tab)


def _proj_mlp_body(x_ref, o_ref, mod_ref, g_ref, wo_ref, w1_ref, w2_ref, out_ref):
    o = jnp.concatenate([o_ref[p] for p in range(o_ref.shape[0])], axis=1)
    y = jnp.dot(o, wo_ref[...], preferred_element_type=F32)
    x = x_ref[...] + mod_ref[2:3, :] * y
    h = _norm_mod(x, g_ref[...], mod_ref[3:4, :], mod_ref[4:5, :]).astype(BF16)
    out_ref[...] = x + mod_ref[5:6, :] * _sq_relu_mlp(h, w1_ref, w2_ref)


def _proj_mlp(x, o, mods4, layer, ctx_row, g, wo, w1, w2, tm):
    b, t, d = x.shape
    d_ff = w1.shape[-1]
    mod_idx = (lambda bi, ti: (layer, bi, 0, 0)) if ctx_row is None else (lambda bi, ti: (layer, ctx_row, 0, 0))
    tok = pl.BlockSpec((None, tm, d), lambda bi, ti: (bi, ti, 0))
    const = lambda bi, ti: (0, 0)
    this_layer = lambda bi, ti: (layer, 0, 0)
    o_spec, _ = _pair_major(b, t, d, tm)
    return pl.pallas_call(
        _proj_mlp_body,
        grid=(b, t // tm),
        in_specs=[tok, o_spec, pl.BlockSpec((None, None, 6, d), mod_idx), pl.BlockSpec((1, d), const),
                  _resident((d, d), const), _resident((None, d, d_ff), this_layer),
                  _resident((None, d_ff, d), this_layer)],
        out_specs=tok,
        out_shape=jax.ShapeDtypeStruct((b, t, d), F32),
        compiler_params=_cparams(("parallel", "parallel")),
    )(x, o, mods4, g, wo, w1, w2)


def _dft_tables(n, dg):
    l2 = FFT_L2
    l1 = n // l2
    kb = SUBLANES
    ar = np.arange
    a = 2.0 * np.pi * np.outer(ar(dg), ar(dg)) / dg
    norm = 1.0 / math.sqrt(n * dg)
    cd, sd = np.cos(a) * norm, np.sin(a) * norm
    al = 2.0 * np.pi * np.outer(ar(l1), ar(l1)) / l1
    c1, s1 = np.cos(al), np.sin(al)
    m1 = np.block([[c1, -s1], [-s1, -c1]])
    k = ar(l1)[:, None, None] + l1 * ar(l2)[None, :, None]
    be = 2.0 * np.pi * (k * ar(l2)[None, None, :] % n) / n
    m2 = np.concatenate([np.cos(be), np.sin(be)], axis=-1)
    m2 = m2.reshape(l1 // kb, kb, l2, 2, l2)
    big = np.zeros((l1 // kb, l2, kb, 2, l2, kb))
    for kk in range(kb):
        big[:, :, kk, :, :, kk] = m2[:, kk]
    m2 = big.reshape(l1 // kb, l2 * kb, 2 * l2 * kb)
    return (jnp.asarray(cd, F32), jnp.asarray(sd, F32), jnp.asarray(m1, F32), jnp.asarray(m2, F32))


def _fold_body(cd_ref, sd_ref, w_ref, o_ref):
    w = w_ref[...]
    hp = lax.Precision.HIGHEST
    o_ref[0] = jnp.dot(cd_ref[...], w, precision=hp, preferred_element_type=F32).astype(BF16)
    o_ref[1] = jnp.dot(sd_ref[...], w, precision=hp, preferred_element_type=F32).astype(BF16)


def _dft1_body(x_ref, mod_ref, g_ref, w_ref, m_ref, z_ref):
    l1, nb, d = x_ref.shape
    x = x_ref[...].reshape(l1 * nb, d)
    h = _norm_mod(x, g_ref[...], mod_ref[0:1, :], mod_ref[1:2, :]).astype(BF16)

    def project(part):
        p = jnp.dot(h, w_ref[part], preferred_element_type=F32)
        return jnp.swapaxes(p.reshape(l1, nb, d), 0, 1).reshape(nb * l1, d).astype(BF16)

    pc, ps = project(0), project(1)
    m1 = m_ref[...].astype(BF16)
    zs = []
    for m in range(nb):
        rows = slice(m * l1, (m + 1) * l1)
        pm = jnp.concatenate([pc[rows], ps[rows]], axis=0)
        zs.append(jnp.dot(m1, pm, preferred_element_type=F32))
    for kblk in range(l1 // nb):
        for part in range(2):
            r0 = part * l1 + kblk * nb
            z_ref[kblk, part] = jnp.concatenate([z[r0:r0 + nb] for z in zs], axis=0).astype(BF16)


def _fourier_stage1(x, mods4, layer, g, wf, m1):
    b, s_len, d = x.shape
    l1 = s_len // FFT_L2
    nb = SUBLANES
    assert (nb * nb) % (2 * SUBLANES) == 0
    return pl.pallas_call(
        _dft1_body,
        grid=(b, FFT_L2 // nb),
        in_specs=[
            pl.BlockSpec((None, l1, nb, d), lambda bi, ti: (bi, 0, ti, 0)),
            pl.BlockSpec((None, None, 6, d), lambda bi, ti: (layer, bi, 0, 0)),
            pl.BlockSpec((1, d), lambda bi, ti: (0, 0)),
            _resident((2, d, d), lambda bi, ti: (0, 0, 0)),
            pl.BlockSpec((2 * l1, 2 * l1), lambda bi, ti: (0, 0)),
        ],
        out_specs=pl.BlockSpec((None, l1 // nb, 2, nb * nb, d), lambda bi, ti: (bi, 0, 0, ti, 0)),
        out_shape=jax.ShapeDtypeStruct((b, l1 // nb, 2, FFT_L2 * nb, d), BF16),
        compiler_params=_cparams(("parallel", "parallel")),
    )(x.reshape(b, l1, FFT_L2, d), mods4, g, wf, m1)


def _dft2_mlp_body(z_ref, x_ref, m_ref, mod_ref, g_ref, w1_ref, w2_ref, out_ref):
    l2, kb, d = x_ref.shape
    n = l2 * kb
    y = jnp.dot(m_ref[...].astype(BF16), z_ref[...].reshape(2 * n, d), preferred_element_type=F32)
    x = x_ref[...].reshape(n, d) + mod_ref[2:3, :] * y
    h = _norm_mod(x, g_ref[...], mod_ref[3:4, :], mod_ref[4:5, :]).astype(BF16)
    out = x + mod_ref[5:6, :] * _sq_relu_mlp(h, w1_ref, w2_ref)
    out_ref[...] = out.reshape(l2, kb, d)


def _fourier_stage2_mlp(z, x, mods4, layer, g, m2, w1, w2):
    b, s_len, d = x.shape
    l1 = s_len // FFT_L2
    kb = SUBLANES
    d_ff = w1.shape[-1]
    xblk = pl.BlockSpec((None, FFT_L2, kb, d), lambda bi, ki: (bi, 0, ki, 0))
    const = lambda bi, ki: (0, 0)
    this_layer = lambda bi, ki: (layer, 0, 0)
    out = pl.pallas_call(
        _dft2_mlp_body,
        grid=(b, l1 // kb),
        in_specs=[
            pl.BlockSpec((None, None, 2, FFT_L2 * kb, d), lambda bi, ki: (bi, ki, 0, 0, 0)), xblk,
            pl.BlockSpec((None, FFT_L2 * kb, 2 * FFT_L2 * kb), lambda bi, ki: (ki, 0, 0)),
            pl.BlockSpec((None, None, 6, d), lambda bi, ki: (layer, bi, 0, 0)),
            pl.BlockSpec((1, d), const),
            _resident((None, d, d_ff), this_layer), _resident((None, d_ff, d), this_layer),
        ],
        out_specs=xblk,
        out_shape=jax.ShapeDtypeStruct((b, FFT_L2, l1, d), F32),
        compiler_params=_cparams(("parallel", "parallel")),
    )(z, x.reshape(b, FFT_L2, l1, d), m2, mods4, g, w1, w2)
    return out.reshape(b, s_len, d)


def kernel(x, c, ctx, c_ctx, ada_w, ada_b, norm_g, attn_wqkv, attn_wo, q_norm_g, k_norm_g, rpb,
           fourier_w, mlp_w1, mlp_w2):
    b, s_len, d = x.shape
    depth = ada_w.shape[0]
    hd = q_norm_g.shape[-1]
    assert depth == 2 and b + 1 <= SUBLANES and 2 * hd == LANES and d % LANES == 0
    assert s_len % (FFT_L2 * SUBLANES) == 0 and s_len % TOKEN_TILE == 0 and s_len % QKV_TOKEN_TILE == 0
    ctx_row = b

    assert rpb.shape[0] == 1 and fourier_w.shape[0] == 1
    cc = jnp.zeros((SUBLANES, d), F32).at[:b].set(c).at[b].set(c_ctx)
    cd, sd, m1, m2 = _dft_tables(s_len, d // FOURIER_GROUPS)
    mods, bias_tab, wf = _prepare_parameters(cc, ada_w, ada_b, rpb[0], cd, sd, fourier_w[0])
    mods4 = mods.reshape(depth, SUBLANES, 6, d)

    for layer in range(depth):
        is_attn = layer % N_MIXERS == 0
        idx = layer // N_MIXERS
        need_ctx = layer < depth - 1
        g_pre = norm_g[layer, 0][None]
        g_post = norm_g[layer, 1][None]
        if is_attn:
            assert layer == 0, "the first layer's QKV kernel also casts every layer's MLP weights"
            gq2 = jnp.tile(q_norm_g[idx], 2)[None]
            gk2 = jnp.tile(k_norm_g[idx], 2)[None]
            q, k, v, wqkv, wo, w1, w2 = _qkv_and_weight_casts(
                x, mods4, layer, g_pre, gq2, gk2, attn_wqkv[idx], attn_wo[idx], mlp_w1, mlp_w2, QKV_TOKEN_TILE)
            qc, kc, vc = _qkv(ctx, mods4, layer, ctx_row, g_pre, gq2, gk2, wqkv, ctx.shape[1])
            o, oc = _attention(q, k, v, qc, kc, vc, bias_tab)
            x = _proj_mlp(x, o, mods4, layer, None, g_post, wo, w1, w2, TOKEN_TILE)
            if need_ctx:
                ctx = _proj_mlp(ctx, oc, mods4, layer, ctx_row, g_post, wo, w1, w2, ctx.shape[1])
        else:
            assert not need_ctx, "a Fourier layer that still feeds a context stream is not supported"
            z = _fourier_stage1(x, mods4, layer, g_pre, wf, m1)
            x = _fourier_stage2_mlp(z, x, mods4, layer, g_post, m2, w1, w2)
    return x
```

```python
import functools
import math

import numpy as np
import jax
import jax.numpy as jnp
from jax import lax
from jax.experimental import pallas as pl
from jax.experimental.pallas import tpu as pltpu

F32 = jnp.float32
BF16 = jnp.bfloat16

GRID_W = 64
WIN_ROWS = 8
WIN_COLS = 16
FOURIER_GROUPS = 4
N_MIXERS = 2
EPS = 1e-6
NEG_INF = float("-inf")
LOG2E = math.log2(math.e)

SUBLANES = 8
LANES = 128
VMEM_LIMIT = 56 * 1024 * 1024

N_DR_TILES = 2 * WIN_ROWS - 2
DC_PAD = 32
MODS_COL_TILE = 1536
ATTN_ROWS_PER_STEP = 4
ATTN_PAIRS_PER_STEP = 2
FFT_L2 = 64
TOKEN_TILE = 1024
FF_CHUNK = 1024


def _cparams(sem):
    return pltpu.CompilerParams(dimension_semantics=sem, vmem_limit_bytes=VMEM_LIMIT)


def _resident(shape, index_map):
    return pl.BlockSpec(shape, index_map, pipeline_mode=pl.Buffered(1))


def _mods_body(c_ref, w_ref, b_ref, o_ref):
    c = c_ref[...]
    s = (c * jax.nn.sigmoid(c)).astype(BF16)
    o_ref[...] = jnp.dot(s, w_ref[...].astype(BF16), preferred_element_type=F32) + b_ref[...]


def _prep_body(c_ref, w_ref, b_ref, r_ref, cd_ref, sd_ref, fw_ref, mods_ref, bias_ref, wf_ref, *, bias_tn):
    _mods_body(c_ref, w_ref, b_ref, mods_ref)
    _bias_chunk(r_ref, bias_ref, pl.program_id(0), bias_tn)
    _fold_body(cd_ref, sd_ref, fw_ref, wf_ref)


def _prepare_parameters(cc, ada_w, ada_b, rpb, cd, sd, fourier_w):
    n_layers, d, d6 = ada_w.shape
    h = rpb.shape[0]
    dg = cd.shape[0]
    rows = cc.shape[0]
    tn = MODS_COL_TILE
    nt = d6 // tn
    n_steps = n_layers * nt
    bias_tn = GRID_W * 2 * GRID_W // n_steps
    n_groups = d // dg
    per_group = n_steps // n_groups
    fcols = d // per_group
    assert bias_tn % (2 * GRID_W) == 0 and n_steps % n_groups == 0 and fcols % LANES == 0
    n_bias_rows = h * N_DR_TILES * GRID_W
    mods, bias, wf = pl.pallas_call(
        functools.partial(_prep_body, bias_tn=bias_tn),
        grid=(n_steps,),
        in_specs=[
            pl.BlockSpec((rows, d), lambda i: (0, 0)),
            pl.BlockSpec((None, d, tn), lambda i: (i // nt, 0, i % nt)),
            pl.BlockSpec((None, 1, tn), lambda i: (i // nt, 0, i % nt)),
            pl.BlockSpec((h * N_DR_TILES, 2 * DC_PAD), lambda i: (0, 0)),
            pl.BlockSpec((dg, dg), lambda i: (0, 0)), pl.BlockSpec((dg, dg), lambda i: (0, 0)),
            pl.BlockSpec((dg, fcols), lambda i: (i // per_group, i % per_group)),
        ],
        out_specs=[
            pl.BlockSpec((None, rows, tn), lambda i: (i // nt, 0, i % nt)),
            pl.BlockSpec((n_bias_rows, 2 * GRID_W), lambda i: (0, 0)),
            pl.BlockSpec((2, dg, fcols), lambda i: (0, i // per_group, i % per_group)),
        ],
        out_shape=[jax.ShapeDtypeStruct((n_layers, rows, d6), F32),
                   jax.ShapeDtypeStruct((n_bias_rows, 2 * GRID_W), F32),
                   jax.ShapeDtypeStruct((2, d, d), BF16)],
        compiler_params=_cparams(("arbitrary",)),
    )(cc, ada_w, ada_b.reshape(n_layers, 1, d6), _bias_operand(rpb), cd, sd, fourier_w)
    return mods, bias.reshape(h, N_DR_TILES, GRID_W, 2 * GRID_W), wf


def _norm_mod(x, g, shift, scale):
    ms = jnp.mean(x * x, axis=-1, keepdims=True)
    return (x * lax.rsqrt(ms + EPS) * g) * (1.0 + scale) + shift


def _sq_relu_mlp(h, w1_ref, w2_ref):
    acc = None
    for c in range(0, w1_ref.shape[1], FF_CHUNK):
        a = jnp.maximum(jnp.dot(h, w1_ref[:, c:c + FF_CHUNK], preferred_element_type=F32), 0.0)
        part = jnp.dot((a * a).astype(BF16), w2_ref[c:c + FF_CHUNK, :], preferred_element_type=F32)
        acc = part if acc is None else acc + part
    return acc


def _qkv_body(x_ref, mod_ref, g_ref, gq_ref, gk_ref, w_ref, q_ref, k_ref, v_ref):
    d = x_ref.shape[-1]
    hd = LANES // 2
    h = _norm_mod(x_ref[...], g_ref[...], mod_ref[0:1, :], mod_ref[1:2, :]).astype(BF16)
    first = lax.broadcasted_iota(jnp.int32, (1, LANES), 1) < hd
    gains = (gq_ref[...] * gk_ref[...] * (hd ** -0.5 * LOG2E), None, None)
    for i, out_ref in enumerate((q_ref, k_ref, v_ref)):
        y = jnp.dot(h, w_ref[:, i * d:(i + 1) * d], preferred_element_type=F32)
        for p in range(out_ref.shape[0]):
            yp = y[:, p * LANES:(p + 1) * LANES]
            if out_ref is not v_ref:
                sq = yp * yp
                lo = jnp.sum(jnp.where(first, sq, 0.0), axis=-1, keepdims=True)
                hi = jnp.sum(jnp.where(first, 0.0, sq), axis=-1, keepdims=True)
                yp = yp * lax.rsqrt(jnp.where(first, lo, hi) * (1.0 / hd) + EPS)
                if gains[i] is not None:
                    yp = yp * gains[i]
            out_ref[p] = yp.astype(BF16)


def _pair_major(b, t, d, tm):
    spec = pl.BlockSpec((None, d // LANES, tm, LANES), lambda bi, ti: (bi, 0, ti, 0))
    return spec, jax.ShapeDtypeStruct((b, d // LANES, t, LANES), BF16)


def _qkv(x, mods4, layer, ctx_row, g, gq2, gk2, w_bf16, tm):
    b, t, d = x.shape
    mod_idx = (lambda bi, ti: (layer, bi, 0, 0)) if ctx_row is None else (lambda bi, ti: (layer, ctx_row, 0, 0))
    tok = pl.BlockSpec((None, tm, d), lambda bi, ti: (bi, ti, 0))
    out_spec, out = _pair_major(b, t, d, tm)
    gain = pl.BlockSpec((1, LANES), lambda bi, ti: (0, 0))
    return pl.pallas_call(
        _qkv_body,
        grid=(b, t // tm),
        in_specs=[
            tok,
            pl.BlockSpec((None, None, 6, d), mod_idx),
            pl.BlockSpec((1, d), lambda bi, ti: (0, 0)),
            gain, gain,
            _resident((d, 3 * d), lambda bi, ti: (0, 0)),
        ],
        out_specs=[out_spec, out_spec, out_spec],
        out_shape=[out, out, out],
        compiler_params=_cparams(("parallel", "parallel")),
    )(x, mods4, g, gq2, gk2, w_bf16)


def _qkv_cast_body(x_ref, mod_ref, g_ref, gq_ref, gk_ref, wqkv_ref, wo_ref, w1_ref, w2_ref,
                   q_ref, k_ref, v_ref, wqkv_o, wo_o, w1_o, w2_o):
    @pl.when((pl.program_id(0) == 0) & (pl.program_id(1) == 0))
    def _():
        wqkv_o[...] = wqkv_ref[...].astype(BF16)

    wo_o[...] = wo_ref[...].astype(BF16)
    w1_o[...] = w1_ref[...].astype(BF16)
    w2_o[...] = w2_ref[...].astype(BF16)
    _qkv_body(x_ref, mod_ref, g_ref, gq_ref, gk_ref, wqkv_o, q_ref, k_ref, v_ref)


def _qkv_and_weight_casts(x, mods4, layer, g, gq2, gk2, wqkv, wo, mlp_w1, mlp_w2, tm):
    b, t, d = x.shape
    nt = t // tm
    n_steps = b * nt
    n_layers, _, d_ff = mlp_w1.shape
    assert d % (2 * SUBLANES * n_steps) == 0
    step = lambda bi, ti: bi * nt + ti
    tok = pl.BlockSpec((None, tm, d), lambda bi, ti: (bi, ti, 0))
    qkv_spec, qkv_out = _pair_major(b, t, d, tm)
    wo_blk = pl.BlockSpec((d // n_steps, d), lambda bi, ti: (step(bi, ti), 0))
    w1_blk = pl.BlockSpec((n_layers, d // n_steps, d_ff), lambda bi, ti: (0, step(bi, ti), 0))
    w2_blk = pl.BlockSpec((n_layers, d_ff // n_steps, d), lambda bi, ti: (0, step(bi, ti), 0))
    gain = pl.BlockSpec((1, LANES), lambda bi, ti: (0, 0))
    return pl.pallas_call(
        _qkv_cast_body,
        grid=(b, nt),
        in_specs=[
            tok,
            pl.BlockSpec((None, None, 6, d), lambda bi, ti: (layer, bi, 0, 0)),
            pl.BlockSpec((1, d), lambda bi, ti: (0, 0)),
            gain, gain,
            _resident((d, 3 * d), lambda bi, ti: (0, 0)),
            wo_blk, w1_blk, w2_blk,
        ],
        out_specs=[qkv_spec, qkv_spec, qkv_spec,
                   pl.BlockSpec((d, 3 * d), lambda bi, ti: (0, 0)), wo_blk, w1_blk, w2_blk],
        out_shape=[qkv_out, qkv_out, qkv_out,
                   jax.ShapeDtypeStruct((d, 3 * d), BF16), jax.ShapeDtypeStruct((d, d), BF16),
                   jax.ShapeDtypeStruct(mlp_w1.shape, BF16), jax.ShapeDtypeStruct(mlp_w2.shape, BF16)],
        compiler_params=_cparams(("arbitrary", "arbitrary")),
    )(x, mods4, g, gq2, gk2, wqkv, wo, mlp_w1, mlp_w2)


def _bias_chunk(r_ref, o_ref, chunk, tn):
    col0 = chunk * tn
    e = col0 + lax.broadcasted_iota(jnp.int32, (2 * DC_PAD, tn), 1)
    r = lax.broadcasted_iota(jnp.int32, (2 * DC_PAD, tn), 0)
    cq = e // (2 * GRID_W)
    half = (e // GRID_W) % 2
    ck = e % GRID_W
    sel = ((r // DC_PAD == half) & (r % DC_PAD == ck - cq + WIN_COLS - 1)).astype(BF16)
    rv = r_ref[...]
    hi = rv.astype(BF16)
    rem = rv - hi.astype(F32)
    mid = rem.astype(BF16)
    lo = (rem - mid.astype(F32)).astype(BF16)
    acc = jnp.dot(hi, sel, preferred_element_type=F32)
    acc += jnp.dot(mid, sel, preferred_element_type=F32)
    acc += jnp.dot(lo, sel, preferred_element_type=F32)
    e1 = col0 + lax.broadcasted_iota(jnp.int32, (1, tn), 1)
    cq1 = e1 // (2 * GRID_W)
    ck1 = e1 % GRID_W
    start = jnp.clip(cq1 - WIN_COLS // 2, 0, GRID_W - WIN_COLS)
    ok = (ck1 >= start) & (ck1 < start + WIN_COLS)
    vals = acc * LOG2E + jnp.where(ok, 0.0, NEG_INF)
    n_tables = r_ref.shape[0]
    w = 2 * GRID_W
    for c in range(tn // w):
        o_ref[pl.ds(chunk * (tn // w) + c, n_tables, stride=GRID_W), :] = vals[:, c * w:(c + 1) * w]


def _bias_operand(rpb):
    h, nr, nc = rpb.shape
    assert nr == 2 * WIN_ROWS - 1 and nc == 2 * WIN_COLS - 1 <= DC_PAD
    rp = jnp.pad(rpb, ((0, 0), (0, 0), (0, DC_PAD - nc)))
    r2 = jnp.concatenate([rp[:, 0:N_DR_TILES], rp[:, 1:N_DR_TILES + 1]], axis=-1)
    return r2.reshape(h * N_DR_TILES, 2 * DC_PAD)


def _attn_body(q_ref, k_ref, v_ref, qc_ref, kc_ref, vc_ref, bias_ref, o_ref, oc_ref,
               s_s, p_s, l_s, sc_s, pc_s, lc_s, *, n_rows):
    n_pairs, s_len, pw = q_ref.shape
    c_len = qc_ref.shape[1]
    hd = pw // 2
    kwin = WIN_ROWS * GRID_W
    nt = (((1,), (1,)), ((), ()))
    rps = ATTN_ROWS_PER_STEP
    n_blocks = n_rows // rps

    lane = lax.broadcasted_iota(jnp.int32, (1, 2 * hd), 1)
    first = lane < hd

    def stack_heads(q):
        zero = jnp.zeros_like(q)
        return jnp.concatenate([jnp.where(first, q, zero), jnp.where(first, zero, q)], axis=0)

    def pick_heads(acc):
        n = acc.shape[0] // 2
        return jnp.where(first, acc[:n], acc[n:])

    def win_start(r):
        return jnp.clip(r - WIN_ROWS // 2, 0, n_rows - WIN_ROWS)

    def scores(pair, r, slot):
        rs = win_start(r)
        qq = stack_heads(q_ref[pair, pl.ds(pl.multiple_of(r * GRID_W, GRID_W), GRID_W), :])
        kw = k_ref[pair, pl.ds(pl.multiple_of(rs * GRID_W, GRID_W), kwin), :]
        t0 = rs - r + WIN_ROWS - 1
        bias = jnp.concatenate(
            [jnp.concatenate([bias_ref[2 * pair + hh, t0 + 2 * ii] for ii in range(WIN_ROWS // 2)], axis=1)
             for hh in range(2)], axis=0)
        s_s[slot, :, 0:kwin] = lax.dot_general(qq, kw, nt, preferred_element_type=F32) + bias
        s_s[slot, :, kwin:kwin + c_len] = lax.dot_general(qq, kc_ref[pair], nt, preferred_element_type=F32)

    def softmax2(s):
        p = jnp.exp2(s - s.max(axis=-1, keepdims=True))
        return p.astype(BF16), 1.0 / p.sum(axis=-1, keepdims=True)

    def probs(slot):
        p_s[slot], l_s[slot] = softmax2(s_s[slot])

    def output(pair, r, slot):
        vw = v_ref[pair, pl.ds(pl.multiple_of(win_start(r) * GRID_W, GRID_W), kwin), :]
        acc = jnp.dot(p_s[slot], jnp.concatenate([vw, vc_ref[pair]], axis=0), preferred_element_type=F32)
        o_ref[pair, pl.ds(pl.multiple_of(r * GRID_W, GRID_W), GRID_W), :] = (
            pick_heads(acc * l_s[slot]).astype(o_ref.dtype))

    def ctx_scores(pair):
        sc_s[...] = lax.dot_general(stack_heads(qc_ref[pair]), kc_ref[pair], nt, preferred_element_type=F32)

    def ctx_probs():
        pc_s[...], lc_s[...] = softmax2(sc_s[...])

    def ctx_output(pair):
        acc = jnp.dot(pc_s[...], vc_ref[pair], preferred_element_type=F32)
        oc_ref[pair] = pick_heads(acc * lc_s[...]).astype(oc_ref.dtype)

    def step(pair, blk, par, out_of=None, probs_of=True, do_scores=True):
        if out_of is None:
            out_of = (pair, blk - 2)
        if out_of is not False:
            for a in range(rps):
                output(out_of[0], out_of[1] * rps + a, par * rps + a)
        if probs_of:
            for a in range(rps):
                probs((1 - par) * rps + a)
        if do_scores:
            for a in range(rps):
                scores(pair, blk * rps + a, par * rps + a)

    for pair in range(n_pairs):
        if pair == 0:
            step(pair, 0, 0, out_of=False, probs_of=False)
            ctx_scores(pair)
            step(pair, 1, 1, out_of=False)
        else:
            step(pair, 0, 0, out_of=(pair - 1, n_blocks - 2))
            ctx_output(pair - 1)
            ctx_scores(pair)
            step(pair, 1, 1, out_of=(pair - 1, n_blocks - 1))
        ctx_probs()

        def loop_step(j, carry, pair=pair):
            step(pair, 2 * j, 0)
            step(pair, 2 * j + 1, 1)
            return carry

        lax.fori_loop(1, n_blocks // 2, loop_step, 0)

    last = n_pairs - 1
    step(last, n_blocks, 0, do_scores=False)
    ctx_output(last)
    step(last, n_blocks + 1, 1, probs_of=False, do_scores=False)


def _attention(q, k, v, qc, kc, vc, bias_tab):
    b, n_pairs, s_len, pw = q.shape
    c_len = qc.shape[2]
    pps = ATTN_PAIRS_PER_STEP
    n_rows = s_len // GRID_W
    n_keys = WIN_ROWS * GRID_W + c_len
    assert n_rows >= WIN_ROWS and n_rows % (2 * ATTN_ROWS_PER_STEP) == 0 and n_pairs % pps == 0
    lat = pl.BlockSpec((None, pps, s_len, pw), lambda bi, pi: (bi, pi, 0, 0))
    cx = pl.BlockSpec((None, pps, c_len, pw), lambda bi, pi: (bi, pi, 0, 0))
    return pl.pallas_call(
        functools.partial(_attn_body, n_rows=n_rows),
        grid=(b, n_pairs // pps),
        in_specs=[lat, lat, lat, cx, cx, cx,
                  pl.BlockSpec((2 * pps, N_DR_TILES, GRID_W, 2 * GRID_W), lambda bi, pi: (pi, 0, 0, 0))],
        out_specs=[lat, cx],
        out_shape=[jax.ShapeDtypeStruct(q.shape, BF16), jax.ShapeDtypeStruct(qc.shape, BF16)],
        scratch_shapes=[
            pltpu.VMEM((2 * ATTN_ROWS_PER_STEP, 2 * GRID_W, n_keys), F32),
            pltpu.VMEM((2 * ATTN_ROWS_PER_STEP, 2 * GRID_W, n_keys), BF16),
            pltpu.VMEM((2 * ATTN_ROWS_PER_STEP, 2 * GRID_W, 1), F32),
            pltpu.VMEM((2 * c_len, c_len), F32), pltpu.VMEM((2 * c_len, c_len), BF16),
            pltpu.VMEM((2 * c_len, 1), F32)],
        compiler_params=_cparams(("parallel", "parallel")),
    )(q, k, v, qc, kc, vc, bias_tab)


def _proj_mlp_body(x_ref, o_ref, mod_ref, g_ref, wo_ref, w1_ref, w2_ref, out_ref):
    o = jnp.concatenate([o_ref[p] for p in range(o_ref.shape[0])], axis=1)
    y = jnp.dot(o, wo_ref[...], preferred_element_type=F32)
    x = x_ref[...] + mod_ref[2:3, :] * y
    h = _norm_mod(x, g_ref[...], mod_ref[3:4, :], mod_ref[4:5, :]).astype(BF16)
    out_ref[...] = x + mod_ref[5:6, :] * _sq_relu_mlp(h, w1_ref, w2_ref)


def _proj_mlp(x, o, mods4, layer, ctx_row, g, wo, w1, w2, tm):
    b, t, d = x.shape
    d_ff = w1.shape[-1]
    mod_idx = (lambda bi, ti: (layer, bi, 0, 0)) if ctx_row is None else (lambda bi, ti: (layer, ctx_row, 0, 0))
    tok = pl.BlockSpec((None, tm, d), lambda bi, ti: (bi, ti, 0))
    const = lambda bi, ti: (0, 0)
    this_layer = lambda bi, ti: (layer, 0, 0)
    o_spec, _ = _pair_major(b, t, d, tm)
    return pl.pallas_call(
        _proj_mlp_body,
        grid=(b, t // tm),
        in_specs=[tok, o_spec, pl.BlockSpec((None, None, 6, d), mod_idx), pl.BlockSpec((1, d), const),
                  _resident((d, d), const), _resident((None, d, d_ff), this_layer),
                  _resident((None, d_ff, d), this_layer)],
        out_specs=tok,
        out_shape=jax.ShapeDtypeStruct((b, t, d), F32),
        compiler_params=_cparams(("parallel", "parallel")),
    )(x, o, mods4, g, wo, w1, w2)


def _dft_tables(n, dg):
    l2 = FFT_L2
    l1 = n // l2
    kb = SUBLANES
    ar = np.arange
    a = 2.0 * np.pi * np.outer(ar(dg), ar(dg)) / dg
    norm = 1.0 / math.sqrt(n * dg)
    cd, sd = np.cos(a) * norm, np.sin(a) * norm
    al = 2.0 * np.pi * np.outer(ar(l1), ar(l1)) / l1
    c1, s1 = np.cos(al), np.sin(al)
    m1 = np.block([[c1, -s1], [-s1, -c1]])
    k = ar(l1)[:, None, None] + l1 * ar(l2)[None, :, None]
    be = 2.0 * np.pi * (k * ar(l2)[None, None, :] % n) / n
    m2 = np.concatenate([np.cos(be), np.sin(be)], axis=-1)
    m2 = m2.reshape(l1 // kb, kb, l2, 2, l2)
    big = np.zeros((l1 // kb, l2, kb, 2, l2, kb))
    for kk in range(kb):
        big[:, :, kk, :, :, kk] = m2[:, kk]
    m2 = big.reshape(l1 // kb, l2 * kb, 2 * l2 * kb)
    return (jnp.asarray(cd, F32), jnp.asarray(sd, F32), jnp.asarray(m1, F32), jnp.asarray(m2, F32))


def _fold_body(cd_ref, sd_ref, w_ref, o_ref):
    w = w_ref[...]
    hp = lax.Precision.HIGHEST
    o_ref[0] = jnp.dot(cd_ref[...], w, precision=hp, preferred_element_type=F32).astype(BF16)
    o_ref[1] = jnp.dot(sd_ref[...], w, precision=hp, preferred_element_type=F32).astype(BF16)


def _dft1_body(x_ref, mod_ref, g_ref, w_ref, m_ref, z_ref):
    l1, nb, d = x_ref.shape
    x = x_ref[...].reshape(l1 * nb, d)
    h = _norm_mod(x, g_ref[...], mod_ref[0:1, :], mod_ref[1:2, :]).astype(BF16)

    def project(part):
        p = jnp.dot(h, w_ref[part], preferred_element_type=F32)
        return jnp.swapaxes(p.reshape(l1, nb, d), 0, 1).reshape(nb * l1, d).astype(BF16)

    pc, ps = project(0), project(1)
    m1 = m_ref[...].astype(BF16)
    zs = []
    for m in range(nb):
        rows = slice(m * l1, (m + 1) * l1)
        pm = jnp.concatenate([pc[rows], ps[rows]], axis=0)
        zs.append(jnp.dot(m1, pm, preferred_element_type=F32))
    for kblk in range(l1 // nb):
        for part in range(2):
            r0 = part * l1 + kblk * nb
            z_ref[kblk, part] = jnp.concatenate([z[r0:r0 + nb] for z in zs], axis=0).astype(BF16)


def _fourier_stage1(x, mods4, layer, g, wf, m1):
    b, s_len, d = x.shape
    l1 = s_len // FFT_L2
    nb = SUBLANES
    assert (nb * nb) % (2 * SUBLANES) == 0
    return pl.pallas_call(
        _dft1_body,
        grid=(b, FFT_L2 // nb),
        in_specs=[
            pl.BlockSpec((None, l1, nb, d), lambda bi, ti: (bi, 0, ti, 0)),
            pl.BlockSpec((None, None, 6, d), lambda bi, ti: (layer, bi, 0, 0)),
            pl.BlockSpec((1, d), lambda bi, ti: (0, 0)),
            _resident((2, d, d), lambda bi, ti: (0, 0, 0)),
            pl.BlockSpec((2 * l1, 2 * l1), lambda bi, ti: (0, 0)),
        ],
        out_specs=pl.BlockSpec((None, l1 // nb, 2, nb * nb, d), lambda bi, ti: (bi, 0, 0, ti, 0)),
        out_shape=jax.ShapeDtypeStruct((b, l1 // nb, 2, FFT_L2 * nb, d), BF16),
        compiler_params=_cparams(("parallel", "parallel")),
    )(x.reshape(b, l1, FFT_L2, d), mods4, g, wf, m1)


def _dft2_mlp_body(z_ref, x_ref, m_ref, mod_ref, g_ref, w1_ref, w2_ref, out_ref):
    l2, kb, d = x_ref.shape
    n = l2 * kb
    y = jnp.dot(m_ref[...].astype(BF16), z_ref[...].reshape(2 * n, d), preferred_element_type=F32)
    x = x_ref[...].reshape(n, d) + mod_ref[2:3, :] * y
    h = _norm_mod(x, g_ref[...], mod_ref[3:4, :], mod_ref[4:5, :]).astype(BF16)
    out = x + mod_ref[5:6, :] * _sq_relu_mlp(h, w1_ref, w2_ref)
    out_ref[...] = out.reshape(l2, kb, d)


def _fourier_stage2_mlp(z, x, mods4, layer, g, m2, w1, w2):
    b, s_len, d = x.shape
    l1 = s_len // FFT_L2
    kb = SUBLANES
    d_ff = w1.shape[-1]
    xblk = pl.BlockSpec((None, FFT_L2, kb, d), lambda bi, ki: (bi, 0, ki, 0))
    const = lambda bi, ki: (0, 0)
    this_layer = lambda bi, ki: (layer, 0, 0)
    out = pl.pallas_call(
        _dft2_mlp_body,
        grid=(b, l1 // kb),
        in_specs=[
            pl.BlockSpec((None, None, 2, FFT_L2 * kb, d), lambda bi, ki: (bi, ki, 0, 0, 0)), xblk,
            pl.BlockSpec((None, FFT_L2 * kb, 2 * FFT_L2 * kb), lambda bi, ki: (ki, 0, 0)),
            pl.BlockSpec((None, None, 6, d), lambda bi, ki: (layer, bi, 0, 0)),
            pl.BlockSpec((1, d), const),
            _resident((None, d, d_ff), this_layer), _resident((None, d_ff, d), this_layer),
        ],
        out_specs=xblk,
        out_shape=jax.ShapeDtypeStruct((b, FFT_L2, l1, d), F32),
        compiler_params=_cparams(("parallel", "parallel")),
    )(z, x.reshape(b, FFT_L2, l1, d), m2, mods4, g, w1, w2)
    return out.reshape(b, s_len, d)


def kernel(x, c, ctx, c_ctx, ada_w, ada_b, norm_g, attn_wqkv, attn_wo, q_norm_g, k_norm_g, rpb,
           fourier_w, mlp_w1, mlp_w2):
    b, s_len, d = x.shape
    depth = ada_w.shape[0]
    hd = q_norm_g.shape[-1]
    assert depth == 2 and b + 1 <= SUBLANES and 2 * hd == LANES and d % LANES == 0
    assert s_len % (FFT_L2 * SUBLANES) == 0 and s_len % TOKEN_TILE == 0
    ctx_row = b

    assert rpb.shape[0] == 1 and fourier_w.shape[0] == 1
    cc = jnp.zeros((SUBLANES, d), F32).at[:b].set(c).at[b].set(c_ctx)
    cd, sd, m1, m2 = _dft_tables(s_len, d // FOURIER_GROUPS)
    mods, bias_tab, wf = _prepare_parameters(cc, ada_w, ada_b, rpb[0], cd, sd, fourier_w[0])
    mods4 = mods.reshape(depth, SUBLANES, 6, d)

    for layer in range(depth):
        is_attn = layer % N_MIXERS == 0
        idx = layer // N_MIXERS
        need_ctx = layer < depth - 1
        g_pre = norm_g[layer, 0][None]
        g_post = norm_g[layer, 1][None]
        if is_attn:
            assert layer == 0, "the first layer's QKV kernel also casts every layer's MLP weights"
            gq2 = jnp.tile(q_norm_g[idx], 2)[None]
            gk2 = jnp.tile(k_norm_g[idx], 2)[None]
            q, k, v, wqkv, wo, w1, w2 = _qkv_and_weight_casts(
                x, mods4, layer, g_pre, gq2, gk2, attn_wqkv[idx], attn_wo[idx], mlp_w1, mlp_w2, TOKEN_TILE)
            qc, kc, vc = _qkv(ctx, mods4, layer, ctx_row, g_pre, gq2, gk2, wqkv, ctx.shape[1])
            o, oc = _attention(q, k, v, qc, kc, vc, bias_tab)
            x = _proj_mlp(x, o, mods4, layer, None, g_post, wo, w1, w2, TOKEN_TILE)
            if need_ctx:
                ctx = _proj_mlp(ctx, oc, mods4, layer, ctx_row, g_post, wo, w1, w2, ctx.shape[1])
        else:
            assert not need_ctx, "a Fourier layer that still feeds a context stream is not supported"
            z = _fourier_stage1(x, mods4, layer, g_pre, wf, m1)
            x = _fourier_stage2_mlp(z, x, mods4, layer, g_post, m2, w1, w2)
    return x
```

```python
import functools
import math

import numpy as np
import jax
import jax.numpy as jnp
from jax import lax
from jax.experimental import pallas as pl
from jax.experimental.pallas import tpu as pltpu

F32 = jnp.float32
BF16 = jnp.bfloat16

GRID_W = 64
WIN_ROWS = 8
WIN_COLS = 16
FOURIER_GROUPS = 4
N_MIXERS = 2
EPS = 1e-6
NEG_INF = float("-inf")
LOG2E = math.log2(math.e)

SUBLANES = 8
LANES = 128
VMEM_LIMIT = 56 * 1024 * 1024

N_DR_TILES = 2 * WIN_ROWS - 2
DC_PAD = 32
MODS_COL_TILE = 1536
ATTN_ROWS_PER_STEP = 4
ATTN_PAIRS_PER_STEP = 2
FFT_L2 = 64
TOKEN_TILE = 1024
QKV_TOKEN_TILE = 512
FF_CHUNK = 1024


def _cparams(sem):
    return pltpu.CompilerParams(dimension_semantics=sem, vmem_limit_bytes=VMEM_LIMIT)


def _resident(shape, index_map):
    return pl.BlockSpec(shape, index_map, pipeline_mode=pl.Buffered(1))


def _mods_body(c_ref, w_ref, b_ref, o_ref):
    c = c_ref[...]
    s = (c * jax.nn.sigmoid(c)).astype(BF16)
    o_ref[...] = jnp.dot(s, w_ref[...].astype(BF16), preferred_element_type=F32) + b_ref[...]


def _prep_body(c_ref, w_ref, b_ref, r_ref, cd_ref, sd_ref, fw_ref, mods_ref, bias_ref, wf_ref, *, bias_tn):
    _mods_body(c_ref, w_ref, b_ref, mods_ref)
    _bias_chunk(r_ref, bias_ref, pl.program_id(0), bias_tn)
    _fold_body(cd_ref, sd_ref, fw_ref, wf_ref)


def _prepare_parameters(cc, ada_w, ada_b, rpb, cd, sd, fourier_w):
    n_layers, d, d6 = ada_w.shape
    h = rpb.shape[0]
    dg = cd.shape[0]
    rows = cc.shape[0]
    tn = MODS_COL_TILE
    nt = d6 // tn
    n_steps = n_layers * nt
    bias_tn = GRID_W * 2 * GRID_W // n_steps
    n_groups = d // dg
    per_group = n_steps // n_groups
    fcols = d // per_group
    assert bias_tn % (2 * GRID_W) == 0 and n_steps % n_groups == 0 and fcols % LANES == 0
    n_bias_rows = h * N_DR_TILES * GRID_W
    mods, bias, wf = pl.pallas_call(
        functools.partial(_prep_body, bias_tn=bias_tn),
        grid=(n_steps,),
        in_specs=[
            pl.BlockSpec((rows, d), lambda i: (0, 0)),
            pl.BlockSpec((None, d, tn), lambda i: (i // nt, 0, i % nt)),
            pl.BlockSpec((None, 1, tn), lambda i: (i // nt, 0, i % nt)),
            pl.BlockSpec((h * N_DR_TILES, 2 * DC_PAD), lambda i: (0, 0)),
            pl.BlockSpec((dg, dg), lambda i: (0, 0)), pl.BlockSpec((dg, dg), lambda i: (0, 0)),
            pl.BlockSpec((dg, fcols), lambda i: (i // per_group, i % per_group)),
        ],
        out_specs=[
            pl.BlockSpec((None, rows, tn), lambda i: (i // nt, 0, i % nt)),
            pl.BlockSpec((n_bias_rows, 2 * GRID_W), lambda i: (0, 0)),
            pl.BlockSpec((2, dg, fcols), lambda i: (0, i // per_group, i % per_group)),
        ],
        out_shape=[jax.ShapeDtypeStruct((n_layers, rows, d6), F32),
                   jax.ShapeDtypeStruct((n_bias_rows, 2 * GRID_W), F32),
                   jax.ShapeDtypeStruct((2, d, d), BF16)],
        compiler_params=_cparams(("arbitrary",)),
    )(cc, ada_w, ada_b.reshape(n_layers, 1, d6), _bias_operand(rpb), cd, sd, fourier_w)
    return mods, bias.reshape(h, N_DR_TILES, GRID_W, 2 * GRID_W), wf


def _norm_mod(x, g, shift, scale):
    ms = jnp.mean(x * x, axis=-1, keepdims=True)
    return (x * lax.rsqrt(ms + EPS) * g) * (1.0 + scale) + shift


def _sq_relu_mlp(h, w1_ref, w2_ref):
    acc = None
    for c in range(0, w1_ref.shape[1], FF_CHUNK):
        a = jnp.maximum(jnp.dot(h, w1_ref[:, c:c + FF_CHUNK], preferred_element_type=F32), 0.0)
        part = jnp.dot((a * a).astype(BF16), w2_ref[c:c + FF_CHUNK, :], preferred_element_type=F32)
        acc = part if acc is None else acc + part
    return acc


def _qkv_body(x_ref, mod_ref, g_ref, gq_ref, gk_ref, w_ref, q_ref, kt_ref, v_ref):
    d = x_ref.shape[-1]
    hd = LANES // 2
    h = _norm_mod(x_ref[...], g_ref[...], mod_ref[0:1, :], mod_ref[1:2, :]).astype(BF16)
    first = lax.broadcasted_iota(jnp.int32, (1, LANES), 1) < hd
    gains = (gq_ref[...] * gk_ref[...] * (hd ** -0.5 * LOG2E), None, None)
    for i, out_ref in enumerate((q_ref, kt_ref, v_ref)):
        y = jnp.dot(h, w_ref[:, i * d:(i + 1) * d], preferred_element_type=F32)
        for p in range(out_ref.shape[0]):
            yp = y[:, p * LANES:(p + 1) * LANES]
            if out_ref is not v_ref:
                sq = yp * yp
                lo = jnp.sum(jnp.where(first, sq, 0.0), axis=-1, keepdims=True)
                hi = jnp.sum(jnp.where(first, 0.0, sq), axis=-1, keepdims=True)
                yp = yp * lax.rsqrt(jnp.where(first, lo, hi) * (1.0 / hd) + EPS)
                if gains[i] is not None:
                    yp = yp * gains[i]
            out_ref[p] = (yp.T if out_ref is kt_ref else yp).astype(BF16)


def _pair_major(b, t, d, tm, transposed=False):
    if transposed:
        spec = pl.BlockSpec((None, d // LANES, LANES, tm), lambda bi, ti: (bi, 0, 0, ti))
        return spec, jax.ShapeDtypeStruct((b, d // LANES, LANES, t), BF16)
    spec = pl.BlockSpec((None, d // LANES, tm, LANES), lambda bi, ti: (bi, 0, ti, 0))
    return spec, jax.ShapeDtypeStruct((b, d // LANES, t, LANES), BF16)


def _qkv(x, mods4, layer, ctx_row, g, gq2, gk2, w_bf16, tm):
    b, t, d = x.shape
    mod_idx = (lambda bi, ti: (layer, bi, 0, 0)) if ctx_row is None else (lambda bi, ti: (layer, ctx_row, 0, 0))
    tok = pl.BlockSpec((None, tm, d), lambda bi, ti: (bi, ti, 0))
    out_spec, out = _pair_major(b, t, d, tm)
    kt_spec, kt_out = _pair_major(b, t, d, tm, transposed=True)
    gain = pl.BlockSpec((1, LANES), lambda bi, ti: (0, 0))
    return pl.pallas_call(
        _qkv_body,
        grid=(b, t // tm),
        in_specs=[
            tok,
            pl.BlockSpec((None, None, 6, d), mod_idx),
            pl.BlockSpec((1, d), lambda bi, ti: (0, 0)),
            gain, gain,
            _resident((d, 3 * d), lambda bi, ti: (0, 0)),
        ],
        out_specs=[out_spec, kt_spec, out_spec],
        out_shape=[out, kt_out, out],
        compiler_params=_cparams(("parallel", "parallel")),
    )(x, mods4, g, gq2, gk2, w_bf16)


def _qkv_cast_body(x_ref, mod_ref, g_ref, gq_ref, gk_ref, wqkv_ref, wo_ref, w1_ref, w2_ref,
                   q_ref, k_ref, v_ref, wqkv_o, wo_o, w1_o, w2_o):
    @pl.when((pl.program_id(0) == 0) & (pl.program_id(1) == 0))
    def _():
        wqkv_o[...] = wqkv_ref[...].astype(BF16)

    wo_o[...] = wo_ref[...].astype(BF16)
    w1_o[...] = w1_ref[...].astype(BF16)
    w2_o[...] = w2_ref[...].astype(BF16)
    _qkv_body(x_ref, mod_ref, g_ref, gq_ref, gk_ref, wqkv_o, q_ref, k_ref, v_ref)


def _qkv_and_weight_casts(x, mods4, layer, g, gq2, gk2, wqkv, wo, mlp_w1, mlp_w2, tm):
    b, t, d = x.shape
    nt = t // tm
    n_steps = b * nt
    n_layers, _, d_ff = mlp_w1.shape
    assert d % (2 * SUBLANES * n_steps) == 0
    step = lambda bi, ti: bi * nt + ti
    tok = pl.BlockSpec((None, tm, d), lambda bi, ti: (bi, ti, 0))
    qkv_spec, qkv_out = _pair_major(b, t, d, tm)
    kt_spec, kt_out = _pair_major(b, t, d, tm, transposed=True)
    wo_blk = pl.BlockSpec((d // n_steps, d), lambda bi, ti: (step(bi, ti), 0))
    w1_blk = pl.BlockSpec((n_layers, d // n_steps, d_ff), lambda bi, ti: (0, step(bi, ti), 0))
    w2_blk = pl.BlockSpec((n_layers, d_ff // n_steps, d), lambda bi, ti: (0, step(bi, ti), 0))
    gain = pl.BlockSpec((1, LANES), lambda bi, ti: (0, 0))
    return pl.pallas_call(
        _qkv_cast_body,
        grid=(b, nt),
        in_specs=[
            tok,
            pl.BlockSpec((None, None, 6, d), lambda bi, ti: (layer, bi, 0, 0)),
            pl.BlockSpec((1, d), lambda bi, ti: (0, 0)),
            gain, gain,
            _resident((d, 3 * d), lambda bi, ti: (0, 0)),
            wo_blk, w1_blk, w2_blk,
        ],
        out_specs=[qkv_spec, kt_spec, qkv_spec,
                   pl.BlockSpec((d, 3 * d), lambda bi, ti: (0, 0)), wo_blk, w1_blk, w2_blk],
        out_shape=[qkv_out, kt_out, qkv_out,
                   jax.ShapeDtypeStruct((d, 3 * d), BF16), jax.ShapeDtypeStruct((d, d), BF16),
                   jax.ShapeDtypeStruct(mlp_w1.shape, BF16), jax.ShapeDtypeStruct(mlp_w2.shape, BF16)],
        compiler_params=_cparams(("arbitrary", "arbitrary")),
    )(x, mods4, g, gq2, gk2, wqkv, wo, mlp_w1, mlp_w2)


def _bias_chunk(r_ref, o_ref, chunk, tn):
    col0 = chunk * tn
    e = col0 + lax.broadcasted_iota(jnp.int32, (2 * DC_PAD, tn), 1)
    r = lax.broadcasted_iota(jnp.int32, (2 * DC_PAD, tn), 0)
    cq = e // (2 * GRID_W)
    half = (e // GRID_W) % 2
    ck = e % GRID_W
    sel = ((r // DC_PAD == half) & (r % DC_PAD == ck - cq + WIN_COLS - 1)).astype(BF16)
    rv = r_ref[...]
    hi = rv.astype(BF16)
    rem = rv - hi.astype(F32)
    mid = rem.astype(BF16)
    lo = (rem - mid.astype(F32)).astype(BF16)
    acc = jnp.dot(hi, sel, preferred_element_type=F32)
    acc += jnp.dot(mid, sel, preferred_element_type=F32)
    acc += jnp.dot(lo, sel, preferred_element_type=F32)
    e1 = col0 + lax.broadcasted_iota(jnp.int32, (1, tn), 1)
    cq1 = e1 // (2 * GRID_W)
    ck1 = e1 % GRID_W
    start = jnp.clip(cq1 - WIN_COLS // 2, 0, GRID_W - WIN_COLS)
    ok = (ck1 >= start) & (ck1 < start + WIN_COLS)
    vals = acc * LOG2E + jnp.where(ok, 0.0, NEG_INF)
    n_tables = r_ref.shape[0]
    w = 2 * GRID_W
    for c in range(tn // w):
        o_ref[pl.ds(chunk * (tn // w) + c, n_tables, stride=GRID_W), :] = vals[:, c * w:(c + 1) * w]


def _bias_operand(rpb):
    h, nr, nc = rpb.shape
    assert nr == 2 * WIN_ROWS - 1 and nc == 2 * WIN_COLS - 1 <= DC_PAD
    rp = jnp.pad(rpb, ((0, 0), (0, 0), (0, DC_PAD - nc)))
    r2 = jnp.concatenate([rp[:, 0:N_DR_TILES], rp[:, 1:N_DR_TILES + 1]], axis=-1)
    return r2.reshape(h * N_DR_TILES, 2 * DC_PAD)


def _attn_body(q_ref, kt_ref, v_ref, qc_ref, kct_ref, vc_ref, bias_ref, o_ref, oc_ref,
               kt_s, s_s, p_s, l_s, sc_s, pc_s, lc_s, *, n_rows):
    n_pairs, s_len, pw = q_ref.shape
    c_len = qc_ref.shape[1]
    hd = pw // 2
    kwin = WIN_ROWS * GRID_W
    rps = ATTN_ROWS_PER_STEP
    n_blocks = n_rows // rps

    lane = lax.broadcasted_iota(jnp.int32, (1, 2 * hd), 1)
    first = lane < hd

    def stack_heads(q):
        zero = jnp.zeros_like(q)
        return jnp.concatenate([jnp.where(first, q, zero), jnp.where(first, zero, q)], axis=0)

    def pick_heads(acc):
        n = acc.shape[0] // 2
        return jnp.where(first, acc[:n], acc[n:])

    def win_start(r):
        return jnp.clip(r - WIN_ROWS // 2, 0, n_rows - WIN_ROWS)

    def stage_keys(pair):
        kt = kt_ref[pair]
        kt_s[0] = kt
        kt_s[1] = pltpu.roll(kt.astype(F32), s_len - GRID_W, 1).astype(BF16)

    def scores(pair, r, slot):
        rs = win_start(r)
        qq = stack_heads(q_ref[pair, pl.ds(pl.multiple_of(r * GRID_W, GRID_W), GRID_W), :])
        odd = rs % 2
        kw = kt_s[odd, :, pl.ds(pl.multiple_of((rs - odd) * GRID_W, 2 * GRID_W), kwin)]
        t0 = rs - r + WIN_ROWS - 1
        bias = jnp.concatenate(
            [jnp.concatenate([bias_ref[2 * pair + hh, t0 + 2 * ii] for ii in range(WIN_ROWS // 2)], axis=1)
             for hh in range(2)], axis=0)
        s_s[slot, :, 0:kwin] = jnp.dot(qq, kw, preferred_element_type=F32) + bias
        s_s[slot, :, kwin:kwin + c_len] = jnp.dot(qq, kct_ref[pair], preferred_element_type=F32)

    def softmax2(s):
        p = jnp.exp2(s - s.max(axis=-1, keepdims=True))
        return p.astype(BF16), 1.0 / p.sum(axis=-1, keepdims=True)

    def probs(slot):
        p_s[slot], l_s[slot] = softmax2(s_s[slot])

    def output(pair, r, slot):
        vw = v_ref[pair, pl.ds(pl.multiple_of(win_start(r) * GRID_W, GRID_W), kwin), :]
        acc = jnp.dot(p_s[slot], jnp.concatenate([vw, vc_ref[pair]], axis=0), preferred_element_type=F32)
        o_ref[pair, pl.ds(pl.multiple_of(r * GRID_W, GRID_W), GRID_W), :] = (
            pick_heads(acc * l_s[slot]).astype(o_ref.dtype))

    def ctx_scores(pair):
        sc_s[...] = jnp.dot(stack_heads(qc_ref[pair]), kct_ref[pair], preferred_element_type=F32)

    def ctx_probs():
        pc_s[...], lc_s[...] = softmax2(sc_s[...])

    def ctx_output(pair):
        acc = jnp.dot(pc_s[...], vc_ref[pair], preferred_element_type=F32)
        oc_ref[pair] = pick_heads(acc * lc_s[...]).astype(oc_ref.dtype)

    def step(pair, blk, par, out_of=None, probs_of=True, do_scores=True):
        if out_of is None:
            out_of = (pair, blk - 2)
        if out_of is not False:
            for a in range(rps):
                output(out_of[0], out_of[1] * rps + a, par * rps + a)
        if probs_of:
            for a in range(rps):
                probs((1 - par) * rps + a)
        if do_scores:
            for a in range(rps):
                scores(pair, blk * rps + a, par * rps + a)

    for pair in range(n_pairs):
        stage_keys(pair)
        if pair == 0:
            step(pair, 0, 0, out_of=False, probs_of=False)
            ctx_scores(pair)
            step(pair, 1, 1, out_of=False)
        else:
            step(pair, 0, 0, out_of=(pair - 1, n_blocks - 2))
            ctx_output(pair - 1)
            ctx_scores(pair)
            step(pair, 1, 1, out_of=(pair - 1, n_blocks - 1))
        ctx_probs()

        def loop_step(j, carry, pair=pair):
            step(pair, 2 * j, 0)
            step(pair, 2 * j + 1, 1)
            return carry

        lax.fori_loop(1, n_blocks // 2, loop_step, 0)

    last = n_pairs - 1
    step(last, n_blocks, 0, do_scores=False)
    ctx_output(last)
    step(last, n_blocks + 1, 1, probs_of=False, do_scores=False)


def _attention(q, kt, v, qc, kct, vc, bias_tab):
    b, n_pairs, s_len, pw = q.shape
    c_len = qc.shape[2]
    pps = ATTN_PAIRS_PER_STEP
    n_rows = s_len // GRID_W
    n_keys = WIN_ROWS * GRID_W + c_len
    assert n_rows >= WIN_ROWS and n_rows % (2 * ATTN_ROWS_PER_STEP) == 0 and n_pairs % pps == 0
    assert 2 * GRID_W == LANES
    lat = pl.BlockSpec((None, pps, s_len, pw), lambda bi, pi: (bi, pi, 0, 0))
    cx = pl.BlockSpec((None, pps, c_len, pw), lambda bi, pi: (bi, pi, 0, 0))
    lat_t = pl.BlockSpec((None, pps, pw, s_len), lambda bi, pi: (bi, pi, 0, 0))
    cx_t = pl.BlockSpec((None, pps, pw, c_len), lambda bi, pi: (bi, pi, 0, 0))
    return pl.pallas_call(
        functools.partial(_attn_body, n_rows=n_rows),
        grid=(b, n_pairs // pps),
        in_specs=[lat, lat_t, lat, cx, cx_t, cx,
                  pl.BlockSpec((2 * pps, N_DR_TILES, GRID_W, 2 * GRID_W), lambda bi, pi: (pi, 0, 0, 0))],
        out_specs=[lat, cx],
        out_shape=[jax.ShapeDtypeStruct(q.shape, BF16), jax.ShapeDtypeStruct(qc.shape, BF16)],
        scratch_shapes=[
            pltpu.VMEM((2, pw, s_len), BF16),
            pltpu.VMEM((2 * ATTN_ROWS_PER_STEP, 2 * GRID_W, n_keys), F32),
            pltpu.VMEM((2 * ATTN_ROWS_PER_STEP, 2 * GRID_W, n_keys), BF16),
            pltpu.VMEM((2 * ATTN_ROWS_PER_STEP, 2 * GRID_W, 1), F32),
            pltpu.VMEM((2 * c_len, c_len), F32), pltpu.VMEM((2 * c_len, c_len), BF16),
            pltpu.VMEM((2 * c_len, 1), F32)],
        compiler_params=_cparams(("parallel", "parallel")),
    )(q, kt, v, qc, kct, vc, bias_tab)


def _proj_mlp_body(x_ref, o_ref, mod_ref, g_ref, wo_ref, w1_ref, w2_ref, out_ref):
    o = jnp.concatenate([o_ref[p] for p in range(o_ref.shape[0])], axis=1)
    y = jnp.dot(o, wo_ref[...], preferred_element_type=F32)
    x = x_ref[...] + mod_ref[2:3, :] * y
    h = _norm_mod(x, g_ref[...], mod_ref[3:4, :], mod_ref[4:5, :]).astype(BF16)
    out_ref[...] = x + mod_ref[5:6, :] * _sq_relu_mlp(h, w1_ref, w2_ref)


def _proj_mlp(x, o, mods4, layer, ctx_row, g, wo, w1, w2, tm):
    b, t, d = x.shape
    d_ff = w1.shape[-1]
    mod_idx = (lambda bi, ti: (layer, bi, 0, 0)) if ctx_row is None else (lambda bi, ti: (layer, ctx_row, 0, 0))
    tok = pl.BlockSpec((None, tm, d), lambda bi, ti: (bi, ti, 0))
    const = lambda bi, ti: (0, 0)
    this_layer = lambda bi, ti: (layer, 0, 0)
    o_spec, _ = _pair_major(b, t, d, tm)
    return pl.pallas_call(
        _proj_mlp_body,
        grid=(b, t // tm),
        in_specs=[tok, o_spec, pl.BlockSpec((None, None, 6, d), mod_idx), pl.BlockSpec((1, d), const),
                  _resident((d, d), const), _resident((None, d, d_ff), this_layer),
                  _resident((None, d_ff, d), this_layer)],
        out_specs=tok,
        out_shape=jax.ShapeDtypeStruct((b, t, d), F32),
        compiler_params=_cparams(("parallel", "parallel")),
    )(x, o, mods4, g, wo, w1, w2)


def _dft_tables(n, dg):
    l2 = FFT_L2
    l1 = n // l2
    kb = SUBLANES
    ar = np.arange
    a = 2.0 * np.pi * np.outer(ar(dg), ar(dg)) / dg
    norm = 1.0 / math.sqrt(n * dg)
    cd, sd = np.cos(a) * norm, np.sin(a) * norm
    al = 2.0 * np.pi * np.outer(ar(l1), ar(l1)) / l1
    c1, s1 = np.cos(al), np.sin(al)
    m1 = np.block([[c1, -s1], [-s1, -c1]])
    k = ar(l1)[:, None, None] + l1 * ar(l2)[None, :, None]
    be = 2.0 * np.pi * (k * ar(l2)[None, None, :] % n) / n
    m2 = np.concatenate([np.cos(be), np.sin(be)], axis=-1)
    m2 = m2.reshape(l1 // kb, kb, l2, 2, l2)
    big = np.zeros((l1 // kb, l2, kb, 2, l2, kb))
    for kk in range(kb):
        big[:, :, kk, :, :, kk] = m2[:, kk]
    m2 = big.reshape(l1 // kb, l2 * kb, 2 * l2 * kb)
    return (jnp.asarray(cd, F32), jnp.asarray(sd, F32), jnp.asarray(m1, F32), jnp.asarray(m2, F32))


def _fold_body(cd_ref, sd_ref, w_ref, o_ref):
    w = w_ref[...]
    hp = lax.Precision.HIGHEST
    o_ref[0] = jnp.dot(cd_ref[...], w, precision=hp, preferred_element_type=F32).astype(BF16)
    o_ref[1] = jnp.dot(sd_ref[...], w, precision=hp, preferred_element_type=F32).astype(BF16)


def _dft1_body(x_ref, mod_ref, g_ref, w_ref, m_ref, z_ref):
    l1, nb, d = x_ref.shape
    x = x_ref[...].reshape(l1 * nb, d)
    h = _norm_mod(x, g_ref[...], mod_ref[0:1, :], mod_ref[1:2, :]).astype(BF16)

    def project(part):
        p = jnp.dot(h, w_ref[part], preferred_element_type=F32)
        return jnp.swapaxes(p.reshape(l1, nb, d), 0, 1).reshape(nb * l1, d).astype(BF16)

    pc, ps = project(0), project(1)
    m1 = m_ref[...].astype(BF16)
    zs = []
    for m in range(nb):
        rows = slice(m * l1, (m + 1) * l1)
        pm = jnp.concatenate([pc[rows], ps[rows]], axis=0)
        zs.append(jnp.dot(m1, pm, preferred_element_type=F32))
    for kblk in range(l1 // nb):
        for part in range(2):
            r0 = part * l1 + kblk * nb
            z_ref[kblk, part] = jnp.concatenate([z[r0:r0 + nb] for z in zs], axis=0).astype(BF16)


def _fourier_stage1(x, mods4, layer, g, wf, m1):
    b, s_len, d = x.shape
    l1 = s_len // FFT_L2
    nb = SUBLANES
    assert (nb * nb) % (2 * SUBLANES) == 0
    return pl.pallas_call(
        _dft1_body,
        grid=(b, FFT_L2 // nb),
        in_specs=[
            pl.BlockSpec((None, l1, nb, d), lambda bi, ti: (bi, 0, ti, 0)),
            pl.BlockSpec((None, None, 6, d), lambda bi, ti: (layer, bi, 0, 0)),
            pl.BlockSpec((1, d), lambda bi, ti: (0, 0)),
            _resident((2, d, d), lambda bi, ti: (0, 0, 0)),
            pl.BlockSpec((2 * l1, 2 * l1), lambda bi, ti: (0, 0)),
        ],
        out_specs=pl.BlockSpec((None, l1 // nb, 2, nb * nb, d), lambda bi, ti: (bi, 0, 0, ti, 0)),
        out_shape=jax.ShapeDtypeStruct((b, l1 // nb, 2, FFT_L2 * nb, d), BF16),
        compiler_params=_cparams(("parallel", "parallel")),
    )(x.reshape(b, l1, FFT_L2, d), mods4, g, wf, m1)


def _dft2_mlp_body(z_ref, x_ref, m_ref, mod_ref, g_ref, w1_ref, w2_ref, out_ref):
    l2, kb, d = x_ref.shape
    n = l2 * kb
    y = jnp.dot(m_ref[...].astype(BF16), z_ref[...].reshape(2 * n, d), preferred_element_type=F32)
    x = x_ref[...].reshape(n, d) + mod_ref[2:3, :] * y
    h = _norm_mod(x, g_ref[...], mod_ref[3:4, :], mod_ref[4:5, :]).astype(BF16)
    out = x + mod_ref[5:6, :] * _sq_relu_mlp(h, w1_ref, w2_ref)
    out_ref[...] = out.reshape(l2, kb, d)


def _fourier_stage2_mlp(z, x, mods4, layer, g, m2, w1, w2):
    b, s_len, d = x.shape
    l1 = s_len // FFT_L2
    kb = SUBLANES
    d_ff = w1.shape[-1]
    xblk = pl.BlockSpec((None, FFT_L2, kb, d), lambda bi, ki: (bi, 0, ki, 0))
    const = lambda bi, ki: (0, 0)
    this_layer = lambda bi, ki: (layer, 0, 0)
    out = pl.pallas_call(
        _dft2_mlp_body,
        grid=(b, l1 // kb),
        in_specs=[
            pl.BlockSpec((None, None, 2, FFT_L2 * kb, d), lambda bi, ki: (bi, ki, 0, 0, 0)), xblk,
            pl.BlockSpec((None, FFT_L2 * kb, 2 * FFT_L2 * kb), lambda bi, ki: (ki, 0, 0)),
            pl.BlockSpec((None, None, 6, d), lambda bi, ki: (layer, bi, 0, 0)),
            pl.BlockSpec((1, d), const),
            _resident((None, d, d_ff), this_layer), _resident((None, d_ff, d), this_layer),
        ],
        out_specs=xblk,
        out_shape=jax.ShapeDtypeStruct((b, FFT_L2, l1, d), F32),
        compiler_params=_cparams(("parallel", "parallel")),
    )(z, x.reshape(b, FFT_L2, l1, d), m2, mods4, g, w1, w2)
    return out.reshape(b, s_len, d)


def kernel(x, c, ctx, c_ctx, ada_w, ada_b, norm_g, attn_wqkv, attn_wo, q_norm_g, k_norm_g, rpb,
           fourier_w, mlp_w1, mlp_w2):
    b, s_len, d = x.shape
    depth = ada_w.shape[0]
    hd = q_norm_g.shape[-1]
    assert depth == 2 and b + 1 <= SUBLANES and 2 * hd == LANES and d % LANES == 0
    assert s_len % (FFT_L2 * SUBLANES) == 0 and s_len % TOKEN_TILE == 0 and s_len % QKV_TOKEN_TILE == 0
    ctx_row = b

    assert rpb.shape[0] == 1 and fourier_w.shape[0] == 1
    cc = jnp.zeros((SUBLANES, d), F32).at[:b].set(c).at[b].set(c_ctx)
    cd, sd, m1, m2 = _dft_tables(s_len, d // FOURIER_GROUPS)
    mods, bias_tab, wf = _prepare_parameters(cc, ada_w, ada_b, rpb[0], cd, sd, fourier_w[0])
    mods4 = mods.reshape(depth, SUBLANES, 6, d)

    for layer in range(depth):
        is_attn = layer % N_MIXERS == 0
        idx = layer // N_MIXERS
        need_ctx = layer < depth - 1
        g_pre = norm_g[layer, 0][None]
        g_post = norm_g[layer, 1][None]
        if is_attn:
            assert layer == 0, "the first layer's QKV kernel also casts every layer's MLP weights"
            gq2 = jnp.tile(q_norm_g[idx], 2)[None]
            gk2 = jnp.tile(k_norm_g[idx], 2)[None]
            q, k, v, wqkv, wo, w1, w2 = _qkv_and_weight_casts(
                x, mods4, layer, g_pre, gq2, gk2, attn_wqkv[idx], attn_wo[idx], mlp_w1, mlp_w2, QKV_TOKEN_TILE)
            qc, kc, vc = _qkv(ctx, mods4, layer, ctx_row, g_pre, gq2, gk2, wqkv, ctx.shape[1])
            o, oc = _attention(q, k, v, qc, kc, vc, bias_tab)
            x = _proj_mlp(x, o, mods4, layer, None, g_post, wo, w1, w2, TOKEN_TILE)
            if need_ctx:
                ctx = _proj_mlp(ctx, oc, mods4, layer, ctx_row, g_post, wo, w1, w2, ctx.shape[1])
        else:
            assert not need_ctx, "a Fourier layer that still feeds a context stream is not supported"
            z = _fourier_stage1(x, mods4, layer, g_pre, wf, m1)
            x = _fourier_stage2_mlp(z, x, mods4, layer, g_post, m2, w1, w2)
    return x
```

```python
import functools
import math

import numpy as np
import jax
import jax.numpy as jnp
from jax import lax
from jax.experimental import pallas as pl
from jax.experimental.pallas import tpu as pltpu

F32 = jnp.float32
BF16 = jnp.bfloat16

GRID_W = 64
WIN_ROWS = 8
WIN_COLS = 16
FOURIER_GROUPS = 4
N_MIXERS = 2
EPS = 1e-6
NEG_INF = float("-inf")
LOG2E = math.log2(math.e)

SUBLANES = 8
LANES = 128
VMEM_LIMIT = 56 * 1024 * 1024
QKV_VMEM_LIMIT = 60 * 1024 * 1024

N_DR_TILES = 2 * WIN_ROWS - 2
DC_PAD = 32
MODS_COL_TILE = 1536
ATTN_ROWS_PER_STEP = 4
ATTN_PAIRS_PER_STEP = 2
FFT_L2 = 64
TOKEN_TILE = 1024
FF_CHUNK = 1024


def _cparams(sem, vmem_limit=VMEM_LIMIT):
    return pltpu.CompilerParams(dimension_semantics=sem, vmem_limit_bytes=vmem_limit)


def _resident(shape, index_map):
    return pl.BlockSpec(shape, index_map, pipeline_mode=pl.Buffered(1))


def _mods_body(c_ref, w_ref, b_ref, o_ref):
    c = c_ref[...]
    s = (c * jax.nn.sigmoid(c)).astype(BF16)
    o_ref[...] = jnp.dot(s, w_ref[...].astype(BF16), preferred_element_type=F32) + b_ref[...]


def _prep_body(c_ref, w_ref, b_ref, r_ref, cd_ref, sd_ref, fw_ref, mods_ref, bias_ref, wf_ref, *, bias_tn):
    _mods_body(c_ref, w_ref, b_ref, mods_ref)
    _bias_chunk(r_ref, bias_ref, pl.program_id(0), bias_tn)
    _fold_body(cd_ref, sd_ref, fw_ref, wf_ref)


def _prepare_parameters(cc, ada_w, ada_b, rpb, cd, sd, fourier_w):
    n_layers, d, d6 = ada_w.shape
    h = rpb.shape[0]
    dg = cd.shape[0]
    rows = cc.shape[0]
    tn = MODS_COL_TILE
    nt = d6 // tn
    n_steps = n_layers * nt
    bias_tn = GRID_W * 2 * GRID_W // n_steps
    n_groups = d // dg
    per_group = n_steps // n_groups
    fcols = d // per_group
    assert bias_tn % (2 * GRID_W) == 0 and n_steps % n_groups == 0 and fcols % LANES == 0
    n_bias_rows = h * N_DR_TILES * GRID_W
    mods, bias, wf = pl.pallas_call(
        functools.partial(_prep_body, bias_tn=bias_tn),
        grid=(n_steps,),
        in_specs=[
            pl.BlockSpec((rows, d), lambda i: (0, 0)),
            pl.BlockSpec((None, d, tn), lambda i: (i // nt, 0, i % nt)),
            pl.BlockSpec((None, 1, tn), lambda i: (i // nt, 0, i % nt)),
            pl.BlockSpec((h * N_DR_TILES, 2 * DC_PAD), lambda i: (0, 0)),
            pl.BlockSpec((dg, dg), lambda i: (0, 0)), pl.BlockSpec((dg, dg), lambda i: (0, 0)),
            pl.BlockSpec((dg, fcols), lambda i: (i // per_group, i % per_group)),
        ],
        out_specs=[
            pl.BlockSpec((None, rows, tn), lambda i: (i // nt, 0, i % nt)),
            pl.BlockSpec((n_bias_rows, 2 * GRID_W), lambda i: (0, 0)),
            pl.BlockSpec((2, dg, fcols), lambda i: (0, i // per_group, i % per_group)),
        ],
        out_shape=[jax.ShapeDtypeStruct((n_layers, rows, d6), F32),
                   jax.ShapeDtypeStruct((n_bias_rows, 2 * GRID_W), F32),
                   jax.ShapeDtypeStruct((2, d, d), BF16)],
        compiler_params=_cparams(("arbitrary",)),
    )(cc, ada_w, ada_b.reshape(n_layers, 1, d6), _bias_operand(rpb), cd, sd, fourier_w)
    return mods, bias.reshape(h, N_DR_TILES, GRID_W, 2 * GRID_W), wf


def _norm_mod(x, g, shift, scale):
    ms = jnp.mean(x * x, axis=-1, keepdims=True)
    return (x * lax.rsqrt(ms + EPS) * g) * (1.0 + scale) + shift


def _sq_relu_mlp(h, w1_ref, w2_ref):
    acc = None
    for c in range(0, w1_ref.shape[1], FF_CHUNK):
        a = jnp.maximum(jnp.dot(h, w1_ref[:, c:c + FF_CHUNK], preferred_element_type=F32), 0.0)
        part = jnp.dot((a * a).astype(BF16), w2_ref[c:c + FF_CHUNK, :], preferred_element_type=F32)
        acc = part if acc is None else acc + part
    return acc


def _qkv_body(x_ref, mod_ref, g_ref, gq_ref, gk_ref, w_ref, q_ref, k_ref, v_ref):
    d = x_ref.shape[-1]
    hd = LANES // 2
    h = _norm_mod(x_ref[...], g_ref[...], mod_ref[0:1, :], mod_ref[1:2, :]).astype(BF16)
    first = lax.broadcasted_iota(jnp.int32, (1, LANES), 1) < hd
    gains = (gq_ref[...] * gk_ref[...] * (hd ** -0.5 * LOG2E), None, None)
    for i, out_ref in enumerate((q_ref, k_ref, v_ref)):
        y = jnp.dot(h, w_ref[:, i * d:(i + 1) * d], preferred_element_type=F32)
        for p in range(out_ref.shape[0]):
            yp = y[:, p * LANES:(p + 1) * LANES]
            if out_ref is not v_ref:
                sq = yp * yp
                lo = jnp.sum(jnp.where(first, sq, 0.0), axis=-1, keepdims=True)
                hi = jnp.sum(jnp.where(first, 0.0, sq), axis=-1, keepdims=True)
                yp = yp * lax.rsqrt(jnp.where(first, lo, hi) * (1.0 / hd) + EPS)
                if gains[i] is not None:
                    yp = yp * gains[i]
            out_ref[p] = yp.astype(BF16)


def _pair_major(b, t, d, tm):
    spec = pl.BlockSpec((None, d // LANES, tm, LANES), lambda bi, ti: (bi, 0, ti, 0))
    return spec, jax.ShapeDtypeStruct((b, d // LANES, t, LANES), BF16)


def _qkv_cast_body(x_ref, xc_ref, mod_ref, modc_ref, g_ref, gq_ref, gk_ref, wqkv_ref, wo_ref, w1_ref, w2_ref,
                   q_ref, k_ref, v_ref, qc_ref, kc_ref, vc_ref, wqkv_o, wo_o, w1_o, w2_o):
    ti = pl.program_id(1)
    n_latent_tiles = pl.num_programs(1) - 1

    @pl.when((pl.program_id(0) == 0) & (ti == 0))
    def _():
        wqkv_o[...] = wqkv_ref[...].astype(BF16)

    @pl.when(ti < n_latent_tiles)
    def _():
        wo_o[...] = wo_ref[...].astype(BF16)
        w1_o[...] = w1_ref[...].astype(BF16)
        w2_o[...] = w2_ref[...].astype(BF16)
        _qkv_body(x_ref, mod_ref, g_ref, gq_ref, gk_ref, wqkv_o, q_ref, k_ref, v_ref)

    @pl.when(ti == n_latent_tiles)
    def _():
        _qkv_body(xc_ref, modc_ref, g_ref, gq_ref, gk_ref, wqkv_o, qc_ref, kc_ref, vc_ref)


def _qkv_and_weight_casts(x, ctx, mods4, layer, ctx_row, g, gq2, gk2, wqkv, wo, mlp_w1, mlp_w2, tm):
    b, t, d = x.shape
    c_len = ctx.shape[1]
    nt = t // tm
    n_steps = b * nt
    n_layers, _, d_ff = mlp_w1.shape
    assert d % (2 * SUBLANES * n_steps) == 0
    tile = lambda ti: jnp.minimum(ti, nt - 1)
    step = lambda bi, ti: bi * nt + tile(ti)
    tok = pl.BlockSpec((None, tm, d), lambda bi, ti: (bi, tile(ti), 0))
    qkv_spec = pl.BlockSpec((None, d // LANES, tm, LANES), lambda bi, ti: (bi, 0, tile(ti), 0))
    _, qkv_out = _pair_major(b, t, d, tm)
    ctx_spec = pl.BlockSpec((None, d // LANES, c_len, LANES), lambda bi, ti: (bi, 0, 0, 0))
    _, ctx_out = _pair_major(b, c_len, d, c_len)
    wo_blk = pl.BlockSpec((d // n_steps, d), lambda bi, ti: (step(bi, ti), 0))
    w1_blk = pl.BlockSpec((n_layers, d // n_steps, d_ff), lambda bi, ti: (0, step(bi, ti), 0))
    w2_blk = pl.BlockSpec((n_layers, d_ff // n_steps, d), lambda bi, ti: (0, step(bi, ti), 0))
    gain = pl.BlockSpec((1, LANES), lambda bi, ti: (0, 0))
    return pl.pallas_call(
        _qkv_cast_body,
        grid=(b, nt + 1),
        in_specs=[
            tok,
            pl.BlockSpec((None, c_len, d), lambda bi, ti: (bi, 0, 0)),
            pl.BlockSpec((None, None, 6, d), lambda bi, ti: (layer, bi, 0, 0)),
            pl.BlockSpec((None, None, 6, d), lambda bi, ti: (layer, ctx_row, 0, 0)),
            pl.BlockSpec((1, d), lambda bi, ti: (0, 0)),
            gain, gain,
            _resident((d, 3 * d), lambda bi, ti: (0, 0)),
            wo_blk, w1_blk, w2_blk,
        ],
        out_specs=[qkv_spec, qkv_spec, qkv_spec, ctx_spec, ctx_spec, ctx_spec,
                   pl.BlockSpec((d, 3 * d), lambda bi, ti: (0, 0)), wo_blk, w1_blk, w2_blk],
        out_shape=[qkv_out, qkv_out, qkv_out, ctx_out, ctx_out, ctx_out,
                   jax.ShapeDtypeStruct((d, 3 * d), BF16), jax.ShapeDtypeStruct((d, d), BF16),
                   jax.ShapeDtypeStruct(mlp_w1.shape, BF16), jax.ShapeDtypeStruct(mlp_w2.shape, BF16)],
        compiler_params=_cparams(("arbitrary", "arbitrary"), QKV_VMEM_LIMIT),
    )(x, ctx, mods4, mods4, g, gq2, gk2, wqkv, wo, mlp_w1, mlp_w2)


def _bias_chunk(r_ref, o_ref, chunk, tn):
    col0 = chunk * tn
    e = col0 + lax.broadcasted_iota(jnp.int32, (2 * DC_PAD, tn), 1)
    r = lax.broadcasted_iota(jnp.int32, (2 * DC_PAD, tn), 0)
    cq = e // (2 * GRID_W)
    half = (e // GRID_W) % 2
    ck = e % GRID_W
    sel = ((r // DC_PAD == half) & (r % DC_PAD == ck - cq + WIN_COLS - 1)).astype(BF16)
    rv = r_ref[...]
    hi = rv.astype(BF16)
    rem = rv - hi.astype(F32)
    mid = rem.astype(BF16)
    lo = (rem - mid.astype(F32)).astype(BF16)
    acc = jnp.dot(hi, sel, preferred_element_type=F32)
    acc += jnp.dot(mid, sel, preferred_element_type=F32)
    acc += jnp.dot(lo, sel, preferred_element_type=F32)
    e1 = col0 + lax.broadcasted_iota(jnp.int32, (1, tn), 1)
    cq1 = e1 // (2 * GRID_W)
    ck1 = e1 % GRID_W
    start = jnp.clip(cq1 - WIN_COLS // 2, 0, GRID_W - WIN_COLS)
    ok = (ck1 >= start) & (ck1 < start + WIN_COLS)
    vals = acc * LOG2E + jnp.where(ok, 0.0, NEG_INF)
    n_tables = r_ref.shape[0]
    w = 2 * GRID_W
    for c in range(tn // w):
        o_ref[pl.ds(chunk * (tn // w) + c, n_tables, stride=GRID_W), :] = vals[:, c * w:(c + 1) * w]


def _bias_operand(rpb):
    h, nr, nc = rpb.shape
    assert nr == 2 * WIN_ROWS - 1 and nc == 2 * WIN_COLS - 1 <= DC_PAD
    rp = jnp.pad(rpb, ((0, 0), (0, 0), (0, DC_PAD - nc)))
    r2 = jnp.concatenate([rp[:, 0:N_DR_TILES], rp[:, 1:N_DR_TILES + 1]], axis=-1)
    return r2.reshape(h * N_DR_TILES, 2 * DC_PAD)


def _attn_body(q_ref, k_ref, v_ref, qc_ref, kc_ref, vc_ref, bias_ref, o_ref, oc_ref,
               s_s, p_s, l_s, sc_s, pc_s, lc_s, *, n_rows):
    n_pairs, s_len, pw = q_ref.shape
    c_len = qc_ref.shape[1]
    hd = pw // 2
    kwin = WIN_ROWS * GRID_W
    nt = (((1,), (1,)), ((), ()))
    rps = ATTN_ROWS_PER_STEP
    n_blocks = n_rows // rps

    lane = lax.broadcasted_iota(jnp.int32, (1, 2 * hd), 1)
    first = lane < hd

    def stack_heads(q):
        zero = jnp.zeros_like(q)
        return jnp.concatenate([jnp.where(first, q, zero), jnp.where(first, zero, q)], axis=0)

    def pick_heads(acc):
        n = acc.shape[0] // 2
        return jnp.where(first, acc[:n], acc[n:])

    def win_start(r):
        return jnp.clip(r - WIN_ROWS // 2, 0, n_rows - WIN_ROWS)

    def scores(pair, r, slot):
        rs = win_start(r)
        qq = stack_heads(q_ref[pair, pl.ds(pl.multiple_of(r * GRID_W, GRID_W), GRID_W), :])
        kw = k_ref[pair, pl.ds(pl.multiple_of(rs * GRID_W, GRID_W), kwin), :]
        t0 = rs - r + WIN_ROWS - 1
        bias = jnp.concatenate(
            [jnp.concatenate([bias_ref[2 * pair + hh, t0 + 2 * ii] for ii in range(WIN_ROWS // 2)], axis=1)
             for hh in range(2)], axis=0)
        s_s[slot, :, 0:kwin] = lax.dot_general(qq, kw, nt, preferred_element_type=F32) + bias
        s_s[slot, :, kwin:kwin + c_len] = lax.dot_general(qq, kc_ref[pair], nt, preferred_element_type=F32)

    def softmax2(s):
        p = jnp.exp2(s - s.max(axis=-1, keepdims=True))
        return p.astype(BF16), 1.0 / p.sum(axis=-1, keepdims=True)

    def probs(slot):
        p_s[slot], l_s[slot] = softmax2(s_s[slot])

    def output(pair, r, slot):
        vw = v_ref[pair, pl.ds(pl.multiple_of(win_start(r) * GRID_W, GRID_W), kwin), :]
        acc = jnp.dot(p_s[slot], jnp.concatenate([vw, vc_ref[pair]], axis=0), preferred_element_type=F32)
        o_ref[pair, pl.ds(pl.multiple_of(r * GRID_W, GRID_W), GRID_W), :] = (
            pick_heads(acc * l_s[slot]).astype(o_ref.dtype))

    def ctx_scores(pair):
        sc_s[...] = lax.dot_general(stack_heads(qc_ref[pair]), kc_ref[pair], nt, preferred_element_type=F32)

    def ctx_probs():
        pc_s[...], lc_s[...] = softmax2(sc_s[...])

    def ctx_output(pair):
        acc = jnp.dot(pc_s[...], vc_ref[pair], preferred_element_type=F32)
        oc_ref[pair] = pick_heads(acc * lc_s[...]).astype(oc_ref.dtype)

    def step(pair, blk, par, out_of=None, probs_of=True, do_scores=True):
        if out_of is None:
            out_of = (pair, blk - 2)
        if out_of is not False:
            for a in range(rps):
                output(out_of[0], out_of[1] * rps + a, par * rps + a)
        if probs_of:
            for a in range(rps):
                probs((1 - par) * rps + a)
        if do_scores:
            for a in range(rps):
                scores(pair, blk * rps + a, par * rps + a)

    for pair in range(n_pairs):
        if pair == 0:
            step(pair, 0, 0, out_of=False, probs_of=False)
            ctx_scores(pair)
            step(pair, 1, 1, out_of=False)
        else:
            step(pair, 0, 0, out_of=(pair - 1, n_blocks - 2))
            ctx_output(pair - 1)
            ctx_scores(pair)
            step(pair, 1, 1, out_of=(pair - 1, n_blocks - 1))
        ctx_probs()

        def loop_step(j, carry, pair=pair):
            step(pair, 2 * j, 0)
            step(pair, 2 * j + 1, 1)
            return carry

        lax.fori_loop(1, n_blocks // 2, loop_step, 0)

    last = n_pairs - 1
    step(last, n_blocks, 0, do_scores=False)
    ctx_output(last)
    step(last, n_blocks + 1, 1, probs_of=False, do_scores=False)


def _attention(q, k, v, qc, kc, vc, bias_tab):
    b, n_pairs, s_len, pw = q.shape
    c_len = qc.shape[2]
    pps = ATTN_PAIRS_PER_STEP
    n_rows = s_len // GRID_W
    n_keys = WIN_ROWS * GRID_W + c_len
    assert n_rows >= WIN_ROWS and n_rows % (2 * ATTN_ROWS_PER_STEP) == 0 and n_pairs % pps == 0
    lat = pl.BlockSpec((None, pps, s_len, pw), lambda bi, pi: (bi, pi, 0, 0))
    cx = pl.BlockSpec((None, pps, c_len, pw), lambda bi, pi: (bi, pi, 0, 0))
    return pl.pallas_call(
        functools.partial(_attn_body, n_rows=n_rows),
        grid=(b, n_pairs // pps),
        in_specs=[lat, lat, lat, cx, cx, cx,
                  pl.BlockSpec((2 * pps, N_DR_TILES, GRID_W, 2 * GRID_W), lambda bi, pi: (pi, 0, 0, 0))],
        out_specs=[lat, cx],
        out_shape=[jax.ShapeDtypeStruct(q.shape, BF16), jax.ShapeDtypeStruct(qc.shape, BF16)],
        scratch_shapes=[
            pltpu.VMEM((2 * ATTN_ROWS_PER_STEP, 2 * GRID_W, n_keys), F32),
            pltpu.VMEM((2 * ATTN_ROWS_PER_STEP, 2 * GRID_W, n_keys), BF16),
            pltpu.VMEM((2 * ATTN_ROWS_PER_STEP, 2 * GRID_W, 1), F32),
            pltpu.VMEM((2 * c_len, c_len), F32), pltpu.VMEM((2 * c_len, c_len), BF16),
            pltpu.VMEM((2 * c_len, 1), F32)],
        compiler_params=_cparams(("parallel", "parallel")),
    )(q, k, v, qc, kc, vc, bias_tab)


def _proj_mlp_body(x_ref, o_ref, mod_ref, g_ref, wo_ref, w1_ref, w2_ref, out_ref):
    o = jnp.concatenate([o_ref[p] for p in range(o_ref.shape[0])], axis=1)
    y = jnp.dot(o, wo_ref[...], preferred_element_type=F32)
    x = x_ref[...] + mod_ref[2:3, :] * y
    h = _norm_mod(x, g_ref[...], mod_ref[3:4, :], mod_ref[4:5, :]).astype(BF16)
    out_ref[...] = x + mod_ref[5:6, :] * _sq_relu_mlp(h, w1_ref, w2_ref)


def _proj_mlp(x, o, mods4, layer, ctx_row, g, wo, w1, w2, tm):
    b, t, d = x.shape
    d_ff = w1.shape[-1]
    mod_idx = (lambda bi, ti: (layer, bi, 0, 0)) if ctx_row is None else (lambda bi, ti: (layer, ctx_row, 0, 0))
    tok = pl.BlockSpec((None, tm, d), lambda bi, ti: (bi, ti, 0))
    const = lambda bi, ti: (0, 0)
    this_layer = lambda bi, ti: (layer, 0, 0)
    o_spec, _ = _pair_major(b, t, d, tm)
    return pl.pallas_call(
        _proj_mlp_body,
        grid=(b, t // tm),
        in_specs=[tok, o_spec, pl.BlockSpec((None, None, 6, d), mod_idx), pl.BlockSpec((1, d), const),
                  _resident((d, d), const), _resident((None, d, d_ff), this_layer),
                  _resident((None, d_ff, d), this_layer)],
        out_specs=tok,
        out_shape=jax.ShapeDtypeStruct((b, t, d), F32),
        compiler_params=_cparams(("parallel", "parallel")),
    )(x, o, mods4, g, wo, w1, w2)


def _dft_tables(n, dg):
    l2 = FFT_L2
    l1 = n // l2
    kb = SUBLANES
    ar = np.arange
    a = 2.0 * np.pi * np.outer(ar(dg), ar(dg)) / dg
    norm = 1.0 / math.sqrt(n * dg)
    cd, sd = np.cos(a) * norm, np.sin(a) * norm
    al = 2.0 * np.pi * np.outer(ar(l1), ar(l1)) / l1
    c1, s1 = np.cos(al), np.sin(al)
    m1 = np.block([[c1, -s1], [-s1, -c1]])
    k = ar(l1)[:, None, None] + l1 * ar(l2)[None, :, None]
    be = 2.0 * np.pi * (k * ar(l2)[None, None, :] % n) / n
    m2 = np.concatenate([np.cos(be), np.sin(be)], axis=-1)
    m2 = m2.reshape(l1 // kb, kb, l2, 2, l2)
    big = np.zeros((l1 // kb, l2, kb, 2, l2, kb))
    for kk in range(kb):
        big[:, :, kk, :, :, kk] = m2[:, kk]
    m2 = big.reshape(l1 // kb, l2 * kb, 2 * l2 * kb)
    return (jnp.asarray(cd, F32), jnp.asarray(sd, F32), jnp.asarray(m1, F32), jnp.asarray(m2, F32))


def _fold_body(cd_ref, sd_ref, w_ref, o_ref):
    w = w_ref[...]
    hp = lax.Precision.HIGHEST
    o_ref[0] = jnp.dot(cd_ref[...], w, precision=hp, preferred_element_type=F32).astype(BF16)
    o_ref[1] = jnp.dot(sd_ref[...], w, precision=hp, preferred_element_type=F32).astype(BF16)


def _dft1_body(x_ref, mod_ref, g_ref, w_ref, m_ref, z_ref):
    l1, nb, d = x_ref.shape
    x = x_ref[...].reshape(l1 * nb, d)
    h = _norm_mod(x, g_ref[...], mod_ref[0:1, :], mod_ref[1:2, :]).astype(BF16)

    def project(part):
        p = jnp.dot(h, w_ref[part], preferred_element_type=F32)
        return jnp.swapaxes(p.reshape(l1, nb, d), 0, 1).reshape(nb * l1, d).astype(BF16)

    pc, ps = project(0), project(1)
    m1 = m_ref[...].astype(BF16)
    zs = []
    for m in range(nb):
        rows = slice(m * l1, (m + 1) * l1)
        pm = jnp.concatenate([pc[rows], ps[rows]], axis=0)
        zs.append(jnp.dot(m1, pm, preferred_element_type=F32))
    for kblk in range(l1 // nb):
        for part in range(2):
            r0 = part * l1 + kblk * nb
            z_ref[kblk, part] = jnp.concatenate([z[r0:r0 + nb] for z in zs], axis=0).astype(BF16)


def _fourier_stage1(x, mods4, layer, g, wf, m1):
    b, s_len, d = x.shape
    l1 = s_len // FFT_L2
    nb = SUBLANES
    assert (nb * nb) % (2 * SUBLANES) == 0
    return pl.pallas_call(
        _dft1_body,
        grid=(b, FFT_L2 // nb),
        in_specs=[
            pl.BlockSpec((None, l1, nb, d), lambda bi, ti: (bi, 0, ti, 0)),
            pl.BlockSpec((None, None, 6, d), lambda bi, ti: (layer, bi, 0, 0)),
            pl.BlockSpec((1, d), lambda bi, ti: (0, 0)),
            _resident((2, d, d), lambda bi, ti: (0, 0, 0)),
            pl.BlockSpec((2 * l1, 2 * l1), lambda bi, ti: (0, 0)),
        ],
        out_specs=pl.BlockSpec((None, l1 // nb, 2, nb * nb, d), lambda bi, ti: (bi, 0, 0, ti, 0)),
        out_shape=jax.ShapeDtypeStruct((b, l1 // nb, 2, FFT_L2 * nb, d), BF16),
        compiler_params=_cparams(("parallel", "parallel")),
    )(x.reshape(b, l1, FFT_L2, d), mods4, g, wf, m1)


def _dft2_mlp_body(z_ref, x_ref, m_ref, mod_ref, g_ref, w1_ref, w2_ref, out_ref):
    l2, kb, d = x_ref.shape
    n = l2 * kb
    y = jnp.dot(m_ref[...].astype(BF16), z_ref[...].reshape(2 * n, d), preferred_element_type=F32)
    x = x_ref[...].reshape(n, d) + mod_ref[2:3, :] * y
    h = _norm_mod(x, g_ref[...], mod_ref[3:4, :], mod_ref[4:5, :]).astype(BF16)
    out = x + mod_ref[5:6, :] * _sq_relu_mlp(h, w1_ref, w2_ref)
    out_ref[...] = out.reshape(l2, kb, d)


def _fourier_stage2_mlp(z, x, mods4, layer, g, m2, w1, w2):
    b, s_len, d = x.shape
    l1 = s_len // FFT_L2
    kb = SUBLANES
    d_ff = w1.shape[-1]
    xblk = pl.BlockSpec((None, FFT_L2, kb, d), lambda bi, ki: (bi, 0, ki, 0))
    const = lambda bi, ki: (0, 0)
    this_layer = lambda bi, ki: (layer, 0, 0)
    out = pl.pallas_call(
        _dft2_mlp_body,
        grid=(b, l1 // kb),
        in_specs=[
            pl.BlockSpec((None, None, 2, FFT_L2 * kb, d), lambda bi, ki: (bi, ki, 0, 0, 0)), xblk,
            pl.BlockSpec((None, FFT_L2 * kb, 2 * FFT_L2 * kb), lambda bi, ki: (ki, 0, 0)),
            pl.BlockSpec((None, None, 6, d), lambda bi, ki: (layer, bi, 0, 0)),
            pl.BlockSpec((1, d), const),
            _resident((None, d, d_ff), this_layer), _resident((None, d_ff, d), this_layer),
        ],
        out_specs=xblk,
        out_shape=jax.ShapeDtypeStruct((b, FFT_L2, l1, d), F32),
        compiler_params=_cparams(("parallel", "parallel")),
    )(z, x.reshape(b, FFT_L2, l1, d), m2, mods4, g, w1, w2)
    return out.reshape(b, s_len, d)


def kernel(x, c, ctx, c_ctx, ada_w, ada_b, norm_g, attn_wqkv, attn_wo, q_norm_g, k_norm_g, rpb,
           fourier_w, mlp_w1, mlp_w2):
    b, s_len, d = x.shape
    depth = ada_w.shape[0]
    hd = q_norm_g.shape[-1]
    assert depth == 2 and b + 1 <= SUBLANES and 2 * hd == LANES and d % LANES == 0
    assert s_len % (FFT_L2 * SUBLANES) == 0 and s_len % TOKEN_TILE == 0
    ctx_row = b

    assert rpb.shape[0] == 1 and fourier_w.shape[0] == 1
    cc = jnp.zeros((SUBLANES, d), F32).at[:b].set(c).at[b].set(c_ctx)
    cd, sd, m1, m2 = _dft_tables(s_len, d // FOURIER_GROUPS)
    mods, bias_tab, wf = _prepare_parameters(cc, ada_w, ada_b, rpb[0], cd, sd, fourier_w[0])
    mods4 = mods.reshape(depth, SUBLANES, 6, d)

    for layer in range(depth):
        is_attn = layer % N_MIXERS == 0
        idx = layer // N_MIXERS
        need_ctx = layer < depth - 1
        g_pre = norm_g[layer, 0][None]
        g_post = norm_g[layer, 1][None]
        if is_attn:
            assert layer == 0, "the first layer's QKV kernel also casts every layer's MLP weights"
            gq2 = jnp.tile(q_norm_g[idx], 2)[None]
            gk2 = jnp.tile(k_norm_g[idx], 2)[None]
            q, k, v, qc, kc, vc, _, wo, w1, w2 = _qkv_and_weight_casts(
                x, ctx, mods4, layer, ctx_row, g_pre, gq2, gk2, attn_wqkv[idx], attn_wo[idx], mlp_w1, mlp_w2,
                TOKEN_TILE)
            o, oc = _attention(q, k, v, qc, kc, vc, bias_tab)
            x = _proj_mlp(x, o, mods4, layer, None, g_post, wo, w1, w2, TOKEN_TILE)
            if need_ctx:
                ctx = _proj_mlp(ctx, oc, mods4, layer, ctx_row, g_post, wo, w1, w2, ctx.shape[1])
        else:
            assert not need_ctx, "a Fourier layer that still feeds a context stream is not supported"
            z = _fourier_stage1(x, mods4, layer, g_pre, wf, m1)
            x = _fourier_stage2_mlp(z, x, mods4, layer, g_post, m2, w1, w2)
    return x
```

```python
import functools
import math

import numpy as np
import jax
import jax.numpy as jnp
from jax import lax
from jax.experimental import pallas as pl
from jax.experimental.pallas import tpu as pltpu

F32 = jnp.float32
BF16 = jnp.bfloat16

GRID_W = 64
WIN_ROWS = 8
WIN_COLS = 16
FOURIER_GROUPS = 4
N_MIXERS = 2
EPS = 1e-6
NEG_INF = float("-inf")
LOG2E = math.log2(math.e)

SUBLANES = 8
LANES = 128
VMEM_LIMIT = 56 * 1024 * 1024

N_DR_TILES = 2 * WIN_ROWS - 2
MODS_ROW_TILE = 256
ATTN_ROWS_PER_STEP = 4
ATTN_PAIRS_PER_STEP = 2
FFT_L2 = 64
TOKEN_TILE = 1024
FF_CHUNK = 1024


def _cparams(sem):
    return pltpu.CompilerParams(dimension_semantics=sem, vmem_limit_bytes=VMEM_LIMIT)


def _resident(shape, index_map):
    return pl.BlockSpec(shape, index_map, pipeline_mode=pl.Buffered(1))


def _mods_body(c_ref, w_ref, b_ref, o_ref, first):
    c = c_ref[...]
    s = (c * jax.nn.sigmoid(c)).astype(BF16)
    part = jnp.dot(s, w_ref[...].astype(BF16), preferred_element_type=F32)

    @pl.when(first)
    def _():
        o_ref[...] = part + b_ref[...]

    @pl.when(jnp.logical_not(first))
    def _():
        o_ref[...] += part


def _prep_body(c_ref, w_ref, b_ref, r_ref, cd_ref, sd_ref, fw_ref, mods_ref, bias_ref, wf_ref, *, nt):
    _mods_body(c_ref, w_ref, b_ref, mods_ref, pl.program_id(0) % nt == 0)
    _bias_chunk(r_ref, bias_ref)
    _fold_body(cd_ref, sd_ref, fw_ref, wf_ref)


def _prepare_parameters(cc, ada_w, ada_b, rpb, cd, sd, fourier_w):
    n_layers, d, d6 = ada_w.shape
    h = rpb.shape[0]
    dg = cd.shape[0]
    rows = cc.shape[0]
    tk = MODS_ROW_TILE
    nt = d // tk
    n_steps = n_layers * nt
    tables = h * N_DR_TILES // n_steps
    n_groups = d // dg
    per_group = n_steps // n_groups
    fcols = d // per_group
    assert (h * N_DR_TILES) % n_steps == 0 and n_steps % n_groups == 0 and fcols % LANES == 0
    n_bias_rows = h * N_DR_TILES * GRID_W
    mods, bias, wf = pl.pallas_call(
        functools.partial(_prep_body, nt=nt),
        grid=(n_steps,),
        in_specs=[
            pl.BlockSpec((rows, tk), lambda i: (0, i % nt)),
            pl.BlockSpec((None, tk, d6), lambda i: (i // nt, i % nt, 0)),
            pl.BlockSpec((None, 1, d6), lambda i: (i // nt, 0, 0)),
            pl.BlockSpec((None, tables, 2 * GRID_W), lambda i: (i, 0, 0)),
            pl.BlockSpec((dg, dg), lambda i: (0, 0)), pl.BlockSpec((dg, dg), lambda i: (0, 0)),
            pl.BlockSpec((dg, fcols), lambda i: (i // per_group, i % per_group)),
        ],
        out_specs=[
            pl.BlockSpec((None, rows, d6), lambda i: (i // nt, 0, 0)),
            pl.BlockSpec((tables * GRID_W, 2 * GRID_W), lambda i: (i, 0)),
            pl.BlockSpec((2, dg, fcols), lambda i: (0, i // per_group, i % per_group)),
        ],
        out_shape=[jax.ShapeDtypeStruct((n_layers, rows, d6), F32),
                   jax.ShapeDtypeStruct((n_bias_rows, 2 * GRID_W), F32),
                   jax.ShapeDtypeStruct((2, d, d), BF16)],
        compiler_params=_cparams(("arbitrary",)),
    )(cc, ada_w, ada_b.reshape(n_layers, 1, d6), _bias_operand(rpb).reshape(n_steps, tables, 2 * GRID_W),
      cd, sd, fourier_w)
    return mods, bias.reshape(h, N_DR_TILES, GRID_W, 2 * GRID_W), wf


def _norm_mod(x, g, shift, scale):
    ms = jnp.mean(x * x, axis=-1, keepdims=True)
    return (x * lax.rsqrt(ms + EPS) * g) * (1.0 + scale) + shift


def _sq_relu_mlp(h, w1_ref, w2_ref):
    acc = None
    for c in range(0, w1_ref.shape[1], FF_CHUNK):
        a = jnp.maximum(jnp.dot(h, w1_ref[:, c:c + FF_CHUNK], preferred_element_type=F32), 0.0)
        part = jnp.dot((a * a).astype(BF16), w2_ref[c:c + FF_CHUNK, :], preferred_element_type=F32)
        acc = part if acc is None else acc + part
    return acc


def _qkv_body(x_ref, mod_ref, g_ref, gq_ref, gk_ref, w_ref, q_ref, k_ref, v_ref):
    d = x_ref.shape[-1]
    hd = LANES // 2
    h = _norm_mod(x_ref[...], g_ref[...], mod_ref[0:1, :], mod_ref[1:2, :]).astype(BF16)
    first = lax.broadcasted_iota(jnp.int32, (1, LANES), 1) < hd
    gains = (gq_ref[...] * gk_ref[...] * (hd ** -0.5 * LOG2E), None, None)
    for i, out_ref in enumerate((q_ref, k_ref, v_ref)):
        y = jnp.dot(h, w_ref[:, i * d:(i + 1) * d], preferred_element_type=F32)
        for p in range(out_ref.shape[0]):
            yp = y[:, p * LANES:(p + 1) * LANES]
            if out_ref is not v_ref:
                sq = yp * yp
                lo = jnp.sum(jnp.where(first, sq, 0.0), axis=-1, keepdims=True)
                hi = jnp.sum(jnp.where(first, 0.0, sq), axis=-1, keepdims=True)
                yp = yp * lax.rsqrt(jnp.where(first, lo, hi) * (1.0 / hd) + EPS)
                if gains[i] is not None:
                    yp = yp * gains[i]
            out_ref[p] = yp.astype(BF16)


def _pair_major(b, t, d, tm):
    spec = pl.BlockSpec((None, d // LANES, tm, LANES), lambda bi, ti: (bi, 0, ti, 0))
    return spec, jax.ShapeDtypeStruct((b, d // LANES, t, LANES), BF16)


def _qkv(x, mods4, layer, ctx_row, g, gq2, gk2, w_bf16, tm):
    b, t, d = x.shape
    mod_idx = (lambda bi, ti: (layer, bi, 0, 0)) if ctx_row is None else (lambda bi, ti: (layer, ctx_row, 0, 0))
    tok = pl.BlockSpec((None, tm, d), lambda bi, ti: (bi, ti, 0))
    out_spec, out = _pair_major(b, t, d, tm)
    gain = pl.BlockSpec((1, LANES), lambda bi, ti: (0, 0))
    return pl.pallas_call(
        _qkv_body,
        grid=(b, t // tm),
        in_specs=[
            tok,
            pl.BlockSpec((None, None, 6, d), mod_idx),
            pl.BlockSpec((1, d), lambda bi, ti: (0, 0)),
            gain, gain,
            _resident((d, 3 * d), lambda bi, ti: (0, 0)),
        ],
        out_specs=[out_spec, out_spec, out_spec],
        out_shape=[out, out, out],
        compiler_params=_cparams(("parallel", "parallel")),
    )(x, mods4, g, gq2, gk2, w_bf16)


def _qkv_cast_body(x_ref, mod_ref, g_ref, gq_ref, gk_ref, wqkv_ref, wo_ref, w1_ref, w2_ref,
                   q_ref, k_ref, v_ref, wqkv_o, wo_o, w1_o, w2_o):
    @pl.when((pl.program_id(0) == 0) & (pl.program_id(1) == 0))
    def _():
        wqkv_o[...] = wqkv_ref[...].astype(BF16)

    wo_o[...] = wo_ref[...].astype(BF16)
    w1_o[...] = w1_ref[...].astype(BF16)
    w2_o[...] = w2_ref[...].astype(BF16)
    _qkv_body(x_ref, mod_ref, g_ref, gq_ref, gk_ref, wqkv_o, q_ref, k_ref, v_ref)


def _qkv_and_weight_casts(x, mods4, layer, g, gq2, gk2, wqkv, wo, mlp_w1, mlp_w2, tm):
    b, t, d = x.shape
    nt = t // tm
    n_steps = b * nt
    n_layers, _, d_ff = mlp_w1.shape
    assert d % (2 * SUBLANES * n_steps) == 0
    step = lambda bi, ti: bi * nt + ti
    tok = pl.BlockSpec((None, tm, d), lambda bi, ti: (bi, ti, 0))
    qkv_spec, qkv_out = _pair_major(b, t, d, tm)
    wo_blk = pl.BlockSpec((d // n_steps, d), lambda bi, ti: (step(bi, ti), 0))
    w1_blk = pl.BlockSpec((n_layers, d // n_steps, d_ff), lambda bi, ti: (0, step(bi, ti), 0))
    w2_blk = pl.BlockSpec((n_layers, d_ff // n_steps, d), lambda bi, ti: (0, step(bi, ti), 0))
    gain = pl.BlockSpec((1, LANES), lambda bi, ti: (0, 0))
    return pl.pallas_call(
        _qkv_cast_body,
        grid=(b, nt),
        in_specs=[
            tok,
            pl.BlockSpec((None, None, 6, d), lambda bi, ti: (layer, bi, 0, 0)),
            pl.BlockSpec((1, d), lambda bi, ti: (0, 0)),
            gain, gain,
            _resident((d, 3 * d), lambda bi, ti: (0, 0)),
            wo_blk, w1_blk, w2_blk,
        ],
        out_specs=[qkv_spec, qkv_spec, qkv_spec,
                   pl.BlockSpec((d, 3 * d), lambda bi, ti: (0, 0)), wo_blk, w1_blk, w2_blk],
        out_shape=[qkv_out, qkv_out, qkv_out,
                   jax.ShapeDtypeStruct((d, 3 * d), BF16), jax.ShapeDtypeStruct((d, d), BF16),
                   jax.ShapeDtypeStruct(mlp_w1.shape, BF16), jax.ShapeDtypeStruct(mlp_w2.shape, BF16)],
        compiler_params=_cparams(("arbitrary", "arbitrary")),
    )(x, mods4, g, gq2, gk2, wqkv, wo, mlp_w1, mlp_w2)


def _bias_chunk(r_ref, o_ref):
    w = 2 * GRID_W
    cq = lax.broadcasted_iota(jnp.int32, (GRID_W, w), 0)
    ck = lax.broadcasted_iota(jnp.int32, (GRID_W, w), 1) % GRID_W
    start = jnp.clip(cq - WIN_COLS // 2, 0, GRID_W - WIN_COLS)
    mask = jnp.where((ck >= start) & (ck < start + WIN_COLS), 0.0, NEG_INF)
    for tb in range(r_ref.shape[0]):
        row = jnp.roll(r_ref[tb:tb + 1, :], w - (WIN_COLS - 1), axis=1)
        tile = pltpu.roll(jnp.broadcast_to(row, (GRID_W, w)), 0, 1, stride=1, stride_axis=0)
        o_ref[tb * GRID_W:(tb + 1) * GRID_W, :] = tile * LOG2E + mask


def _bias_operand(rpb):
    h, nr, nc = rpb.shape
    assert nr == 2 * WIN_ROWS - 1 and nc == 2 * WIN_COLS - 1 <= GRID_W
    rp = jnp.pad(rpb, ((0, 0), (0, 0), (0, GRID_W - nc)))
    r2 = jnp.concatenate([rp[:, 0:N_DR_TILES], rp[:, 1:N_DR_TILES + 1]], axis=-1)
    return r2.reshape(h * N_DR_TILES, 2 * GRID_W)


def _attn_body(q_ref, k_ref, v_ref, qc_ref, kc_ref, vc_ref, bias_ref, o_ref, oc_ref,
               s_s, p_s, l_s, sc_s, pc_s, lc_s, *, n_rows):
    n_pairs, s_len, pw = q_ref.shape
    c_len = qc_ref.shape[1]
    hd = pw // 2
    kwin = WIN_ROWS * GRID_W
    nt = (((1,), (1,)), ((), ()))
    rps = ATTN_ROWS_PER_STEP
    n_blocks = n_rows // rps

    lane = lax.broadcasted_iota(jnp.int32, (1, 2 * hd), 1)
    first = lane < hd

    def stack_heads(q):
        zero = jnp.zeros_like(q)
        return jnp.concatenate([jnp.where(first, q, zero), jnp.where(first, zero, q)], axis=0)

    def pick_heads(acc):
        n = acc.shape[0] // 2
        return jnp.where(first, acc[:n], acc[n:])

    def win_start(r):
        return jnp.clip(r - WIN_ROWS // 2, 0, n_rows - WIN_ROWS)

    def scores(pair, r, slot):
        rs = win_start(r)
        qq = stack_heads(q_ref[pair, pl.ds(pl.multiple_of(r * GRID_W, GRID_W), GRID_W), :])
        kw = k_ref[pair, pl.ds(pl.multiple_of(rs * GRID_W, GRID_W), kwin), :]
        t0 = rs - r + WIN_ROWS - 1
        bias = jnp.concatenate(
            [jnp.concatenate([bias_ref[2 * pair + hh, t0 + 2 * ii] for ii in range(WIN_ROWS // 2)], axis=1)
             for hh in range(2)], axis=0)
        s_s[slot, :, 0:kwin] = lax.dot_general(qq, kw, nt, preferred_element_type=F32) + bias
        s_s[slot, :, kwin:kwin + c_len] = lax.dot_general(qq, kc_ref[pair], nt, preferred_element_type=F32)

    def softmax2(s):
        p = jnp.exp2(s - s.max(axis=-1, keepdims=True))
        return p.astype(BF16), 1.0 / p.sum(axis=-1, keepdims=True)

    def probs(slot):
        p_s[slot], l_s[slot] = softmax2(s_s[slot])

    def output(pair, r, slot):
        vw = v_ref[pair, pl.ds(pl.multiple_of(win_start(r) * GRID_W, GRID_W), kwin), :]
        acc = jnp.dot(p_s[slot], jnp.concatenate([vw, vc_ref[pair]], axis=0), preferred_element_type=F32)
        o_ref[pair, pl.ds(pl.multiple_of(r * GRID_W, GRID_W), GRID_W), :] = (
            pick_heads(acc * l_s[slot]).astype(o_ref.dtype))

    def ctx_scores(pair):
        sc_s[...] = lax.dot_general(stack_heads(qc_ref[pair]), kc_ref[pair], nt, preferred_element_type=F32)

    def ctx_probs():
        pc_s[...], lc_s[...] = softmax2(sc_s[...])

    def ctx_output(pair):
        acc = jnp.dot(pc_s[...], vc_ref[pair], preferred_element_type=F32)
        oc_ref[pair] = pick_heads(acc * lc_s[...]).astype(oc_ref.dtype)

    def step(pair, blk, par, out_of=None, probs_of=True, do_scores=True):
        if out_of is None:
            out_of = (pair, blk - 2)
        if out_of is not False:
            for a in range(rps):
                output(out_of[0], out_of[1] * rps + a, par * rps + a)
        if probs_of:
            for a in range(rps):
                probs((1 - par) * rps + a)
        if do_scores:
            for a in range(rps):
                scores(pair, blk * rps + a, par * rps + a)

    for pair in range(n_pairs):
        if pair == 0:
            step(pair, 0, 0, out_of=False, probs_of=False)
            ctx_scores(pair)
            step(pair, 1, 1, out_of=False)
        else:
            step(pair, 0, 0, out_of=(pair - 1, n_blocks - 2))
            ctx_output(pair - 1)
            ctx_scores(pair)
            step(pair, 1, 1, out_of=(pair - 1, n_blocks - 1))
        ctx_probs()

        def loop_step(j, carry, pair=pair):
            step(pair, 2 * j, 0)
            step(pair, 2 * j + 1, 1)
            return carry

        lax.fori_loop(1, n_blocks // 2, loop_step, 0)

    last = n_pairs - 1
    step(last, n_blocks, 0, do_scores=False)
    ctx_output(last)
    step(last, n_blocks + 1, 1, probs_of=False, do_scores=False)


def _attention(q, k, v, qc, kc, vc, bias_tab):
    b, n_pairs, s_len, pw = q.shape
    c_len = qc.shape[2]
    pps = ATTN_PAIRS_PER_STEP
    n_rows = s_len // GRID_W
    n_keys = WIN_ROWS * GRID_W + c_len
    assert n_rows >= WIN_ROWS and n_rows % (2 * ATTN_ROWS_PER_STEP) == 0 and n_pairs % pps == 0
    lat = pl.BlockSpec((None, pps, s_len, pw), lambda bi, pi: (bi, pi, 0, 0))
    cx = pl.BlockSpec((None, pps, c_len, pw), lambda bi, pi: (bi, pi, 0, 0))
    return pl.pallas_call(
        functools.partial(_attn_body, n_rows=n_rows),
        grid=(b, n_pairs // pps),
        in_specs=[lat, lat, lat, cx, cx, cx,
                  pl.BlockSpec((2 * pps, N_DR_TILES, GRID_W, 2 * GRID_W), lambda bi, pi: (pi, 0, 0, 0))],
        out_specs=[lat, cx],
        out_shape=[jax.ShapeDtypeStruct(q.shape, BF16), jax.ShapeDtypeStruct(qc.shape, BF16)],
        scratch_shapes=[
            pltpu.VMEM((2 * ATTN_ROWS_PER_STEP, 2 * GRID_W, n_keys), F32),
            pltpu.VMEM((2 * ATTN_ROWS_PER_STEP, 2 * GRID_W, n_keys), BF16),
            pltpu.VMEM((2 * ATTN_ROWS_PER_STEP, 2 * GRID_W, 1), F32),
            pltpu.VMEM((2 * c_len, c_len), F32), pltpu.VMEM((2 * c_len, c_len), BF16),
            pltpu.VMEM((2 * c_len, 1), F32)],
        compiler_params=_cparams(("parallel", "parallel")),
    )(q, k, v, qc, kc, vc, bias_tab)


def _proj_mlp_body(x_ref, o_ref, mod_ref, g_ref, wo_ref, w1_ref, w2_ref, out_ref):
    o = jnp.concatenate([o_ref[p] for p in range(o_ref.shape[0])], axis=1)
    y = jnp.dot(o, wo_ref[...], preferred_element_type=F32)
    x = x_ref[...] + mod_ref[2:3, :] * y
    h = _norm_mod(x, g_ref[...], mod_ref[3:4, :], mod_ref[4:5, :]).astype(BF16)
    out_ref[...] = x + mod_ref[5:6, :] * _sq_relu_mlp(h, w1_ref, w2_ref)


def _proj_mlp(x, o, mods4, layer, ctx_row, g, wo, w1, w2, tm):
    b, t, d = x.shape
    d_ff = w1.shape[-1]
    mod_idx = (lambda bi, ti: (layer, bi, 0, 0)) if ctx_row is None else (lambda bi, ti: (layer, ctx_row, 0, 0))
    tok = pl.BlockSpec((None, tm, d), lambda bi, ti: (bi, ti, 0))
    const = lambda bi, ti: (0, 0)
    this_layer = lambda bi, ti: (layer, 0, 0)
    o_spec, _ = _pair_major(b, t, d, tm)
    return pl.pallas_call(
        _proj_mlp_body,
        grid=(b, t // tm),
        in_specs=[tok, o_spec, pl.BlockSpec((None, None, 6, d), mod_idx), pl.BlockSpec((1, d), const),
                  _resident((d, d), const), _resident((None, d, d_ff), this_layer),
                  _resident((None, d_ff, d), this_layer)],
        out_specs=tok,
        out_shape=jax.ShapeDtypeStruct((b, t, d), F32),
        compiler_params=_cparams(("parallel", "parallel")),
    )(x, o, mods4, g, wo, w1, w2)


def _dft_tables(n, dg):
    l2 = FFT_L2
    l1 = n // l2
    kb = SUBLANES
    ar = np.arange
    a = 2.0 * np.pi * np.outer(ar(dg), ar(dg)) / dg
    norm = 1.0 / math.sqrt(n * dg)
    cd, sd = np.cos(a) * norm, np.sin(a) * norm
    al = 2.0 * np.pi * np.outer(ar(l1), ar(l1)) / l1
    c1, s1 = np.cos(al), np.sin(al)
    m1 = np.block([[c1, -s1], [-s1, -c1]])
    k = ar(l1)[:, None, None] + l1 * ar(l2)[None, :, None]
    be = 2.0 * np.pi * (k * ar(l2)[None, None, :] % n) / n
    m2 = np.concatenate([np.cos(be), np.sin(be)], axis=-1)
    m2 = m2.reshape(l1 // kb, kb, l2, 2, l2)
    big = np.zeros((l1 // kb, l2, kb, 2, l2, kb))
    for kk in range(kb):
        big[:, :, kk, :, :, kk] = m2[:, kk]
    m2 = big.reshape(l1 // kb, l2 * kb, 2 * l2 * kb)
    return (jnp.asarray(cd, F32), jnp.asarray(sd, F32), jnp.asarray(m1, F32), jnp.asarray(m2, F32))


def _fold_body(cd_ref, sd_ref, w_ref, o_ref):
    w = w_ref[...]
    hp = lax.Precision.HIGHEST
    o_ref[0] = jnp.dot(cd_ref[...], w, precision=hp, preferred_element_type=F32).astype(BF16)
    o_ref[1] = jnp.dot(sd_ref[...], w, precision=hp, preferred_element_type=F32).astype(BF16)


def _dft1_body(x_ref, mod_ref, g_ref, w_ref, m_ref, z_ref):
    l1, nb, d = x_ref.shape
    x = x_ref[...].reshape(l1 * nb, d)
    h = _norm_mod(x, g_ref[...], mod_ref[0:1, :], mod_ref[1:2, :]).astype(BF16)

    def project(part):
        p = jnp.dot(h, w_ref[part], preferred_element_type=F32)
        return jnp.swapaxes(p.reshape(l1, nb, d), 0, 1).reshape(nb * l1, d).astype(BF16)

    pc, ps = project(0), project(1)
    m1 = m_ref[...].astype(BF16)
    zs = []
    for m in range(nb):
        rows = slice(m * l1, (m + 1) * l1)
        pm = jnp.concatenate([pc[rows], ps[rows]], axis=0)
        zs.append(jnp.dot(m1, pm, preferred_element_type=F32))
    for kblk in range(l1 // nb):
        for part in range(2):
            r0 = part * l1 + kblk * nb
            z_ref[kblk, part] = jnp.concatenate([z[r0:r0 + nb] for z in zs], axis=0).astype(BF16)


def _fourier_stage1(x, mods4, layer, g, wf, m1):
    b, s_len, d = x.shape
    l1 = s_len // FFT_L2
    nb = SUBLANES
    assert (nb * nb) % (2 * SUBLANES) == 0
    return pl.pallas_call(
        _dft1_body,
        grid=(b, FFT_L2 // nb),
        in_specs=[
            pl.BlockSpec((None, l1, nb, d), lambda bi, ti: (bi, 0, ti, 0)),
            pl.BlockSpec((None, None, 6, d), lambda bi, ti: (layer, bi, 0, 0)),
            pl.BlockSpec((1, d), lambda bi, ti: (0, 0)),
            _resident((2, d, d), lambda bi, ti: (0, 0, 0)),
            pl.BlockSpec((2 * l1, 2 * l1), lambda bi, ti: (0, 0)),
        ],
        out_specs=pl.BlockSpec((None, l1 // nb, 2, nb * nb, d), lambda bi, ti: (bi, 0, 0, ti, 0)),
        out_shape=jax.ShapeDtypeStruct((b, l1 // nb, 2, FFT_L2 * nb, d), BF16),
        compiler_params=_cparams(("parallel", "parallel")),
    )(x.reshape(b, l1, FFT_L2, d), mods4, g, wf, m1)


def _dft2_mlp_body(z_ref, x_ref, m_ref, mod_ref, g_ref, w1_ref, w2_ref, out_ref):
    l2, kb, d = x_ref.shape
    n = l2 * kb
    y = jnp.dot(m_ref[...].astype(BF16), z_ref[...].reshape(2 * n, d), preferred_element_type=F32)
    x = x_ref[...].reshape(n, d) + mod_ref[2:3, :] * y
    h = _norm_mod(x, g_ref[...], mod_ref[3:4, :], mod_ref[4:5, :]).astype(BF16)
    out = x + mod_ref[5:6, :] * _sq_relu_mlp(h, w1_ref, w2_ref)
    out_ref[...] = out.reshape(l2, kb, d)


def _fourier_stage2_mlp(z, x, mods4, layer, g, m2, w1, w2):
    b, s_len, d = x.shape
    l1 = s_len // FFT_L2
    kb = SUBLANES
    d_ff = w1.shape[-1]
    xblk = pl.BlockSpec((None, FFT_L2, kb, d), lambda bi, ki: (bi, 0, ki, 0))
    const = lambda bi, ki: (0, 0)
    this_layer = lambda bi, ki: (layer, 0, 0)
    out = pl.pallas_call(
        _dft2_mlp_body,
        grid=(b, l1 // kb),
        in_specs=[
            pl.BlockSpec((None, None, 2, FFT_L2 * kb, d), lambda bi, ki: (bi, ki, 0, 0, 0)), xblk,
            pl.BlockSpec((None, FFT_L2 * kb, 2 * FFT_L2 * kb), lambda bi, ki: (ki, 0, 0)),
            pl.BlockSpec((None, None, 6, d), lambda bi, ki: (layer, bi, 0, 0)),
            pl.BlockSpec((1, d), const),
            _resident((None, d, d_ff), this_layer), _resident((None, d_ff, d), this_layer),
        ],
        out_specs=xblk,
        out_shape=jax.ShapeDtypeStruct((b, FFT_L2, l1, d), F32),
        compiler_params=_cparams(("parallel", "parallel")),
    )(z, x.reshape(b, FFT_L2, l1, d), m2, mods4, g, w1, w2)
    return out.reshape(b, s_len, d)


def kernel(x, c, ctx, c_ctx, ada_w, ada_b, norm_g, attn_wqkv, attn_wo, q_norm_g, k_norm_g, rpb,
           fourier_w, mlp_w1, mlp_w2):
    b, s_len, d = x.shape
    depth = ada_w.shape[0]
    hd = q_norm_g.shape[-1]
    assert depth == 2 and b + 1 <= SUBLANES and 2 * hd == LANES and d % LANES == 0
    assert s_len % (FFT_L2 * SUBLANES) == 0 and s_len % TOKEN_TILE == 0
    ctx_row = b

    assert rpb.shape[0] == 1 and fourier_w.shape[0] == 1
    cc = jnp.zeros((SUBLANES, d), F32).at[:b].set(c).at[b].set(c_ctx)
    cd, sd, m1, m2 = _dft_tables(s_len, d // FOURIER_GROUPS)
    mods, bias_tab, wf = _prepare_parameters(cc, ada_w, ada_b, rpb[0], cd, sd, fourier_w[0])
    mods4 = mods.reshape(depth, SUBLANES, 6, d)

    for layer in range(depth):
        is_attn = layer % N_MIXERS == 0
        idx = layer // N_MIXERS
        need_ctx = layer < depth - 1
        g_pre = norm_g[layer, 0][None]
        g_post = norm_g[layer, 1][None]
        if is_attn:
            assert layer == 0, "the first layer's QKV kernel also casts every layer's MLP weights"
            gq2 = jnp.tile(q_norm_g[idx], 2)[None]
            gk2 = jnp.tile(k_norm_g[idx], 2)[None]
            q, k, v, wqkv, wo, w1, w2 = _qkv_and_weight_casts(
                x, mods4, layer, g_pre, gq2, gk2, attn_wqkv[idx], attn_wo[idx], mlp_w1, mlp_w2, TOKEN_TILE)
            qc, kc, vc = _qkv(ctx, mods4, layer, ctx_row, g_pre, gq2, gk2, wqkv, ctx.shape[1])
            o, oc = _attention(q, k, v, qc, kc, vc, bias_tab)
            x = _proj_mlp(x, o, mods4, layer, None, g_post, wo, w1, w2, TOKEN_TILE)
            if need_ctx:
                ctx = _proj_mlp(ctx, oc, mods4, layer, ctx_row, g_post, wo, w1, w2, ctx.shape[1])
        else:
            assert not need_ctx, "a Fourier layer that still feeds a context stream is not supported"
            z = _fourier_stage1(x, mods4, layer, g_pre, wf, m1)
            x = _fourier_stage2_mlp(z, x, mods4, layer, g_post, m2, w1, w2)
    return x
```
